```python
import jax
import jax.numpy as jnp
from jax import lax
import numpy as np

D_MODEL = 4096
BATCH = 1
SEQ = 8192
DEPTH = 4

GRID_W = 64
CTX_LEN = 256
HEAD_DIM = 128
BRANCH_W = 1024
N_BRANCHES = 3
CHUNK = 128
GMLP_GROUPS = 8
GMLP_GROUP_DIM = BRANCH_W // GMLP_GROUPS
NA_HEADS = BRANCH_W // HEAD_DIM
NA_WIN_R = 8
NA_WIN_C = 16
SWA_HEADS = BRANCH_W // HEAD_DIM
SWA_KV_HEADS = 2
SWA_KV_W = SWA_KV_HEADS * HEAD_DIM
SWA_WINDOW = 128
SWA_BLOCK = 128
ROPE_BASE = 10000.0
FFN_DENSE = 2048
N_EXPERTS = 8
TOP_K = 2
FFN_EXPERT = 384
N_DENSE_LAYERS = (DEPTH + 1) // 2
N_MOE_LAYERS = DEPTH // 2

OFF_AU = 0
OFF_AV = OFF_AU + BRANCH_W
OFF_BQ = OFF_AV + BRANCH_W
OFF_CQ = OFF_BQ + BRANCH_W
OFF_GATE = OFF_CQ + BRANCH_W
OFF_KV = OFF_GATE + N_BRANCHES * D_MODEL
OFF_BK = OFF_KV
OFF_BV = OFF_BK + BRANCH_W
OFF_CK = OFF_BV + BRANCH_W
OFF_CV = OFF_CK + SWA_KV_W
P_TOTAL = OFF_CV + SWA_KV_W
NEG_INF = -1e30

kernel_name = "hybrid_gmlp_natten_swa_moe_dit"


def rmsnorm(x, g, eps=1e-6):
    xf = x.astype(jnp.float32)
    y = xf * lax.rsqrt(jnp.mean(xf * xf, axis=-1, keepdims=True) + eps)
    return y.astype(x.dtype) * g


def layernorm(x, g, b, eps=1e-6):
    xf = x.astype(jnp.float32)
    mu = jnp.mean(xf, axis=-1, keepdims=True)
    var = jnp.mean(jnp.square(xf - mu), axis=-1, keepdims=True)
    return ((xf - mu) * lax.rsqrt(var + eps)).astype(x.dtype) * g + b


def modulate(h, shift, scale):
    return h * (1.0 + scale) + shift


def _heads(t):
    return t.reshape(t.shape[0], t.shape[1], -1, HEAD_DIM)


def _rope_1d(x, pos):
    m = x.shape[-1]
    inv = ROPE_BASE ** (-jnp.arange(0, m, 2, dtype=jnp.float32) / m)
    ang = pos.astype(jnp.float32)[:, None] * inv[None, :]
    cos, sin = jnp.cos(ang)[:, None, :], jnp.sin(ang)[:, None, :]
    xf = x.astype(jnp.float32)
    x1, x2 = xf[..., : m // 2], xf[..., m // 2:]
    return jnp.concatenate([x1 * cos - x2 * sin, x2 * cos + x1 * sin], axis=-1).astype(x.dtype)


def axial_rope(x, pos_r, pos_c):
    h = x.shape[-1] // 2
    return jnp.concatenate([_rope_1d(x[..., :h], pos_r), _rope_1d(x[..., h:], pos_c)], axis=-1)


def chunk_gmlp(u, v, ln_g, ln_b, w_s, b_s):
    bn, n, w = u.shape
    u = jax.nn.gelu(u)
    v = layernorm(jax.nn.gelu(v), ln_g, ln_b)
    v = v.reshape(bn, n // CHUNK, CHUNK, GMLP_GROUPS, GMLP_GROUP_DIM)
    s = jnp.einsum('gpq,bnqgc->bnpgc', w_s, v) + b_s.T[:, :, None]
    return u * s.reshape(bn, n, w)


def neighbourhood_attention(q, k, v, k_ctx, v_ctx, rpb):
    bn, n, h, d = q.shape
    rows = n // GRID_W
    kr = min(NA_WIN_R, rows)
    r = jnp.arange(rows)
    key_rows = jnp.clip(r - kr // 2, 0, rows - kr)[:, None] + jnp.arange(kr)[None, :]
    col = jnp.arange(GRID_W)
    c0 = jnp.clip(col - NA_WIN_C // 2, 0, GRID_W - NA_WIN_C)
    in_win = (col[None, :] >= c0[:, None]) & (col[None, :] < c0[:, None] + NA_WIN_C)
    qg = q.reshape(bn, rows, GRID_W, h, d)
    kg = k.reshape(bn, rows, GRID_W, h, d)[:, key_rows]
    vg = v.reshape(bn, rows, GRID_W, h, d)[:, key_rows]
    scale = d ** -0.5
    s_loc = jnp.einsum('brqhd,brjkhd->brhqjk', qg, kg, preferred_element_type=jnp.float32) * scale
    roff = key_rows - r[:, None] + (NA_WIN_R - 1)
    coff = jnp.clip(col[None, :] - col[:, None] + (NA_WIN_C - 1), 0, 2 * NA_WIN_C - 2)
    bias = rpb[:, roff[:, :, None, None], coff[None, None]].transpose(1, 0, 3, 2, 4)
    s_loc = jnp.where(in_win[:, None, :], s_loc + bias[None].astype(jnp.float32), NEG_INF)
    s_loc = s_loc.reshape(bn, rows, h, GRID_W, kr * GRID_W)
    s_ctx = jnp.einsum('brqhd,bchd->brhqc', qg, k_ctx, preferred_element_type=jnp.float32) * scale
    p = jax.nn.softmax(jnp.concatenate([s_loc, s_ctx], axis=-1), axis=-1)
    p_loc = p[..., : kr * GRID_W].reshape(bn, rows, h, GRID_W, kr, GRID_W).astype(v.dtype)
    p_ctx = p[..., kr * GRID_W:].astype(v.dtype)
    o = jnp.einsum('brhqjk,brjkhd->brqhd', p_loc, vg) + jnp.einsum('brhqc,bchd->brqhd', p_ctx, v_ctx)
    return o.reshape(bn, n, h * d)


def window_gqa(q, k, v, k_ctx, v_ctx, sink):
    bn, n, hq, d = q.shape
    hk = k.shape[2]
    g = hq // hk
    blk = SWA_BLOCK
    nb = n // blk
    pad = ((0, 0), (blk, blk), (0, 0), (0, 0))
    kp = jnp.pad(k, pad).reshape(bn, nb + 2, blk, hk, d)
    vp = jnp.pad(v, pad).reshape(bn, nb + 2, blk, hk, d)
    kw = jnp.concatenate([kp[:, :-2], kp[:, 1:-1], kp[:, 2:]], axis=2)
    vw = jnp.concatenate([vp[:, :-2], vp[:, 1:-1], vp[:, 2:]], axis=2)
    qb = q.reshape(bn, nb, blk, hk, g, d)
    scale = d ** -0.5
    s = jnp.einsum('bnqkgd,bnskd->bnkgqs', qb, kw, preferred_element_type=jnp.float32) * scale
    a = jnp.arange(blk)
    j = jnp.arange(3 * blk)
    rel = (j[None, :] - blk) - a[:, None]
    kpos = jnp.arange(nb)[:, None] * blk - blk + j[None, :]
    valid = (jnp.abs(rel) <= SWA_WINDOW)[None] & ((kpos >= 0) & (kpos < n))[:, None, :]
    s = jnp.where(valid[:, None, None], s, NEG_INF)
    s_ctx = jnp.einsum('bnqkgd,bckd->bnkgqc', qb, k_ctx, preferred_element_type=jnp.float32) * scale
    sink_col = jnp.broadcast_to(sink.reshape(hk, g)[:, :, None, None].astype(jnp.float32), s.shape[:-1] + (1,))
    p = jax.nn.softmax(jnp.concatenate([s, s_ctx, sink_col], axis=-1), axis=-1)
    n_ctx = k_ctx.shape[1]
    p_loc = p[..., : 3 * blk].astype(v.dtype)
    p_ctx = p[..., 3 * blk: 3 * blk + n_ctx].astype(v.dtype)
    o = jnp.einsum('bnkgqs,bnskd->bnqkgd', p_loc, vw) + jnp.einsum('bnkgqc,bckd->bnqkgd', p_ctx, v_ctx)
    return o.reshape(bn, n, hq * d)


def context_attention(q, k, v, sink):
    bn, n, hq, d = q.shape
    hk = k.shape[2]
    g = hq // hk
    qg = q.reshape(bn, n, hk, g, d)
    s = jnp.einsum('bqkgd,bskd->bkgqs', qg, k, preferred_element_type=jnp.float32) * (d ** -0.5)
    n_keys = k.shape[1]
    if sink is not None:
        sink_col = jnp.broadcast_to(sink.reshape(hk, g)[:, :, None, None].astype(jnp.float32), s.shape[:-1] + (1,))
        s = jnp.concatenate([s, sink_col], axis=-1)
    p = jax.nn.softmax(s, axis=-1)[..., :n_keys].astype(v.dtype)
    o = jnp.einsum('bkgqs,bskd->bqkgd', p, v)
    return o.reshape(bn, n, hq * d)


def merge_branches(ys, p_gate, w_branch, w_out):
    gates = jax.nn.sigmoid(p_gate.astype(jnp.float32)).astype(p_gate.dtype)
    acc = gates[..., :D_MODEL] * (ys[0] @ w_branch[0])
    for i in range(1, N_BRANCHES):
        acc = acc + gates[..., i * D_MODEL:(i + 1) * D_MODEL] * (ys[i] @ w_branch[i])
    return acc @ w_out


def swiglu(h, w_gate, w_up, w_down):
    return (jax.nn.silu(h @ w_gate) * (h @ w_up)) @ w_down


def moe_swiglu(h, w_router, w_gate, w_up, w_down):
    logits = jnp.einsum('btd,de->bte', h, w_router, preferred_element_type=jnp.float32)
    top_v, top_i = lax.top_k(logits, TOP_K)
    wts = jax.nn.softmax(top_v, axis=-1)
    gate = jnp.sum(jax.nn.one_hot(top_i, N_EXPERTS, dtype=jnp.float32) * wts[..., None], axis=-2)
    hid = jax.nn.silu(jnp.einsum('btd,edf->btef', h, w_gate)) * jnp.einsum('btd,edf->btef', h, w_up)
    hid = hid * gate.astype(h.dtype)[..., None]
    return jnp.einsum('btef,efd->btd', hid, w_down)


def channel_mixer(h, l, w_ffn_gate, w_ffn_up, w_ffn_down, w_router, w_exp_gate, w_exp_up, w_exp_down):
    i = l // 2
    if l % 2 == 0:
        return swiglu(h, w_ffn_gate[i], w_ffn_up[i], w_ffn_down[i])
    return moe_swiglu(h, w_router[i], w_exp_gate[i], w_exp_up[i], w_exp_down[i])


def setup_inputs(seed: int = 0) -> dict:
    key = jax.random.key(seed)
    ks = jax.random.split(key, 32)
    f32 = jnp.float32
    L, D = DEPTH, D_MODEL

    def nrm(k, shape, scale):
        return jax.random.normal(k, shape, f32) * scale

    return {
        "x": nrm(ks[0], (BATCH, SEQ, D), 1.0),
        "c": nrm(ks[1], (BATCH, D), 1.0),
        "ctx": nrm(ks[2], (BATCH, CTX_LEN, D), 1.0),
        "c_ctx": nrm(ks[3], (D,), 1.0),
        "w_ada": nrm(ks[4], (L, D, 6 * D), 0.5 * D ** -0.5),
        "b_ada": nrm(ks[5], (L, 6 * D), 0.02),
        "g_mix": 1.0 + nrm(ks[6], (L, D), 0.05),
        "w_in": nrm(ks[7], (L, D, P_TOTAL), D ** -0.5),
        "gmlp_ln_g": 1.0 + nrm(ks[8], (L, BRANCH_W), 0.05),
        "gmlp_ln_b": nrm(ks[9], (L, BRANCH_W), 0.02),
        "gmlp_ws": nrm(ks[10], (L, GMLP_GROUPS, CHUNK, CHUNK), CHUNK ** -0.5),
        "gmlp_bs": 1.0 + nrm(ks[11], (L, GMLP_GROUPS, CHUNK), 0.05),
        "na_rpb": nrm(ks[12], (L, NA_HEADS, 2 * NA_WIN_R - 1, 2 * NA_WIN_C - 1), 0.1),
        "swa_sink": nrm(ks[13], (L, SWA_HEADS), 1.0),
        "w_branch": nrm(ks[14], (L, N_BRANCHES, BRANCH_W, D), BRANCH_W ** -0.5),
        "w_out": nrm(ks[15], (L, D, D), D ** -0.5),
        "g_ffn": 1.0 + nrm(ks[16], (L, D), 0.05),
        "w_ffn_gate": nrm(ks[17], (N_DENSE_LAYERS, D, FFN_DENSE), D ** -0.5),
        "w_ffn_up": nrm(ks[18], (N_DENSE_LAYERS, D, FFN_DENSE), D ** -0.5),
        "w_ffn_down": nrm(ks[19], (N_DENSE_LAYERS, FFN_DENSE, D), FFN_DENSE ** -0.5),
        "w_router": nrm(ks[20], (N_MOE_LAYERS, D, N_EXPERTS), D ** -0.5),
        "w_exp_gate": nrm(ks[21], (N_MOE_LAYERS, N_EXPERTS, D, FFN_EXPERT), D ** -0.5),
        "w_exp_up": nrm(ks[22], (N_MOE_LAYERS, N_EXPERTS, D, FFN_EXPERT), D ** -0.5),
        "w_exp_down": nrm(ks[23], (N_MOE_LAYERS, N_EXPERTS, FFN_EXPERT, D), FFN_EXPERT ** -0.5),
        "g_final": 1.0 + nrm(ks[24], (D,), 0.05),
    }


def reference(x, c, ctx, c_ctx, w_ada, b_ada, g_mix, w_in, gmlp_ln_g, gmlp_ln_b, gmlp_ws, gmlp_bs, na_rpb,
              swa_sink, w_branch, w_out, g_ffn, w_ffn_gate, w_ffn_up, w_ffn_down, w_router, w_exp_gate,
              w_exp_up, w_exp_down, g_final):
    n_lat = x.shape[1]
    t = jnp.arange(n_lat, dtype=jnp.int32)
    pos_r, pos_c = t // GRID_W, t % GRID_W
    cs = jax.nn.silu(c)
    ccs = jax.nn.silu(c_ctx)
    for l in range(DEPTH):
        last = l == DEPTH - 1
        mod_x = jnp.split((cs @ w_ada[l] + b_ada[l])[:, None, :], 6, axis=-1)
        mod_c = jnp.split((ccs @ w_ada[l] + b_ada[l])[None, None, :], 6, axis=-1)

        hx = modulate(rmsnorm(x, g_mix[l]), mod_x[0], mod_x[1])
        hc = modulate(rmsnorm(ctx, g_mix[l]), mod_c[0], mod_c[1])
        px = hx @ w_in[l]
        pc = hc @ (w_in[l][:, OFF_KV:] if last else w_in[l])
        pc_kv = pc if last else pc[..., OFF_KV:]
        kb_ctx = _heads(pc_kv[..., :BRANCH_W])
        vb_ctx = _heads(pc_kv[..., BRANCH_W:2 * BRANCH_W])
        kc_ctx = _heads(pc_kv[..., 2 * BRANCH_W:2 * BRANCH_W + SWA_KV_W])
        vc_ctx = _heads(pc_kv[..., 2 * BRANCH_W + SWA_KV_W:])

        y_a = chunk_gmlp(px[..., OFF_AU:OFF_AV], px[..., OFF_AV:OFF_BQ], gmlp_ln_g[l], gmlp_ln_b[l],
                         gmlp_ws[l], gmlp_bs[l])
        y_b = neighbourhood_attention(_heads(px[..., OFF_BQ:OFF_CQ]), _heads(px[..., OFF_BK:OFF_BV]),
                                      _heads(px[..., OFF_BV:OFF_CK]), kb_ctx, vb_ctx, na_rpb[l])
        q_c = axial_rope(_heads(px[..., OFF_CQ:OFF_GATE]), pos_r, pos_c)
        k_c = axial_rope(_heads(px[..., OFF_CK:OFF_CV]), pos_r, pos_c)
        y_c = window_gqa(q_c, k_c, _heads(px[..., OFF_CV:P_TOTAL]), kc_ctx, vc_ctx, swa_sink[l])
        mix_x = merge_branches((y_a, y_b, y_c), px[..., OFF_GATE:OFF_KV], w_branch[l], w_out[l])
        x = x + mod_x[2] * mix_x

        if not last:
            yc_a = chunk_gmlp(pc[..., OFF_AU:OFF_AV], pc[..., OFF_AV:OFF_BQ], gmlp_ln_g[l], gmlp_ln_b[l],
                              gmlp_ws[l], gmlp_bs[l])
            yc_b = context_attention(_heads(pc[..., OFF_BQ:OFF_CQ]), kb_ctx, vb_ctx, None)
            yc_c = context_attention(_heads(pc[..., OFF_CQ:OFF_GATE]), kc_ctx, vc_ctx, swa_sink[l])
            mix_c = merge_branches((yc_a, yc_b, yc_c), pc[..., OFF_GATE:OFF_KV], w_branch[l], w_out[l])
            ctx = ctx + mod_c[2] * mix_c

        hx = modulate(rmsnorm(x, g_ffn[l]), mod_x[3], mod_x[4])
        x = x + mod_x[5] * channel_mixer(hx, l, w_ffn_gate, w_ffn_up, w_ffn_down, w_router, w_exp_gate,
                                          w_exp_up, w_exp_down)
        if not last:
            hc = modulate(rmsnorm(ctx, g_ffn[l]), mod_c[3], mod_c[4])
            ctx = ctx + mod_c[5] * channel_mixer(hc, l, w_ffn_gate, w_ffn_up, w_ffn_down, w_router,
                                                  w_exp_gate, w_exp_up, w_exp_down)
    return rmsnorm(x, g_final)
```

```python
import functools

import numpy as np
import jax
import jax.numpy as jnp
from jax import lax
from jax.experimental import pallas as pl
from jax.experimental.pallas import tpu as pltpu

F32 = jnp.float32
BF16 = jnp.bfloat16

GRID_W = 64
HEAD_DIM = 128
BRANCH_W = 1024
N_BRANCHES = 3
CHUNK = 128
GMLP_GROUPS = 8
NA_HEADS = BRANCH_W // HEAD_DIM
NA_WIN_R = 8
NA_WIN_C = 16
SWA_HEADS = BRANCH_W // HEAD_DIM
SWA_KV_HEADS = 2
SWA_GROUP = SWA_HEADS // SWA_KV_HEADS
SWA_KV_W = SWA_KV_HEADS * HEAD_DIM
SWA_WINDOW = 128
SWA_BLOCK = 128
ROPE_BASE = 10000.0
N_EXPERTS = 8
TOP_K = 2
NEG_INF = -1e30
EPS = 1e-6

NA_QROWS = 4
NA_KROWS = 12
NA_QB = NA_QROWS * GRID_W
NA_KB = NA_KROWS * GRID_W
NA_PIECES = NA_KB // NA_QB

ROUTER_LANES = 128

MIB = 1024 * 1024


def _params(sem, vmem_mib):
    return pltpu.CompilerParams(dimension_semantics=sem, vmem_limit_bytes=vmem_mib * MIB)


def _ada_kernel(c_ref, w_ref, b_ref, o_ref):
    cs = c_ref[...]
    cs = (cs * jax.nn.sigmoid(cs)).astype(BF16)
    w = w_ref[...].astype(BF16)
    o_ref[...] = jnp.dot(cs, w, preferred_element_type=F32) + b_ref[...]


def _ada_mods(c8, w_ada, b_ada, tn=512):
    depth, d, n6 = w_ada.shape
    return pl.pallas_call(
        _ada_kernel,
        grid=(depth, n6 // tn),
        in_specs=[
            pl.BlockSpec((8, d), lambda l, j: (0, 0)),
            pl.BlockSpec((None, d, tn), lambda l, j: (l, 0, j)),
            pl.BlockSpec((None, 1, tn), lambda l, j: (l, 0, j)),
        ],
        out_specs=pl.BlockSpec((None, 8, tn), lambda l, j: (l, 0, j)),
        out_shape=jax.ShapeDtypeStruct((depth, 8, n6), F32),
        compiler_params=_params(("arbitrary", "arbitrary"), 40),
        name="ada_mods",
    )(c8, w_ada, b_ada.reshape(depth, 1, n6))


def _norm_mod(x, g_ref, sh_ref, sc_ref):
    ms = jnp.mean(x * x, axis=-1, keepdims=True)
    y = x * lax.rsqrt(ms + EPS) * g_ref[...]
    return y * (1.0 + sc_ref[...]) + sh_ref[...]


def _norm_kernel(x_ref, g_ref, sh_ref, sc_ref, o_ref):
    o_ref[...] = _norm_mod(x_ref[...], g_ref, sh_ref, sc_ref).astype(o_ref.dtype)


def _norm_router_kernel(x_ref, g_ref, sh_ref, sc_ref, wr_ref, o_ref, gate_ref):
    h = _norm_mod(x_ref[...], g_ref, sh_ref, sc_ref)
    o_ref[...] = h.astype(o_ref.dtype)
    logits = jnp.dot(h, wr_ref[...], preferred_element_type=F32, precision=lax.Precision.HIGHEST)
    lane = lax.broadcasted_iota(jnp.int32, logits.shape, 1)
    logits = jnp.where(lane < N_EXPERTS, logits, -jnp.inf)
    m1 = jnp.max(logits, axis=-1, keepdims=True)
    i1 = jnp.min(jnp.where(logits == m1, lane, ROUTER_LANES), axis=-1, keepdims=True)
    first = lane == i1
    rest = jnp.where(first, -jnp.inf, logits)
    m2 = jnp.max(rest, axis=-1, keepdims=True)
    i2 = jnp.min(jnp.where(rest == m2, lane, ROUTER_LANES), axis=-1, keepdims=True)
    second = lane == i2
    e2 = jnp.exp(m2 - m1)
    w1 = 1.0 / (1.0 + e2)
    gate_ref[...] = jnp.where(first, w1, 0.0) + jnp.where(second, e2 * w1, 0.0)


def _final_norm_kernel(x_ref, g_ref, o_ref):
    x = x_ref[...]
    ms = jnp.mean(x * x, axis=-1, keepdims=True)
    o_ref[...] = x * lax.rsqrt(ms + EPS) * g_ref[...]


def _norm_call(x, g, mods, l, which, n_lat, w_router=None, tr=256):
    m, d = x.shape
    lat_tiles = n_lat // tr
    row_kind = lambda i: jnp.where(i >= lat_tiles, 1, 0)
    mod_spec = lambda k: pl.BlockSpec((None, None, None, 1, d), lambda i: (l, row_kind(i), k, 0, 0))
    in_specs = [
        pl.BlockSpec((tr, d), lambda i: (i, 0)),
        pl.BlockSpec((None, 1, d), lambda i: (l, 0, 0)),
        mod_spec(which),
        mod_spec(which + 1),
    ]
    args = [x, g.reshape(g.shape[0], 1, d), mods, mods]
    if w_router is None:
        return pl.pallas_call(
            _norm_kernel,
            grid=(m // tr,),
            in_specs=in_specs,
            out_specs=pl.BlockSpec((tr, d), lambda i: (i, 0)),
            out_shape=jax.ShapeDtypeStruct((m, d), BF16),
            compiler_params=_params(("arbitrary",), 40),
            name="norm_mod",
        )(*args)
    in_specs.append(pl.BlockSpec((d, ROUTER_LANES), lambda i: (0, 0)))
    args.append(w_router)
    return pl.pallas_call(
        _norm_router_kernel,
        grid=(m // tr,),
        in_specs=in_specs,
        out_specs=[pl.BlockSpec((tr, d), lambda i: (i, 0)),
                   pl.BlockSpec((tr, ROUTER_LANES), lambda i: (i, 0))],
        out_shape=[jax.ShapeDtypeStruct((m, d), BF16),
                   jax.ShapeDtypeStruct((m, ROUTER_LANES), F32)],
        compiler_params=_params(("arbitrary",), 48),
        name="norm_mod_router",
    )(*args)


def _final_norm_call(x, g, n_lat, tr=256):
    d = x.shape[1]
    return pl.pallas_call(
        _final_norm_kernel,
        grid=(n_lat // tr,),
        in_specs=[pl.BlockSpec((tr, d), lambda i: (i, 0)),
                  pl.BlockSpec((1, d), lambda i: (0, 0))],
        out_specs=pl.BlockSpec((tr, d), lambda i: (i, 0)),
        out_shape=jax.ShapeDtypeStruct((n_lat, d), F32),
        compiler_params=_params(("arbitrary",), 40),
        name="final_norm",
    )(x, g.reshape(1, d))


def _mm_kernel(a_ref, w_ref, o_ref, wb_ref):
    @pl.when(pl.program_id(1) == 0)
    def _():
        wb_ref[...] = w_ref[...].astype(BF16)

    o_ref[...] = jnp.dot(a_ref[...], wb_ref[...], preferred_element_type=F32).astype(o_ref.dtype)


def _mm_call(a, w, l, tm, tn):
    m, k = a.shape
    n = w.shape[2]
    return pl.pallas_call(
        _mm_kernel,
        grid=(n // tn, m // tm),
        in_specs=[pl.BlockSpec((tm, k), lambda j, i: (i, 0)),
                  pl.BlockSpec((None, k, tn), lambda j, i: (l, 0, j))],
        out_specs=pl.BlockSpec((tm, tn), lambda j, i: (i, j)),
        out_shape=jax.ShapeDtypeStruct((m, n), BF16),
        scratch_shapes=[pltpu.VMEM((k, tn), BF16)],
        compiler_params=_params(("arbitrary", "arbitrary"), 48),
        name="in_proj",
    )(a, w)


def _mm_res_kernel(a_ref, w_ref, x_ref, gx_ref, gc_ref, o_ref, wb_ref, *, n_lat, tm):
    i = pl.program_id(1)

    @pl.when(i == 0)
    def _():
        wb_ref[...] = w_ref[...].astype(BF16)

    y = jnp.dot(a_ref[...], wb_ref[...], preferred_element_type=F32)
    row = i * tm + lax.broadcasted_iota(jnp.int32, (tm, 1), 0)
    gate = jnp.where(row < n_lat, gx_ref[...], gc_ref[...])
    o_ref[...] = x_ref[...] + gate * y


def _mm_res_call(a, w, l, x, mods, lm, which, n_lat, tm, tn):
    m, k = a.shape
    d = w.shape[2]
    gate_spec = lambda kind: pl.BlockSpec((None, None, None, 1, tn), lambda j, i: (lm, kind, which, 0, j))
    return pl.pallas_call(
        functools.partial(_mm_res_kernel, n_lat=n_lat, tm=tm),
        grid=(d // tn, m // tm),
        in_specs=[pl.BlockSpec((tm, k), lambda j, i: (i, 0)),
                  pl.BlockSpec((None, k, tn), lambda j, i: (l, 0, j)),
                  pl.BlockSpec((tm, tn), lambda j, i: (i, j)),
                  gate_spec(0), gate_spec(1)],
        out_specs=pl.BlockSpec((tm, tn), lambda j, i: (i, j)),
        out_shape=jax.ShapeDtypeStruct((m, d), F32),
        scratch_shapes=[pltpu.VMEM((k, tn), BF16)],
        compiler_params=_params(("arbitrary", "arbitrary"), 48),
        name="proj_residual",
    )(a, w, x, mods, mods)


def _merge_kernel(ya_ref, yb_ref, yc_ref, ga_ref, gb_ref, gc_ref, w_ref, o_ref, wb_ref):
    @pl.when(pl.program_id(1) == 0)
    def _():
        wb_ref[...] = w_ref[...].astype(BF16)

    acc = None
    for br, (y_ref, g_ref) in enumerate(((ya_ref, ga_ref), (yb_ref, gb_ref), (yc_ref, gc_ref))):
        t = jax.nn.sigmoid(g_ref[...].astype(F32)) * jnp.dot(y_ref[...], wb_ref[br],
                                                             preferred_element_type=F32)
        acc = t if acc is None else acc + t
    o_ref[...] = acc.astype(o_ref.dtype)


def _merge_call(ya, yb, yc, px, w_branch, l, off_gate, tm, tn):
    m, bw = ya.shape
    d = w_branch.shape[3]
    y_spec = pl.BlockSpec((tm, bw), lambda j, i: (i, 0))
    g_spec = lambda br: pl.BlockSpec((tm, tn), lambda j, i: (i, (off_gate + br * d) // tn + j))
    return pl.pallas_call(
        _merge_kernel,
        grid=(d // tn, m // tm),
        in_specs=[y_spec, y_spec, y_spec, g_spec(0), g_spec(1), g_spec(2),
                  pl.BlockSpec((None, N_BRANCHES, bw, tn), lambda j, i: (l, 0, 0, j))],
        out_specs=pl.BlockSpec((tm, tn), lambda j, i: (i, j)),
        out_shape=jax.ShapeDtypeStruct((m, d), BF16),
        scratch_shapes=[pltpu.VMEM((N_BRANCHES, bw, tn), BF16)],
        compiler_params=_params(("arbitrary", "arbitrary"), 48),
        name="branch_merge",
    )(ya, yb, yc, px, px, px, w_branch)


def _ffn_up_kernel(a_ref, wg_ref, wu_ref, o_ref, wgb_ref, wub_ref):
    @pl.when(pl.program_id(1) == 0)
    def _():
        wgb_ref[...] = wg_ref[...].astype(BF16)
        wub_ref[...] = wu_ref[...].astype(BF16)

    a = a_ref[...]
    hg = jnp.dot(a, wgb_ref[...], preferred_element_type=F32)
    hu = jnp.dot(a, wub_ref[...], preferred_element_type=F32)
    o_ref[...] = (hg * jax.nn.sigmoid(hg) * hu).astype(o_ref.dtype)


def _moe_up_kernel(a_ref, wg_ref, wu_ref, gate_ref, o_ref, wgb_ref, wub_ref):
    e = pl.program_id(0)

    @pl.when(pl.program_id(1) == 0)
    def _():
        wgb_ref[...] = wg_ref[...].astype(BF16)
        wub_ref[...] = wu_ref[...].astype(BF16)

    a = a_ref[...]
    hg = jnp.dot(a, wgb_ref[...], preferred_element_type=F32)
    hu = jnp.dot(a, wub_ref[...], preferred_element_type=F32)
    gates = gate_ref[...]
    lane = lax.broadcasted_iota(jnp.int32, gates.shape, 1)
    ge = jnp.sum(jnp.where(lane == e, gates, 0.0), axis=-1, keepdims=True)
    o_ref[...] = (hg * jax.nn.sigmoid(hg) * hu * ge).astype(o_ref.dtype)


def _ffn_up_call(a, wg, wu, i_layer, tm, tn):
    m, k = a.shape
    f = wg.shape[2]
    w_spec = pl.BlockSpec((None, k, tn), lambda j, i: (i_layer, 0, j))
    return pl.pallas_call(
        _ffn_up_kernel,
        grid=(f // tn, m // tm),
        in_specs=[pl.BlockSpec((tm, k), lambda j, i: (i, 0)), w_spec, w_spec],
        out_specs=pl.BlockSpec((tm, tn), lambda j, i: (i, j)),
        out_shape=jax.ShapeDtypeStruct((m, f), BF16),
        scratch_shapes=[pltpu.VMEM((k, tn), BF16), pltpu.VMEM((k, tn), BF16)],
        compiler_params=_params(("arbitrary", "arbitrary"), 52),
        name="ffn_up",
    )(a, wg, wu)


def _moe_up_call(a, wg, wu, gates, i_layer, tm):
    m, k = a.shape
    n_exp, fe = wg.shape[1], wg.shape[3]
    w_spec = pl.BlockSpec((None, None, k, fe), lambda e, i: (i_layer, e, 0, 0))
    return pl.pallas_call(
        _moe_up_kernel,
        grid=(n_exp, m // tm),
        in_specs=[pl.BlockSpec((tm, k), lambda e, i: (i, 0)), w_spec, w_spec,
                  pl.BlockSpec((tm, ROUTER_LANES), lambda e, i: (i, 0))],
        out_specs=pl.BlockSpec((tm, fe), lambda e, i: (i, e)),
        out_shape=jax.ShapeDtypeStruct((m, n_exp * fe), BF16),
        scratch_shapes=[pltpu.VMEM((k, fe), BF16), pltpu.VMEM((k, fe), BF16)],
        compiler_params=_params(("arbitrary", "arbitrary"), 52),
        name="moe_up",
    )(a, wg, wu, gates)


def _gmlp_kernel(u_ref, v_ref, lng_ref, lnb_ref, ws_ref, bs_ref, o_ref, *, chunks):
    u = jax.nn.gelu(u_ref[...].astype(F32))
    v = jax.nn.gelu(v_ref[...].astype(F32))
    mu = jnp.mean(v, axis=-1, keepdims=True)
    var = jnp.mean(jnp.square(v - mu), axis=-1, keepdims=True)
    vn = ((v - mu) * lax.rsqrt(var + EPS) * lng_ref[...] + lnb_ref[...]).astype(BF16)
    for g in range(GMLP_GROUPS):
        cols = slice(g * CHUNK, (g + 1) * CHUNK)
        wsg = ws_ref[g].astype(BF16)
        for c in range(chunks):
            rows = slice(c * CHUNK, (c + 1) * CHUNK)
            s = jnp.dot(wsg, vn[rows, cols], preferred_element_type=F32) + bs_ref[:, cols]
            o_ref[rows, cols] = (u[rows, cols] * s).astype(o_ref.dtype)


def _gmlp_call(px, lng, lnb, ws, bs_full, l, chunks):
    m = px.shape[0]
    t = chunks * CHUNK
    return pl.pallas_call(
        functools.partial(_gmlp_kernel, chunks=chunks),
        grid=(m // t,),
        in_specs=[pl.BlockSpec((t, BRANCH_W), lambda i: (i, 0)),
                  pl.BlockSpec((t, BRANCH_W), lambda i: (i, 1)),
                  pl.BlockSpec((None, 1, BRANCH_W), lambda i: (l, 0, 0)),
                  pl.BlockSpec((None, 1, BRANCH_W), lambda i: (l, 0, 0)),
                  pl.BlockSpec((None, GMLP_GROUPS, CHUNK, CHUNK), lambda i: (l, 0, 0, 0)),
                  pl.BlockSpec((None, CHUNK, BRANCH_W), lambda i: (l, 0, 0))],
        out_specs=pl.BlockSpec((t, BRANCH_W), lambda i: (i, 0)),
        out_shape=jax.ShapeDtypeStruct((m, BRANCH_W), BF16),
        compiler_params=_params(("arbitrary",), 40),
        name="gmlp",
    )(px, px, lng.reshape(-1, 1, BRANCH_W), lnb.reshape(-1, 1, BRANCH_W), ws, bs_full)


def _dot_nt(a, b):
    return lax.dot_general(a, b, (((1,), (1,)), ((), ())), preferred_element_type=F32)


def _na_kernel(q_ref, k0_ref, k1_ref, k2_ref, v0_ref, v1_ref, v2_ref, kc_ref, vc_ref, bias_ref, o_ref):
    q = q_ref[...]
    scale = HEAD_DIM ** -0.5
    k_refs = (k0_ref, k1_ref, k2_ref)
    v_refs = (v0_ref, v1_ref, v2_ref)
    s = [_dot_nt(q, k_refs[j][...]) * scale + bias_ref[:, j * NA_QB:(j + 1) * NA_QB]
         for j in range(NA_PIECES)]
    s.append(_dot_nt(q, kc_ref[...]) * scale)
    m = functools.reduce(jnp.maximum, [jnp.max(t, axis=-1, keepdims=True) for t in s])
    p = [jnp.exp(t - m) for t in s]
    denom = functools.reduce(lambda a, b: a + b, [jnp.sum(t, axis=-1, keepdims=True) for t in p])
    o = jnp.dot(p[NA_PIECES].astype(BF16), vc_ref[...], preferred_element_type=F32)
    for j in range(NA_PIECES):
        o = o + jnp.dot(p[j].astype(BF16), v_refs[j][...], preferred_element_type=F32)
    o_ref[...] = (o / denom).astype(o_ref.dtype)


def _na_call(px, bias, n_lat, off_q, off_k, off_v):
    m = px.shape[0]
    nb = n_lat // NA_QB
    ctx_blk = n_lat // NA_QB
    hb = HEAD_DIM
    kstart = lambda b: jnp.clip(b - 1, 0, nb - NA_PIECES)
    k_spec = lambda off, j: pl.BlockSpec((NA_QB, hb), lambda h, b: (kstart(b) + j, off // hb + h))
    kind = lambda b: jnp.where(b == 0, 0, jnp.where(b == nb - 1, 2, jnp.where(b == nb, 3, 1)))
    return pl.pallas_call(
        _na_kernel,
        grid=(NA_HEADS, m // NA_QB),
        in_specs=[pl.BlockSpec((NA_QB, hb), lambda h, b: (b, off_q // hb + h)),
                  k_spec(off_k, 0), k_spec(off_k, 1), k_spec(off_k, 2),
                  k_spec(off_v, 0), k_spec(off_v, 1), k_spec(off_v, 2),
                  pl.BlockSpec((NA_QB, hb), lambda h, b: (ctx_blk, off_k // hb + h)),
                  pl.BlockSpec((NA_QB, hb), lambda h, b: (ctx_blk, off_v // hb + h)),
                  pl.BlockSpec((None, None, NA_QB, NA_KB), lambda h, b: (kind(b), h, 0, 0))],
        out_specs=pl.BlockSpec((NA_QB, hb), lambda h, b: (b, h)),
        out_shape=jax.ShapeDtypeStruct((m, BRANCH_W), BF16),
        compiler_params=_params(("arbitrary", "arbitrary"), 40),
        name="neighbourhood_attention",
    )(px, px, px, px, px, px, px, px, px, bias)


def _na_bias_tables(rpb, rows):
    h = rpb.shape[0]
    col = np.arange(GRID_W)
    c0 = np.clip(col - NA_WIN_C // 2, 0, GRID_W - NA_WIN_C)
    in_win = (col[None, :] >= c0[:, None]) & (col[None, :] < c0[:, None] + NA_WIN_C)
    coff = np.clip(col[None, :] - col[:, None] + (NA_WIN_C - 1), 0, 2 * NA_WIN_C - 2)
    tables = []
    for r0, s0 in ((0, 0), (NA_QROWS, 0), (rows - NA_QROWS, rows - NA_KROWS)):
        r = r0 + np.arange(NA_QROWS)
        start = np.clip(r - NA_WIN_R // 2, 0, rows - NA_WIN_R)
        key_row = s0 + np.arange(NA_KROWS)
        valid_r = (key_row[None, :] >= start[:, None]) & (key_row[None, :] < start[:, None] + NA_WIN_R)
        roff = np.clip(key_row[None, :] - r[:, None] + (NA_WIN_R - 1), 0, 2 * NA_WIN_R - 2)
        bias = rpb[:, roff[:, None, :, None], coff[None, :, None, :]]
        valid = valid_r[:, None, :, None] & in_win[None, :, None, :]
        tables.append(jnp.where(valid[None], bias.astype(F32), NEG_INF).reshape(h, NA_QB, NA_KB))
    tables.append(jnp.full((h, NA_QB, NA_KB), NEG_INF, F32))
    return jnp.stack(tables)


def _swa_kernel(q_ref, cq_ref, sq_ref, k0_ref, k1_ref, k2_ref, ck0_ref, ck1_ref, ck2_ref,
                sk0_ref, sk1_ref, sk2_ref, v0_ref, v1_ref, v2_ref, kc_ref, vc_ref, mask_ref, sink_ref,
                rot_ref, o_ref):
    rot = rot_ref[...]
    scale = HEAD_DIM ** -0.5

    def rope(x, cos, sin):
        swapped = jnp.dot(x, rot, preferred_element_type=F32)
        return (x.astype(F32) * cos + swapped * sin).astype(BF16)

    cq, sq = cq_ref[...], sq_ref[...]
    q = jnp.concatenate([rope(q_ref[:, g * HEAD_DIM:(g + 1) * HEAD_DIM], cq, sq) for g in range(SWA_GROUP)],
                        axis=0)
    k_refs = ((k0_ref, ck0_ref, sk0_ref), (k1_ref, ck1_ref, sk1_ref), (k2_ref, ck2_ref, sk2_ref))
    v_refs = (v0_ref, v1_ref, v2_ref)
    s = [_dot_nt(q, rope(kr[...], cr[...], sr[...])) * scale + mask_ref[:, j * SWA_BLOCK:(j + 1) * SWA_BLOCK]
         for j, (kr, cr, sr) in enumerate(k_refs)]
    s.append(_dot_nt(q, kc_ref[...]) * scale)
    sink = sink_ref[...]
    m = functools.reduce(jnp.maximum, [jnp.max(t, axis=-1, keepdims=True) for t in s] + [sink])
    p = [jnp.exp(t - m) for t in s]
    denom = functools.reduce(lambda a, b: a + b, [jnp.sum(t, axis=-1, keepdims=True) for t in p])
    denom = denom + jnp.exp(sink - m)
    o = jnp.dot(p[3].astype(BF16), vc_ref[...], preferred_element_type=F32)
    for j in range(3):
        o = o + jnp.dot(p[j].astype(BF16), v_refs[j][...], preferred_element_type=F32)
    o = o / denom
    for g in range(SWA_GROUP):
        o_ref[:, g * HEAD_DIM:(g + 1) * HEAD_DIM] = o[g * SWA_BLOCK:(g + 1) * SWA_BLOCK].astype(o_ref.dtype)


def _swa_call(px, cos_t, sin_t, mask, sink_col, rot, n_lat, n_ctx, off_q, off_k, off_v):
    m = px.shape[0]
    blk, hb = SWA_BLOCK, HEAD_DIM
    nbl = n_lat // blk
    qw = SWA_GROUP * hb
    nbr = lambda b, j: jnp.clip(b + j - 1, 0, nbl - 1)
    kind = lambda b: jnp.where(b == 0, 0, jnp.where(b == nbl - 1, 2, jnp.where(b >= nbl, 3, 1)))
    kv_spec = lambda off, j: pl.BlockSpec((blk, hb), lambda kv, b: (nbr(b, j), off // hb + kv))
    tab_spec = lambda j: pl.BlockSpec((blk, hb), lambda kv, b: (nbr(b, j), 0))
    ctx_spec = lambda off: pl.BlockSpec((n_ctx, hb), lambda kv, b: (n_lat // n_ctx, off // hb + kv))
    here = pl.BlockSpec((blk, hb), lambda kv, b: (b, 0))
    return pl.pallas_call(
        _swa_kernel,
        grid=(SWA_KV_HEADS, m // blk),
        in_specs=[pl.BlockSpec((blk, qw), lambda kv, b: (b, off_q // qw + kv)), here, here,
                  kv_spec(off_k, 0), kv_spec(off_k, 1), kv_spec(off_k, 2),
                  tab_spec(0), tab_spec(1), tab_spec(2),
                  tab_spec(0), tab_spec(1), tab_spec(2),
                  kv_spec(off_v, 0), kv_spec(off_v, 1), kv_spec(off_v, 2),
                  ctx_spec(off_k), ctx_spec(off_v),
                  pl.BlockSpec((None, SWA_GROUP * blk, 3 * blk), lambda kv, b: (kind(b), 0, 0)),
                  pl.BlockSpec((None, SWA_GROUP * blk, 1), lambda kv, b: (kv, 0, 0)),
                  pl.BlockSpec((hb, hb), lambda kv, b: (0, 0))],
        out_specs=pl.BlockSpec((blk, qw), lambda kv, b: (b, kv)),
        out_shape=jax.ShapeDtypeStruct((m, BRANCH_W), BF16),
        compiler_params=_params(("arbitrary", "arbitrary"), 40),
        name="window_gqa",
    )(px, cos_t, sin_t, px, px, px, cos_t, cos_t, cos_t, sin_t, sin_t, sin_t, px, px, px, px, px,
      mask, sink_col, rot)


def _rope_tables(n_lat, n_ctx):
    t = jnp.arange(n_lat, dtype=jnp.int32)
    half = HEAD_DIM // 2
    inv = ROPE_BASE ** (-jnp.arange(0, half, 2, dtype=F32) / half)
    parts_c, parts_s = [], []
    for pos in (t // GRID_W, t % GRID_W):
        ang = pos.astype(F32)[:, None] * inv[None, :]
        parts_c += [jnp.cos(ang), jnp.cos(ang)]
        parts_s += [jnp.sin(ang), jnp.sin(ang)]
    cos_t = jnp.concatenate(parts_c, axis=-1)
    sin_t = jnp.concatenate(parts_s, axis=-1)
    cos_t = jnp.concatenate([cos_t, jnp.ones((n_ctx, HEAD_DIM), F32)], axis=0)
    sin_t = jnp.concatenate([sin_t, jnp.zeros((n_ctx, HEAD_DIM), F32)], axis=0)
    quarter = half // 2
    rot = np.zeros((HEAD_DIM, HEAD_DIM), np.float32)
    for lane in range(HEAD_DIM):
        if lane % half < quarter:
            rot[lane + quarter, lane] = -1.0
        else:
            rot[lane - quarter, lane] = 1.0
    return cos_t, sin_t, jnp.asarray(rot, BF16)


def _swa_mask():
    a = np.arange(SWA_BLOCK)
    j = np.arange(3 * SWA_BLOCK)
    in_band = np.abs((j[None, :] - SWA_BLOCK) - a[:, None]) <= SWA_WINDOW
    piece_ok = {0: j >= SWA_BLOCK, 1: j >= 0, 2: j < 2 * SWA_BLOCK, 3: j < 0}
    tabs = [np.where(in_band & piece_ok[kind][None, :], 0.0, NEG_INF) for kind in range(4)]
    return jnp.asarray(np.tile(np.stack(tabs), (1, SWA_GROUP, 1)), F32)


def kernel(x, c, ctx, c_ctx, w_ada, b_ada, g_mix, w_in, gmlp_ln_g, gmlp_ln_b, gmlp_ws, gmlp_bs, na_rpb,
           swa_sink, w_branch, w_out, g_ffn, w_ffn_gate, w_ffn_up, w_ffn_down, w_router, w_exp_gate,
           w_exp_up, w_exp_down, g_final):
    batch, n_lat, d = x.shape
    n_ctx = ctx.shape[1]
    depth = w_ada.shape[0]
    assert batch == 1 and c.shape[0] == 1
    assert n_lat % (NA_QB * 2) == 0 and n_ctx == NA_QB and n_lat // GRID_W >= NA_KROWS
    m = n_lat + n_ctx

    off_au, off_av = 0, BRANCH_W
    off_bq, off_cq = 2 * BRANCH_W, 3 * BRANCH_W
    off_gate = 4 * BRANCH_W
    off_bk = off_gate + N_BRANCHES * d
    off_bv = off_bk + BRANCH_W
    off_ck = off_bv + BRANCH_W
    off_cv = off_ck + SWA_KV_W
    assert w_in.shape[2] == off_cv + SWA_KV_W and off_au == 0 and off_av == BRANCH_W

    tm = 768 if m % 768 == 0 else 256
    tn = 512

    xs = jnp.concatenate([x[0], ctx[0]], axis=0)
    c8 = jnp.concatenate([c, c_ctx[None, :], jnp.zeros((6, d), F32)], axis=0)
    mods = _ada_mods(c8, w_ada, b_ada)[:, :2].reshape(depth, 2, 6, 1, d)

    cos_t, sin_t, rot = _rope_tables(n_lat, n_ctx)
    swa_mask = _swa_mask()
    bs_full = jnp.repeat(jnp.swapaxes(gmlp_bs, 1, 2), CHUNK, axis=2)
    sink_col = jnp.repeat(swa_sink.reshape(depth, SWA_KV_HEADS, SWA_GROUP), SWA_BLOCK, axis=2)[..., None]
    w_router_p = jnp.pad(w_router, ((0, 0), (0, 0), (0, ROUTER_LANES - N_EXPERTS)))
    n_moe, n_exp, _, fe = w_exp_gate.shape
    w_exp_down2 = w_exp_down.reshape(n_moe, n_exp * fe, d)

    for l in range(depth):
        i_layer = l // 2
        h = _norm_call(xs, g_mix, mods, l, 0, n_lat)
        px = _mm_call(h, w_in, l, tm, tn)
        y_a = _gmlp_call(px, gmlp_ln_g, gmlp_ln_b, gmlp_ws, bs_full, l, chunks=tm // CHUNK if tm == 768 else 2)
        y_b = _na_call(px, _na_bias_tables(na_rpb[l], n_lat // GRID_W), n_lat, off_bq, off_bk, off_bv)
        y_c = _swa_call(px, cos_t, sin_t, swa_mask, sink_col[l], rot, n_lat, n_ctx, off_cq, off_ck, off_cv)
        acc = _merge_call(y_a, y_b, y_c, px, w_branch, l, off_gate, tm, tn)
        xs = _mm_res_call(acc, w_out, l, xs, mods, l, 2, n_lat, tm, tn)
        if l % 2 == 0:
            h = _norm_call(xs, g_ffn, mods, l, 3, n_lat)
            hid = _ffn_up_call(h, w_ffn_gate, w_ffn_up, i_layer, tm, 256)
            xs = _mm_res_call(hid, w_ffn_down, i_layer, xs, mods, l, 5, n_lat, tm, tn)
        else:
            h, gates = _norm_call(xs, g_ffn, mods, l, 3, n_lat, w_router=w_router_p[i_layer])
            hid = _moe_up_call(h, w_exp_gate, w_exp_up, gates, i_layer, tm)
            xs = _mm_res_call(hid, w_exp_down2, i_layer, xs, mods, l, 5, n_lat, tm, tn)
    return _final_norm_call(xs, g_final, n_lat)[None]
```

```python
import functools

import numpy as np
import jax
import jax.numpy as jnp
from jax import lax
from jax.experimental import pallas as pl
from jax.experimental.pallas import tpu as pltpu

F32 = jnp.float32
BF16 = jnp.bfloat16

GRID_W = 64
HEAD_DIM = 128
BRANCH_W = 1024
N_BRANCHES = 3
CHUNK = 128
GMLP_GROUPS = 8
NA_HEADS = BRANCH_W // HEAD_DIM
NA_WIN_R = 8
NA_WIN_C = 16
SWA_HEADS = BRANCH_W // HEAD_DIM
SWA_KV_HEADS = 2
SWA_GROUP = SWA_HEADS // SWA_KV_HEADS
SWA_KV_W = SWA_KV_HEADS * HEAD_DIM
SWA_WINDOW = 128
SWA_BLOCK = 128
ROPE_BASE = 10000.0
N_EXPERTS = 8
TOP_K = 2
NEG_INF = -1e30
EPS = 1e-6

NA_QROWS = 4
NA_KROWS = 12
NA_QB = NA_QROWS * GRID_W
NA_KB = NA_KROWS * GRID_W
NA_PIECES = NA_KB // NA_QB

ROUTER_LANES = 128

MIB = 1024 * 1024
V7X_VMEM_BYTES = 64 * MIB
COMPILER_SCRATCH_BYTES = 6 * MIB


def _params(sem, vmem_bytes):
    limit = vmem_bytes + COMPILER_SCRATCH_BYTES
    assert limit <= V7X_VMEM_BYTES, limit
    return pltpu.CompilerParams(dimension_semantics=sem, vmem_limit_bytes=limit)


def _ada_kernel(c_ref, w_ref, b_ref, o_ref):
    cs = c_ref[...]
    cs = (cs * jax.nn.sigmoid(cs)).astype(BF16)
    w = w_ref[...].astype(BF16)
    o_ref[...] = jnp.dot(cs, w, preferred_element_type=F32) + b_ref[...]


def _ada_mods(c8, w_ada, b_ada, tn=512):
    depth, d, n6 = w_ada.shape
    return pl.pallas_call(
        _ada_kernel,
        grid=(depth, n6 // tn),
        in_specs=[
            pl.BlockSpec((8, d), lambda l, j: (0, 0)),
            pl.BlockSpec((None, d, tn), lambda l, j: (l, 0, j)),
            pl.BlockSpec((None, 1, tn), lambda l, j: (l, 0, j)),
        ],
        out_specs=pl.BlockSpec((None, 8, tn), lambda l, j: (l, 0, j)),
        out_shape=jax.ShapeDtypeStruct((depth, 8, n6), F32),
        compiler_params=_params(("arbitrary", "arbitrary"), 2 * d * tn * 4 + d * tn * 2 + MIB),
        name="ada_mods",
    )(c8, w_ada, b_ada.reshape(depth, 1, n6))


def _norm_mod(x, g_ref, sh_ref, sc_ref):
    ms = jnp.mean(x * x, axis=-1, keepdims=True)
    y = x * lax.rsqrt(ms + EPS) * g_ref[...]
    return y * (1.0 + sc_ref[...]) + sh_ref[...]


def _norm_kernel(x_ref, g_ref, sh_ref, sc_ref, o_ref):
    o_ref[...] = _norm_mod(x_ref[...], g_ref, sh_ref, sc_ref).astype(o_ref.dtype)


def _norm_router_kernel(x_ref, g_ref, sh_ref, sc_ref, wr_ref, o_ref, gate_ref):
    h = _norm_mod(x_ref[...], g_ref, sh_ref, sc_ref)
    o_ref[...] = h.astype(o_ref.dtype)
    logits = jnp.dot(h, wr_ref[...], preferred_element_type=F32, precision=lax.Precision.HIGHEST)
    lane = lax.broadcasted_iota(jnp.int32, logits.shape, 1)
    logits = jnp.where(lane < N_EXPERTS, logits, -jnp.inf)
    m1 = jnp.max(logits, axis=-1, keepdims=True)
    i1 = jnp.min(jnp.where(logits == m1, lane, ROUTER_LANES), axis=-1, keepdims=True)
    first = lane == i1
    rest = jnp.where(first, -jnp.inf, logits)
    m2 = jnp.max(rest, axis=-1, keepdims=True)
    i2 = jnp.min(jnp.where(rest == m2, lane, ROUTER_LANES), axis=-1, keepdims=True)
    second = lane == i2
    e2 = jnp.exp(m2 - m1)
    w1 = 1.0 / (1.0 + e2)
    gate_ref[...] = jnp.where(first, w1, 0.0) + jnp.where(second, e2 * w1, 0.0)


def _final_norm_kernel(x_ref, g_ref, o_ref):
    x = x_ref[...]
    ms = jnp.mean(x * x, axis=-1, keepdims=True)
    o_ref[...] = x * lax.rsqrt(ms + EPS) * g_ref[...]


def _norm_call(x, g, mods, l, which, n_lat, w_router=None, tr=256):
    m, d = x.shape
    lat_tiles = n_lat // tr
    row_kind = lambda i: jnp.where(i >= lat_tiles, 1, 0)
    mod_spec = lambda k: pl.BlockSpec((None, None, None, 1, d), lambda i: (l, row_kind(i), k, 0, 0))
    in_specs = [
        pl.BlockSpec((tr, d), lambda i: (i, 0)),
        pl.BlockSpec((None, 1, d), lambda i: (l, 0, 0)),
        mod_spec(which),
        mod_spec(which + 1),
    ]
    args = [x, g.reshape(g.shape[0], 1, d), mods, mods]
    vmem = 2 * (tr * d * 4 + tr * d * 2) + 4 * tr * d * 4
    if w_router is None:
        return pl.pallas_call(
            _norm_kernel,
            grid=(m // tr,),
            in_specs=in_specs,
            out_specs=pl.BlockSpec((tr, d), lambda i: (i, 0)),
            out_shape=jax.ShapeDtypeStruct((m, d), BF16),
            compiler_params=_params(("arbitrary",), vmem),
            name="norm_mod",
        )(*args)
    in_specs.append(pl.BlockSpec((d, ROUTER_LANES), lambda i: (0, 0)))
    args.append(w_router)
    return pl.pallas_call(
        _norm_router_kernel,
        grid=(m // tr,),
        in_specs=in_specs,
        out_specs=[pl.BlockSpec((tr, d), lambda i: (i, 0)),
                   pl.BlockSpec((tr, ROUTER_LANES), lambda i: (i, 0))],
        out_shape=[jax.ShapeDtypeStruct((m, d), BF16),
                   jax.ShapeDtypeStruct((m, ROUTER_LANES), F32)],
        compiler_params=_params(("arbitrary",), vmem + 2 * d * ROUTER_LANES * 4 + 4 * tr * d * 4),
        name="norm_mod_router",
    )(*args)


def _final_norm_call(x, g, n_lat, tr=256):
    d = x.shape[1]
    return pl.pallas_call(
        _final_norm_kernel,
        grid=(n_lat // tr,),
        in_specs=[pl.BlockSpec((tr, d), lambda i: (i, 0)),
                  pl.BlockSpec((1, d), lambda i: (0, 0))],
        out_specs=pl.BlockSpec((tr, d), lambda i: (i, 0)),
        out_shape=jax.ShapeDtypeStruct((n_lat, d), F32),
        compiler_params=_params(("arbitrary",), 6 * tr * d * 4),
        name="final_norm",
    )(x, g.reshape(1, d))


class _Tiling:
    def __init__(self, m, k, n, ms, tn):
        assert m % ms == 0 and k % ms == 0 and n % tn == 0
        self.ms, self.tn, self.nt = ms, tn, n // tn
        self.tm, self.ck = m // ms, k // ms
        assert self.tm % 16 == 0 and self.ck % 16 == 0
        self.grid = (self.nt + 1, ms)

    def row(self, j, i):
        return jnp.where(j == 0, 0, i)

    def col(self, j):
        return jnp.maximum(j - 1, 0)

    def wrow(self, j, i):
        return jnp.where(j == self.nt, self.ms - 1, i)

    def wcol(self, j):
        return jnp.minimum(j, self.nt - 1)


def _stage_weight(t, w_refs, wb_refs):
    j, i = pl.program_id(0), pl.program_id(1)

    @pl.when(j < t.nt)
    def _():
        slot = j % 2
        rows = pl.ds(pl.multiple_of(i * t.ck, t.ck), t.ck)
        for w_ref, wb_ref in zip(w_refs, wb_refs):
            if len(w_ref.shape) == 3:
                fe = w_ref.shape[2]
                for e in range(w_ref.shape[0]):
                    wb_ref[slot, rows, e * fe:(e + 1) * fe] = w_ref[e].astype(BF16)
            else:
                wb_ref[slot, rows, :] = w_ref[...].astype(BF16)


def _mm_kernel(a_ref, w_ref, o_ref, wb_ref, *, t):
    j = pl.program_id(0)
    _stage_weight(t, (w_ref,), (wb_ref,))

    @pl.when(j > 0)
    def _():
        o_ref[...] = jnp.dot(a_ref[...], wb_ref[(j + 1) % 2], preferred_element_type=F32).astype(o_ref.dtype)


def _mm_call(a, w, l, col0, ncols, ms, tn, name):
    m, k = a.shape
    t = _Tiling(m, k, ncols, ms, tn)
    assert col0 % tn == 0
    vmem = 2 * (t.tm * k * 2 + t.ck * tn * 4 + t.tm * tn * 2) + 2 * k * tn * 2 + t.tm * tn * 4
    return pl.pallas_call(
        functools.partial(_mm_kernel, t=t),
        grid=t.grid,
        in_specs=[pl.BlockSpec((t.tm, k), lambda j, i: (t.row(j, i), 0)),
                  pl.BlockSpec((None, t.ck, tn), lambda j, i: (l, t.wrow(j, i), col0 // tn + t.wcol(j)))],
        out_specs=pl.BlockSpec((t.tm, tn), lambda j, i: (t.row(j, i), t.col(j))),
        out_shape=jax.ShapeDtypeStruct((m, ncols), BF16),
        scratch_shapes=[pltpu.VMEM((2, k, tn), BF16)],
        compiler_params=_params(("arbitrary", "arbitrary"), vmem),
        name=name,
    )(a, w)


def _mm_res_kernel(a_ref, w_ref, x_ref, gx_ref, gc_ref, o_ref, wb_ref, *, t, n_lat):
    j, i = pl.program_id(0), pl.program_id(1)
    _stage_weight(t, (w_ref,), (wb_ref,))

    @pl.when(j > 0)
    def _():
        y = jnp.dot(a_ref[...], wb_ref[(j + 1) % 2], preferred_element_type=F32)
        row = i * t.tm + lax.broadcasted_iota(jnp.int32, (t.tm, 1), 0)
        gate = jnp.where(row < n_lat, gx_ref[...], gc_ref[...])
        o_ref[...] = x_ref[...] + gate * y


def _mm_res_call(a, w, l, x, mods, lm, which, n_lat, ms, tn):
    m, k = a.shape
    d = w.shape[2]
    t = _Tiling(m, k, d, ms, tn)
    gate_spec = lambda kind: pl.BlockSpec((None, None, None, 1, tn),
                                          lambda j, i: (lm, kind, which, 0, t.col(j)))
    xo_spec = pl.BlockSpec((t.tm, tn), lambda j, i: (t.row(j, i), t.col(j)))
    vmem = 2 * (t.tm * k * 2 + t.ck * tn * 4 + 2 * t.tm * tn * 4) + 2 * k * tn * 2 + t.tm * tn * 4
    return pl.pallas_call(
        functools.partial(_mm_res_kernel, t=t, n_lat=n_lat),
        grid=t.grid,
        in_specs=[pl.BlockSpec((t.tm, k), lambda j, i: (t.row(j, i), 0)),
                  pl.BlockSpec((None, t.ck, tn), lambda j, i: (l, t.wrow(j, i), t.wcol(j))),
                  xo_spec, gate_spec(0), gate_spec(1)],
        out_specs=xo_spec,
        out_shape=jax.ShapeDtypeStruct((m, d), F32),
        scratch_shapes=[pltpu.VMEM((2, k, tn), BF16)],
        compiler_params=_params(("arbitrary", "arbitrary"), vmem),
        name="proj_residual",
    )(a, w, x, mods, mods)


def _merge_kernel(ya_ref, yb_ref, yc_ref, ga_ref, gb_ref, gc_ref, w_ref, o_ref, wb_ref, *, t, bw):
    j = pl.program_id(0)
    _stage_weight(t, (w_ref,), (wb_ref,))

    @pl.when(j > 0)
    def _():
        slot = (j + 1) % 2
        acc = None
        for br, (y_ref, g_ref) in enumerate(((ya_ref, ga_ref), (yb_ref, gb_ref), (yc_ref, gc_ref))):
            part = jnp.dot(y_ref[...], wb_ref[slot, br * bw:(br + 1) * bw, :], preferred_element_type=F32)
            part = jax.nn.sigmoid(g_ref[...].astype(F32)) * part
            acc = part if acc is None else acc + part
        o_ref[...] = acc.astype(o_ref.dtype)


def _merge_call(ya, yb, yc, px, w_branch, l, off_gate, ms, tn):
    m, bw = ya.shape
    d = w_branch.shape[3]
    k = N_BRANCHES * bw
    t = _Tiling(m, k, d, ms, tn)
    y_spec = pl.BlockSpec((t.tm, bw), lambda j, i: (t.row(j, i), 0))
    g_spec = lambda br: pl.BlockSpec((t.tm, tn), lambda j, i: (t.row(j, i), (off_gate + br * d) // tn + t.col(j)))
    vmem = (2 * (3 * t.tm * bw * 2 + 3 * t.tm * tn * 2 + t.ck * tn * 4 + t.tm * tn * 2) + 2 * k * tn * 2
            + 2 * t.tm * tn * 4)
    return pl.pallas_call(
        functools.partial(_merge_kernel, t=t, bw=bw),
        grid=t.grid,
        in_specs=[y_spec, y_spec, y_spec, g_spec(0), g_spec(1), g_spec(2),
                  pl.BlockSpec((None, t.ck, tn), lambda j, i: (l, t.wrow(j, i), t.wcol(j)))],
        out_specs=pl.BlockSpec((t.tm, tn), lambda j, i: (t.row(j, i), t.col(j))),
        out_shape=jax.ShapeDtypeStruct((m, d), BF16),
        scratch_shapes=[pltpu.VMEM((2, k, tn), BF16)],
        compiler_params=_params(("arbitrary", "arbitrary"), vmem),
        name="branch_merge",
    )(ya, yb, yc, px, px, px, w_branch.reshape(w_branch.shape[0], k, d))


def _ffn_up_kernel(a_ref, wg_ref, wu_ref, o_ref, wgb_ref, wub_ref, *, t):
    j = pl.program_id(0)
    _stage_weight(t, (wg_ref, wu_ref), (wgb_ref, wub_ref))

    @pl.when(j > 0)
    def _():
        slot = (j + 1) % 2
        a = a_ref[...]
        hg = jnp.dot(a, wgb_ref[slot], preferred_element_type=F32)
        hu = jnp.dot(a, wub_ref[slot], preferred_element_type=F32)
        o_ref[...] = (hg * jax.nn.sigmoid(hg) * hu).astype(o_ref.dtype)


def _moe_up_kernel(a_ref, wg_ref, wu_ref, gate_ref, o_ref, wgb_ref, wub_ref, *, t, fe):
    j = pl.program_id(0)
    _stage_weight(t, (wg_ref, wu_ref), (wgb_ref, wub_ref))

    @pl.when(j > 0)
    def _():
        slot = (j + 1) % 2
        a = a_ref[...]
        hg = jnp.dot(a, wgb_ref[slot], preferred_element_type=F32)
        hu = jnp.dot(a, wub_ref[slot], preferred_element_type=F32)
        gates = gate_ref[...]
        lane = lax.broadcasted_iota(jnp.int32, gates.shape, 1)
        per_tile = t.tn // fe
        out_lane = lax.broadcasted_iota(jnp.int32, (1, t.tn), 1)
        ge = None
        for e in range(per_tile):
            w_e = jnp.sum(jnp.where(lane == (j - 1) * per_tile + e, gates, 0.0), axis=-1, keepdims=True)
            ge = w_e if ge is None else jnp.where(out_lane >= e * fe, w_e, ge)
        o_ref[...] = (hg * jax.nn.sigmoid(hg) * hu * ge).astype(o_ref.dtype)


def _ffn_up_call(a, wg, wu, i_layer, ms, tn):
    m, k = a.shape
    f = wg.shape[2]
    t = _Tiling(m, k, f, ms, tn)
    w_spec = pl.BlockSpec((None, t.ck, tn), lambda j, i: (i_layer, t.wrow(j, i), t.wcol(j)))
    vmem = 2 * (t.tm * k * 2 + 2 * t.ck * tn * 4 + t.tm * tn * 2) + 4 * k * tn * 2 + 3 * t.tm * tn * 4
    return pl.pallas_call(
        functools.partial(_ffn_up_kernel, t=t),
        grid=t.grid,
        in_specs=[pl.BlockSpec((t.tm, k), lambda j, i: (t.row(j, i), 0)), w_spec, w_spec],
        out_specs=pl.BlockSpec((t.tm, tn), lambda j, i: (t.row(j, i), t.col(j))),
        out_shape=jax.ShapeDtypeStruct((m, f), BF16),
        scratch_shapes=[pltpu.VMEM((2, k, tn), BF16), pltpu.VMEM((2, k, tn), BF16)],
        compiler_params=_params(("arbitrary", "arbitrary"), vmem),
        name="ffn_up",
    )(a, wg, wu)


def _moe_up_call(a, wg, wu, gates, i_layer, ms, per_tile):
    m, k = a.shape
    n_exp, fe = wg.shape[1], wg.shape[3]
    tn = per_tile * fe
    t = _Tiling(m, k, n_exp * fe, ms, tn)
    w_spec = pl.BlockSpec((None, per_tile, t.ck, fe), lambda j, i: (i_layer, t.wcol(j), t.wrow(j, i), 0))
    vmem = (2 * (t.tm * k * 2 + 2 * t.ck * tn * 4 + t.tm * tn * 2 + t.tm * ROUTER_LANES * 4) + 4 * k * tn * 2
            + 3 * t.tm * tn * 4)
    return pl.pallas_call(
        functools.partial(_moe_up_kernel, t=t, fe=fe),
        grid=t.grid,
        in_specs=[pl.BlockSpec((t.tm, k), lambda j, i: (t.row(j, i), 0)), w_spec, w_spec,
                  pl.BlockSpec((t.tm, ROUTER_LANES), lambda j, i: (t.row(j, i), 0))],
        out_specs=pl.BlockSpec((t.tm, tn), lambda j, i: (t.row(j, i), t.col(j))),
        out_shape=jax.ShapeDtypeStruct((m, n_exp * fe), BF16),
        scratch_shapes=[pltpu.VMEM((2, k, tn), BF16), pltpu.VMEM((2, k, tn), BF16)],
        compiler_params=_params(("arbitrary", "arbitrary"), vmem),
        name="moe_up",
    )(a, wg, wu, gates)


def _gmlp_kernel(u_ref, v_ref, lng_ref, lnb_ref, ws_ref, bs_ref, o_ref, *, chunks):
    u = jax.nn.gelu(u_ref[...].astype(F32))
    v = jax.nn.gelu(v_ref[...].astype(F32))
    mu = jnp.mean(v, axis=-1, keepdims=True)
    var = jnp.mean(jnp.square(v - mu), axis=-1, keepdims=True)
    vn = ((v - mu) * lax.rsqrt(var + EPS) * lng_ref[...] + lnb_ref[...]).astype(BF16)
    for g in range(GMLP_GROUPS):
        cols = slice(g * CHUNK, (g + 1) * CHUNK)
        wsg = ws_ref[g].astype(BF16)
        for c in range(chunks):
            rows = slice(c * CHUNK, (c + 1) * CHUNK)
            s = jnp.dot(wsg, vn[rows, cols], preferred_element_type=F32) + bs_ref[:, cols]
            o_ref[rows, cols] = (u[rows, cols] * s).astype(o_ref.dtype)


def _gmlp_call(px, lng, lnb, ws, bs_full, l, chunks):
    m = px.shape[0]
    t = chunks * CHUNK
    return pl.pallas_call(
        functools.partial(_gmlp_kernel, chunks=chunks),
        grid=(m // t,),
        in_specs=[pl.BlockSpec((t, BRANCH_W), lambda i: (i, 0)),
                  pl.BlockSpec((t, BRANCH_W), lambda i: (i, 1)),
                  pl.BlockSpec((None, 1, BRANCH_W), lambda i: (l, 0, 0)),
                  pl.BlockSpec((None, 1, BRANCH_W), lambda i: (l, 0, 0)),
                  pl.BlockSpec((None, GMLP_GROUPS, CHUNK, CHUNK), lambda i: (l, 0, 0, 0)),
                  pl.BlockSpec((None, CHUNK, BRANCH_W), lambda i: (l, 0, 0))],
        out_specs=pl.BlockSpec((t, BRANCH_W), lambda i: (i, 0)),
        out_shape=jax.ShapeDtypeStruct((m, BRANCH_W), BF16),
        compiler_params=_params(("arbitrary",), 6 * t * BRANCH_W * 2 + 6 * t * BRANCH_W * 4 + 2 * MIB),
        name="gmlp",
    )(px, px, lng.reshape(-1, 1, BRANCH_W), lnb.reshape(-1, 1, BRANCH_W), ws, bs_full)


def _dot_nt(a, b):
    return lax.dot_general(a, b, (((1,), (1,)), ((), ())), preferred_element_type=F32)


def _na_kernel(q_ref, k0_ref, k1_ref, k2_ref, v0_ref, v1_ref, v2_ref, kc_ref, vc_ref, bias_ref, o_ref):
    q = q_ref[...]
    scale = HEAD_DIM ** -0.5
    k_refs = (k0_ref, k1_ref, k2_ref)
    v_refs = (v0_ref, v1_ref, v2_ref)
    s = [_dot_nt(q, k_refs[j][...]) * scale + bias_ref[:, j * NA_QB:(j + 1) * NA_QB]
         for j in range(NA_PIECES)]
    s.append(_dot_nt(q, kc_ref[...]) * scale)
    m = functools.reduce(jnp.maximum, [jnp.max(t, axis=-1, keepdims=True) for t in s])
    p = [jnp.exp(t - m) for t in s]
    denom = functools.reduce(lambda a, b: a + b, [jnp.sum(t, axis=-1, keepdims=True) for t in p])
    o = jnp.dot(p[NA_PIECES].astype(BF16), vc_ref[...], preferred_element_type=F32)
    for j in range(NA_PIECES):
        o = o + jnp.dot(p[j].astype(BF16), v_refs[j][...], preferred_element_type=F32)
    o_ref[...] = (o / denom).astype(o_ref.dtype)


def _na_call(px, bias, l, n_lat, off_q, off_k, off_v):
    m = px.shape[0]
    nb = n_lat // NA_QB
    ctx_blk = n_lat // NA_QB
    hb = HEAD_DIM
    kstart = lambda b: jnp.clip(b - 1, 0, nb - NA_PIECES)
    k_spec = lambda off, j: pl.BlockSpec((NA_QB, hb), lambda h, b: (kstart(b) + j, off // hb + h))
    kind = lambda b: jnp.where(b == 0, 0, jnp.where(b == nb - 1, 2, jnp.where(b == nb, 3, 1)))
    return pl.pallas_call(
        _na_kernel,
        grid=(NA_HEADS, m // NA_QB),
        in_specs=[pl.BlockSpec((NA_QB, hb), lambda h, b: (b, off_q // hb + h)),
                  k_spec(off_k, 0), k_spec(off_k, 1), k_spec(off_k, 2),
                  k_spec(off_v, 0), k_spec(off_v, 1), k_spec(off_v, 2),
                  pl.BlockSpec((NA_QB, hb), lambda h, b: (ctx_blk, off_k // hb + h)),
                  pl.BlockSpec((NA_QB, hb), lambda h, b: (ctx_blk, off_v // hb + h)),
                  pl.BlockSpec((None, None, None, NA_QB, NA_KB), lambda h, b: (l, kind(b), h, 0, 0))],
        out_specs=pl.BlockSpec((NA_QB, hb), lambda h, b: (b, h)),
        out_shape=jax.ShapeDtypeStruct((m, BRANCH_W), BF16),
        compiler_params=_params(("arbitrary", "arbitrary"), 2 * NA_QB * NA_KB * 4 + 12 * NA_QB * (NA_KB + NA_QB) * 4),
        name="neighbourhood_attention",
    )(px, px, px, px, px, px, px, px, px, bias)


def _na_bias_tables(rpb, rows):
    depth, h = rpb.shape[:2]
    col = np.arange(GRID_W)
    c0 = np.clip(col - NA_WIN_C // 2, 0, GRID_W - NA_WIN_C)
    in_win = (col[None, :] >= c0[:, None]) & (col[None, :] < c0[:, None] + NA_WIN_C)
    coff = np.clip(col[None, :] - col[:, None] + (NA_WIN_C - 1), 0, 2 * NA_WIN_C - 2)
    pick_c = (coff[:, :, None] == np.arange(2 * NA_WIN_C - 1)).astype(np.float32)
    by_col = jnp.einsum('lhij,cdj->lhicd', rpb.astype(F32), pick_c, precision=lax.Precision.HIGHEST)
    tables = []
    for r0, s0 in ((0, 0), (NA_QROWS, 0), (rows - NA_QROWS, rows - NA_KROWS)):
        r = r0 + np.arange(NA_QROWS)
        start = np.clip(r - NA_WIN_R // 2, 0, rows - NA_WIN_R)
        key_row = s0 + np.arange(NA_KROWS)
        valid_r = (key_row[None, :] >= start[:, None]) & (key_row[None, :] < start[:, None] + NA_WIN_R)
        roff = np.clip(key_row[None, :] - r[:, None] + (NA_WIN_R - 1), 0, 2 * NA_WIN_R - 2)
        pick_r = (roff[:, :, None] == np.arange(2 * NA_WIN_R - 1)).astype(np.float32)
        bias = jnp.einsum('qki,lhicd->lhqckd', pick_r, by_col, precision=lax.Precision.HIGHEST)
        valid = valid_r[:, None, :, None] & in_win[None, :, None, :]
        tables.append(jnp.where(valid[None, None], bias, NEG_INF).reshape(depth, h, NA_QB, NA_KB))
    tables.append(jnp.full((depth, h, NA_QB, NA_KB), NEG_INF, F32))
    return jnp.stack(tables, axis=1)


def _swa_kernel(q_ref, cq_ref, sq_ref, k0_ref, k1_ref, k2_ref, ck0_ref, ck1_ref, ck2_ref,
                sk0_ref, sk1_ref, sk2_ref, v0_ref, v1_ref, v2_ref, kc_ref, vc_ref, mask_ref, sink_ref,
                rot_ref, o_ref):
    rot = rot_ref[...]
    scale = HEAD_DIM ** -0.5

    def rope(x, cos, sin):
        swapped = jnp.dot(x, rot, preferred_element_type=F32)
        return (x.astype(F32) * cos + swapped * sin).astype(BF16)

    cq, sq = cq_ref[...], sq_ref[...]
    q = jnp.concatenate([rope(q_ref[:, g * HEAD_DIM:(g + 1) * HEAD_DIM], cq, sq) for g in range(SWA_GROUP)],
                        axis=0)
    k_refs = ((k0_ref, ck0_ref, sk0_ref), (k1_ref, ck1_ref, sk1_ref), (k2_ref, ck2_ref, sk2_ref))
    v_refs = (v0_ref, v1_ref, v2_ref)
    s = [_dot_nt(q, rope(kr[...], cr[...], sr[...])) * scale + mask_ref[:, j * SWA_BLOCK:(j + 1) * SWA_BLOCK]
         for j, (kr, cr, sr) in enumerate(k_refs)]
    s.append(_dot_nt(q, kc_ref[...]) * scale)
    sink = sink_ref[...]
    m = functools.reduce(jnp.maximum, [jnp.max(t, axis=-1, keepdims=True) for t in s] + [sink])
    p = [jnp.exp(t - m) for t in s]
    denom = functools.reduce(lambda a, b: a + b, [jnp.sum(t, axis=-1, keepdims=True) for t in p])
    denom = denom + jnp.exp(sink - m)
    o = jnp.dot(p[3].astype(BF16), vc_ref[...], preferred_element_type=F32)
    for j in range(3):
        o = o + jnp.dot(p[j].astype(BF16), v_refs[j][...], preferred_element_type=F32)
    o = o / denom
    for g in range(SWA_GROUP):
        o_ref[:, g * HEAD_DIM:(g + 1) * HEAD_DIM] = o[g * SWA_BLOCK:(g + 1) * SWA_BLOCK].astype(o_ref.dtype)


def _swa_call(px, pkv, cos_t, sin_t, mask, sink_col, rot, n_lat, n_ctx, off_q, off_k, off_v):
    m = px.shape[0]
    blk, hb = SWA_BLOCK, HEAD_DIM
    nbl = n_lat // blk
    qw = SWA_GROUP * hb
    nbr = lambda b, j: jnp.clip(b + j - 1, 0, nbl - 1)
    kind = lambda b: jnp.where(b == 0, 0, jnp.where(b == nbl - 1, 2, jnp.where(b >= nbl, 3, 1)))
    kv_spec = lambda off, j: pl.BlockSpec((blk, hb), lambda kv, b: (nbr(b, j), off // hb + kv))
    tab_spec = lambda j: pl.BlockSpec((blk, hb), lambda kv, b: (nbr(b, j), 0))
    ctx_spec = lambda off: pl.BlockSpec((n_ctx, hb), lambda kv, b: (n_lat // n_ctx, off // hb + kv))
    here = pl.BlockSpec((blk, hb), lambda kv, b: (b, 0))
    return pl.pallas_call(
        _swa_kernel,
        grid=(SWA_KV_HEADS, m // blk),
        in_specs=[pl.BlockSpec((blk, qw), lambda kv, b: (b, off_q // qw + kv)), here, here,
                  kv_spec(off_k, 0), kv_spec(off_k, 1), kv_spec(off_k, 2),
                  tab_spec(0), tab_spec(1), tab_spec(2),
                  tab_spec(0), tab_spec(1), tab_spec(2),
                  kv_spec(off_v, 0), kv_spec(off_v, 1), kv_spec(off_v, 2),
                  ctx_spec(off_k), ctx_spec(off_v),
                  pl.BlockSpec((None, SWA_GROUP * blk, 3 * blk), lambda kv, b: (kind(b), 0, 0)),
                  pl.BlockSpec((None, SWA_GROUP * blk, 1), lambda kv, b: (kv, 0, 0)),
                  pl.BlockSpec((hb, hb), lambda kv, b: (0, 0))],
        out_specs=pl.BlockSpec((blk, qw), lambda kv, b: (b, kv)),
        out_shape=jax.ShapeDtypeStruct((m, BRANCH_W), BF16),
        compiler_params=_params(("arbitrary", "arbitrary"), 16 * SWA_GROUP * blk * (3 * blk + n_ctx) * 4),
        name="window_gqa",
    )(px, cos_t, sin_t, pkv, pkv, pkv, cos_t, cos_t, cos_t, sin_t, sin_t, sin_t, pkv, pkv, pkv, pkv, pkv,
      mask, sink_col, rot)


def _rope_tables(n_lat, n_ctx):
    t = jnp.arange(n_lat, dtype=jnp.int32)
    half = HEAD_DIM // 2
    inv = ROPE_BASE ** (-jnp.arange(0, half, 2, dtype=F32) / half)
    parts_c, parts_s = [], []
    for pos in (t // GRID_W, t % GRID_W):
        ang = pos.astype(F32)[:, None] * inv[None, :]
        parts_c += [jnp.cos(ang), jnp.cos(ang)]
        parts_s += [jnp.sin(ang), jnp.sin(ang)]
    cos_t = jnp.concatenate(parts_c, axis=-1)
    sin_t = jnp.concatenate(parts_s, axis=-1)
    cos_t = jnp.concatenate([cos_t, jnp.ones((n_ctx, HEAD_DIM), F32)], axis=0)
    sin_t = jnp.concatenate([sin_t, jnp.zeros((n_ctx, HEAD_DIM), F32)], axis=0)
    quarter = half // 2
    rot = np.zeros((HEAD_DIM, HEAD_DIM), np.float32)
    for lane in range(HEAD_DIM):
        if lane % half < quarter:
            rot[lane + quarter, lane] = -1.0
        else:
            rot[lane - quarter, lane] = 1.0
    return cos_t, sin_t, jnp.asarray(rot, BF16)


def _swa_mask():
    a = np.arange(SWA_BLOCK)
    j = np.arange(3 * SWA_BLOCK)
    in_band = np.abs((j[None, :] - SWA_BLOCK) - a[:, None]) <= SWA_WINDOW
    piece_ok = {0: j >= SWA_BLOCK, 1: j >= 0, 2: j < 2 * SWA_BLOCK, 3: j < 0}
    tabs = [np.where(in_band & piece_ok[kind][None, :], 0.0, NEG_INF) for kind in range(4)]
    return jnp.asarray(np.tile(np.stack(tabs), (1, SWA_GROUP, 1)), F32)


def kernel(x, c, ctx, c_ctx, w_ada, b_ada, g_mix, w_in, gmlp_ln_g, gmlp_ln_b, gmlp_ws, gmlp_bs, na_rpb,
           swa_sink, w_branch, w_out, g_ffn, w_ffn_gate, w_ffn_up, w_ffn_down, w_router, w_exp_gate,
           w_exp_up, w_exp_down, g_final):
    batch, n_lat, d = x.shape
    n_ctx = ctx.shape[1]
    depth = w_ada.shape[0]
    assert batch == 1 and c.shape[0] == 1
    assert n_lat % NA_QB == 0 and n_ctx == NA_QB and n_lat // GRID_W >= NA_KROWS
    m = n_lat + n_ctx

    off_au, off_av = 0, BRANCH_W
    off_bq, off_cq = 2 * BRANCH_W, 3 * BRANCH_W
    off_gate = 4 * BRANCH_W
    off_bk = off_gate + N_BRANCHES * d
    off_bv = off_bk + BRANCH_W
    off_ck = off_bv + BRANCH_W
    off_cv = off_ck + SWA_KV_W
    assert w_in.shape[2] == off_cv + SWA_KV_W and off_au == 0 and off_av == BRANCH_W

    ms_big, ms_small = 8, 16
    gmlp_chunks = 6 if m % (6 * CHUNK) == 0 else 2

    xs = jnp.concatenate([x[0], ctx[0]], axis=0)
    c8 = jnp.concatenate([c, c_ctx[None, :], jnp.zeros((6, d), F32)], axis=0)
    mods = _ada_mods(c8, w_ada, b_ada)[:, :2].reshape(depth, 2, 6, 1, d)

    cos_t, sin_t, rot = _rope_tables(n_lat, n_ctx)
    swa_mask = _swa_mask()
    na_bias = _na_bias_tables(na_rpb, n_lat // GRID_W)
    bs_full = jnp.repeat(jnp.swapaxes(gmlp_bs, 1, 2), CHUNK, axis=2)
    sink_col = jnp.repeat(swa_sink.reshape(depth, SWA_KV_HEADS, SWA_GROUP), SWA_BLOCK, axis=2)[..., None]
    w_router_p = jnp.pad(w_router, ((0, 0), (0, 0), (0, ROUTER_LANES - N_EXPERTS)))
    n_moe, n_exp, _, fe = w_exp_gate.shape
    w_exp_down2 = w_exp_down.reshape(n_moe, n_exp * fe, d)

    for l in range(depth):
        i_layer = l // 2
        h = _norm_call(xs, g_mix, mods, l, 0, n_lat)
        px = _mm_call(h, w_in, l, 0, off_ck, ms_big, 1024, "in_proj")
        pkv = _mm_call(h, w_in, l, off_ck, 2 * SWA_KV_W, ms_big, 2 * SWA_KV_W, "in_proj_kv")
        y_a = _gmlp_call(px, gmlp_ln_g, gmlp_ln_b, gmlp_ws, bs_full, l, gmlp_chunks)
        y_b = _na_call(px, na_bias, l, n_lat, off_bq, off_bk, off_bv)
        y_c = _swa_call(px, pkv, cos_t, sin_t, swa_mask, sink_col[l], rot, n_lat, n_ctx, off_cq, 0, SWA_KV_W)
        acc = _merge_call(y_a, y_b, y_c, px, w_branch, l, off_gate, ms_small, 1024)
        xs = _mm_res_call(acc, w_out, l, xs, mods, l, 2, n_lat, ms_small, 1024)
        if l % 2 == 0:
            h = _norm_call(xs, g_ffn, mods, l, 3, n_lat)
            hid = _ffn_up_call(h, w_ffn_gate, w_ffn_up, i_layer, ms_big, 512)
            xs = _mm_res_call(hid, w_ffn_down, i_layer, xs, mods, l, 5, n_lat, ms_big, 1024)
        else:
            h, gates = _norm_call(xs, g_ffn, mods, l, 3, n_lat, w_router=w_router_p[i_layer])
            hid = _moe_up_call(h, w_exp_gate, w_exp_up, gates, i_layer, ms_small, 2)
            xs = _mm_res_call(hid, w_exp_down2, i_layer, xs, mods, l, 5, n_lat, ms_big, 1024)
    return _final_norm_call(xs, g_final, n_lat)[None]
```

```python
import functools

import numpy as np
import jax
import jax.numpy as jnp
from jax import lax
from jax.experimental import pallas as pl
from jax.experimental.pallas import tpu as pltpu

F32 = jnp.float32
BF16 = jnp.bfloat16

GRID_W = 64
HEAD_DIM = 128
BRANCH_W = 1024
N_BRANCHES = 3
CHUNK = 128
GMLP_GROUPS = 8
NA_HEADS = BRANCH_W // HEAD_DIM
NA_WIN_R = 8
NA_WIN_C = 16
SWA_HEADS = BRANCH_W // HEAD_DIM
SWA_KV_HEADS = 2
SWA_GROUP = SWA_HEADS // SWA_KV_HEADS
SWA_KV_W = SWA_KV_HEADS * HEAD_DIM
SWA_WINDOW = 128
SWA_BLOCK = 128
ROPE_BASE = 10000.0
N_EXPERTS = 8
TOP_K = 2
NEG_INF = -1e30
EPS = 1e-6

NA_QROWS = 4
NA_KROWS = 12
NA_QB = NA_QROWS * GRID_W
NA_KB = NA_KROWS * GRID_W
NA_PIECES = NA_KB // NA_QB

SWA_QB = 2 * SWA_BLOCK
SWA_PIECES = 4

LOG2E = 1.4426950408889634

ROUTER_LANES = 128

MIB = 1024 * 1024
V7X_VMEM_BYTES = 64 * MIB
COMPILER_SCRATCH_BYTES = 6 * MIB


def _params(sem, vmem_bytes):
    limit = vmem_bytes + COMPILER_SCRATCH_BYTES
    assert limit <= V7X_VMEM_BYTES, limit
    return pltpu.CompilerParams(dimension_semantics=sem, vmem_limit_bytes=limit)


def _ada_kernel(c_ref, w_ref, b_ref, o_ref):
    cs = c_ref[...]
    cs = (cs * jax.nn.sigmoid(cs)).astype(BF16)
    w = w_ref[...].astype(BF16)
    o_ref[...] = jnp.dot(cs, w, preferred_element_type=F32) + b_ref[...]


def _ada_mods(c8, w_ada, b_ada, tn=512):
    depth, d, n6 = w_ada.shape
    return pl.pallas_call(
        _ada_kernel,
        grid=(depth, n6 // tn),
        in_specs=[
            pl.BlockSpec((8, d), lambda l, j: (0, 0)),
            pl.BlockSpec((None, d, tn), lambda l, j: (l, 0, j)),
            pl.BlockSpec((None, 1, tn), lambda l, j: (l, 0, j)),
        ],
        out_specs=pl.BlockSpec((None, 8, tn), lambda l, j: (l, 0, j)),
        out_shape=jax.ShapeDtypeStruct((depth, 8, n6), F32),
        compiler_params=_params(("arbitrary", "arbitrary"), 2 * d * tn * 4 + d * tn * 2 + MIB),
        name="ada_mods",
    )(c8, w_ada, b_ada.reshape(depth, 1, n6))


def _norm_mod(x, g_ref, sh_ref, sc_ref):
    ms = jnp.mean(x * x, axis=-1, keepdims=True)
    y = x * lax.rsqrt(ms + EPS) * g_ref[...]
    return y * (1.0 + sc_ref[...]) + sh_ref[...]


def _norm_kernel(x_ref, g_ref, sh_ref, sc_ref, o_ref):
    o_ref[...] = _norm_mod(x_ref[...], g_ref, sh_ref, sc_ref).astype(o_ref.dtype)


def _norm_router_kernel(x_ref, g_ref, sh_ref, sc_ref, wr_ref, o_ref, gate_ref):
    h = _norm_mod(x_ref[...], g_ref, sh_ref, sc_ref)
    o_ref[...] = h.astype(o_ref.dtype)
    logits = jnp.dot(h, wr_ref[...], preferred_element_type=F32, precision=lax.Precision.HIGHEST)
    lane = lax.broadcasted_iota(jnp.int32, logits.shape, 1)
    logits = jnp.where(lane < N_EXPERTS, logits, -jnp.inf)
    m1 = jnp.max(logits, axis=-1, keepdims=True)
    i1 = jnp.min(jnp.where(logits == m1, lane, ROUTER_LANES), axis=-1, keepdims=True)
    first = lane == i1
    rest = jnp.where(first, -jnp.inf, logits)
    m2 = jnp.max(rest, axis=-1, keepdims=True)
    i2 = jnp.min(jnp.where(rest == m2, lane, ROUTER_LANES), axis=-1, keepdims=True)
    second = lane == i2
    e2 = jnp.exp(m2 - m1)
    w1 = 1.0 / (1.0 + e2)
    gate_ref[...] = jnp.where(first, w1, 0.0) + jnp.where(second, e2 * w1, 0.0)


def _final_norm_kernel(x_ref, g_ref, o_ref):
    x = x_ref[...]
    ms = jnp.mean(x * x, axis=-1, keepdims=True)
    o_ref[...] = x * lax.rsqrt(ms + EPS) * g_ref[...]


def _norm_call(x, g, mods, l, which, n_lat, w_router=None, tr=256):
    m, d = x.shape
    lat_tiles = n_lat // tr
    row_kind = lambda i: jnp.where(i >= lat_tiles, 1, 0)
    mod_spec = lambda k: pl.BlockSpec((None, None, None, 1, d), lambda i: (l, row_kind(i), k, 0, 0))
    in_specs = [
        pl.BlockSpec((tr, d), lambda i: (i, 0)),
        pl.BlockSpec((None, 1, d), lambda i: (l, 0, 0)),
        mod_spec(which),
        mod_spec(which + 1),
    ]
    args = [x, g.reshape(g.shape[0], 1, d), mods, mods]
    vmem = 2 * (tr * d * 4 + tr * d * 2) + 4 * tr * d * 4
    if w_router is None:
        return pl.pallas_call(
            _norm_kernel,
            grid=(m // tr,),
            in_specs=in_specs,
            out_specs=pl.BlockSpec((tr, d), lambda i: (i, 0)),
            out_shape=jax.ShapeDtypeStruct((m, d), BF16),
            compiler_params=_params(("arbitrary",), vmem),
            name="norm_mod",
        )(*args)
    in_specs.append(pl.BlockSpec((d, ROUTER_LANES), lambda i: (0, 0)))
    args.append(w_router)
    return pl.pallas_call(
        _norm_router_kernel,
        grid=(m // tr,),
        in_specs=in_specs,
        out_specs=[pl.BlockSpec((tr, d), lambda i: (i, 0)),
                   pl.BlockSpec((tr, ROUTER_LANES), lambda i: (i, 0))],
        out_shape=[jax.ShapeDtypeStruct((m, d), BF16),
                   jax.ShapeDtypeStruct((m, ROUTER_LANES), F32)],
        compiler_params=_params(("arbitrary",), vmem + 2 * d * ROUTER_LANES * 4 + 4 * tr * d * 4),
        name="norm_mod_router",
    )(*args)


def _final_norm_call(x, g, n_lat, tr=256):
    d = x.shape[1]
    return pl.pallas_call(
        _final_norm_kernel,
        grid=(n_lat // tr,),
        in_specs=[pl.BlockSpec((tr, d), lambda i: (i, 0)),
                  pl.BlockSpec((1, d), lambda i: (0, 0))],
        out_specs=pl.BlockSpec((tr, d), lambda i: (i, 0)),
        out_shape=jax.ShapeDtypeStruct((n_lat, d), F32),
        compiler_params=_params(("arbitrary",), 6 * tr * d * 4),
        name="final_norm",
    )(x, g.reshape(1, d))


class _Tiling:
    def __init__(self, m, k, n, ms, tn):
        assert m % ms == 0 and k % ms == 0 and n % tn == 0
        self.ms, self.tn, self.nt = ms, tn, n // tn
        self.tm, self.ck = m // ms, k // ms
        assert self.tm % 16 == 0 and self.ck % 16 == 0
        self.grid = (self.nt + 1, ms)

    def row(self, j, i):
        return jnp.where(j == 0, 0, i)

    def col(self, j):
        return jnp.maximum(j - 1, 0)

    def wrow(self, j, i):
        return jnp.where(j == self.nt, self.ms - 1, i)

    def wcol(self, j):
        return jnp.minimum(j, self.nt - 1)


def _stage_weight(t, w_refs, wb_refs):
    j, i = pl.program_id(0), pl.program_id(1)

    @pl.when(j < t.nt)
    def _():
        slot = j % 2
        rows = pl.ds(pl.multiple_of(i * t.ck, t.ck), t.ck)
        for w_ref, wb_ref in zip(w_refs, wb_refs):
            if len(w_ref.shape) == 3:
                fe = w_ref.shape[2]
                for e in range(w_ref.shape[0]):
                    wb_ref[slot, rows, e * fe:(e + 1) * fe] = w_ref[e].astype(BF16)
            else:
                wb_ref[slot, rows, :] = w_ref[...].astype(BF16)


def _mm_kernel(a_ref, w_ref, o_ref, wb_ref, *, t):
    j = pl.program_id(0)
    _stage_weight(t, (w_ref,), (wb_ref,))

    @pl.when(j > 0)
    def _():
        o_ref[...] = jnp.dot(a_ref[...], wb_ref[(j + 1) % 2], preferred_element_type=F32).astype(o_ref.dtype)


def _mm_call(a, w, l, col0, ncols, ms, tn, name):
    m, k = a.shape
    t = _Tiling(m, k, ncols, ms, tn)
    assert col0 % tn == 0
    vmem = 2 * (t.tm * k * 2 + t.ck * tn * 4 + t.tm * tn * 2) + 2 * k * tn * 2 + t.tm * tn * 4
    return pl.pallas_call(
        functools.partial(_mm_kernel, t=t),
        grid=t.grid,
        in_specs=[pl.BlockSpec((t.tm, k), lambda j, i: (t.row(j, i), 0)),
                  pl.BlockSpec((None, t.ck, tn), lambda j, i: (l, t.wrow(j, i), col0 // tn + t.wcol(j)))],
        out_specs=pl.BlockSpec((t.tm, tn), lambda j, i: (t.row(j, i), t.col(j))),
        out_shape=jax.ShapeDtypeStruct((m, ncols), BF16),
        scratch_shapes=[pltpu.VMEM((2, k, tn), BF16)],
        compiler_params=_params(("arbitrary", "arbitrary"), vmem),
        name=name,
    )(a, w)


def _mm_res_kernel(a_ref, w_ref, x_ref, gx_ref, gc_ref, o_ref, wb_ref, *, t, n_lat):
    j, i = pl.program_id(0), pl.program_id(1)
    _stage_weight(t, (w_ref,), (wb_ref,))

    @pl.when(j > 0)
    def _():
        y = jnp.dot(a_ref[...], wb_ref[(j + 1) % 2], preferred_element_type=F32)
        row = i * t.tm + lax.broadcasted_iota(jnp.int32, (t.tm, 1), 0)
        gate = jnp.where(row < n_lat, gx_ref[...], gc_ref[...])
        o_ref[...] = x_ref[...] + gate * y


def _mm_res_call(a, w, l, x, mods, lm, which, n_lat, ms, tn):
    m, k = a.shape
    d = w.shape[2]
    t = _Tiling(m, k, d, ms, tn)
    gate_spec = lambda kind: pl.BlockSpec((None, None, None, 1, tn),
                                          lambda j, i: (lm, kind, which, 0, t.col(j)))
    xo_spec = pl.BlockSpec((t.tm, tn), lambda j, i: (t.row(j, i), t.col(j)))
    vmem = 2 * (t.tm * k * 2 + t.ck * tn * 4 + 2 * t.tm * tn * 4) + 2 * k * tn * 2 + t.tm * tn * 4
    return pl.pallas_call(
        functools.partial(_mm_res_kernel, t=t, n_lat=n_lat),
        grid=t.grid,
        in_specs=[pl.BlockSpec((t.tm, k), lambda j, i: (t.row(j, i), 0)),
                  pl.BlockSpec((None, t.ck, tn), lambda j, i: (l, t.wrow(j, i), t.wcol(j))),
                  xo_spec, gate_spec(0), gate_spec(1)],
        out_specs=xo_spec,
        out_shape=jax.ShapeDtypeStruct((m, d), F32),
        scratch_shapes=[pltpu.VMEM((2, k, tn), BF16)],
        compiler_params=_params(("arbitrary", "arbitrary"), vmem),
        name="proj_residual",
    )(a, w, x, mods, mods)


def _merge_kernel(ya_ref, yb_ref, yc_ref, ga_ref, gb_ref, gc_ref, w_ref, o_ref, wb_ref, *, t, bw):
    j = pl.program_id(0)
    _stage_weight(t, (w_ref,), (wb_ref,))

    @pl.when(j > 0)
    def _():
        slot = (j + 1) % 2
        acc = None
        for br, (y_ref, g_ref) in enumerate(((ya_ref, ga_ref), (yb_ref, gb_ref), (yc_ref, gc_ref))):
            part = jnp.dot(y_ref[...], wb_ref[slot, br * bw:(br + 1) * bw, :], preferred_element_type=F32)
            part = jax.nn.sigmoid(g_ref[...].astype(F32)) * part
            acc = part if acc is None else acc + part
        o_ref[...] = acc.astype(o_ref.dtype)


def _merge_call(ya, yb, yc, px, w_branch, l, off_gate, ms, tn):
    m, bw = ya.shape
    d = w_branch.shape[3]
    k = N_BRANCHES * bw
    t = _Tiling(m, k, d, ms, tn)
    y_spec = pl.BlockSpec((t.tm, bw), lambda j, i: (t.row(j, i), 0))
    g_spec = lambda br: pl.BlockSpec((t.tm, tn), lambda j, i: (t.row(j, i), (off_gate + br * d) // tn + t.col(j)))
    vmem = (2 * (3 * t.tm * bw * 2 + 3 * t.tm * tn * 2 + t.ck * tn * 4 + t.tm * tn * 2) + 2 * k * tn * 2
            + 2 * t.tm * tn * 4)
    return pl.pallas_call(
        functools.partial(_merge_kernel, t=t, bw=bw),
        grid=t.grid,
        in_specs=[y_spec, y_spec, y_spec, g_spec(0), g_spec(1), g_spec(2),
                  pl.BlockSpec((None, t.ck, tn), lambda j, i: (l, t.wrow(j, i), t.wcol(j)))],
        out_specs=pl.BlockSpec((t.tm, tn), lambda j, i: (t.row(j, i), t.col(j))),
        out_shape=jax.ShapeDtypeStruct((m, d), BF16),
        scratch_shapes=[pltpu.VMEM((2, k, tn), BF16)],
        compiler_params=_params(("arbitrary", "arbitrary"), vmem),
        name="branch_merge",
    )(ya, yb, yc, px, px, px, w_branch.reshape(w_branch.shape[0], k, d))


def _ffn_up_kernel(a_ref, wg_ref, wu_ref, o_ref, wgb_ref, wub_ref, *, t):
    j = pl.program_id(0)
    _stage_weight(t, (wg_ref, wu_ref), (wgb_ref, wub_ref))

    @pl.when(j > 0)
    def _():
        slot = (j + 1) % 2
        a = a_ref[...]
        hg = jnp.dot(a, wgb_ref[slot], preferred_element_type=F32)
        hu = jnp.dot(a, wub_ref[slot], preferred_element_type=F32)
        o_ref[...] = (hg * jax.nn.sigmoid(hg) * hu).astype(o_ref.dtype)


def _moe_up_kernel(a_ref, wg_ref, wu_ref, gate_ref, o_ref, wgb_ref, wub_ref, *, t, fe):
    j = pl.program_id(0)
    _stage_weight(t, (wg_ref, wu_ref), (wgb_ref, wub_ref))

    @pl.when(j > 0)
    def _():
        slot = (j + 1) % 2
        a = a_ref[...]
        hg = jnp.dot(a, wgb_ref[slot], preferred_element_type=F32)
        hu = jnp.dot(a, wub_ref[slot], preferred_element_type=F32)
        gates = gate_ref[...]
        lane = lax.broadcasted_iota(jnp.int32, gates.shape, 1)
        per_tile = t.tn // fe
        out_lane = lax.broadcasted_iota(jnp.int32, (1, t.tn), 1)
        ge = None
        for e in range(per_tile):
            w_e = jnp.sum(jnp.where(lane == (j - 1) * per_tile + e, gates, 0.0), axis=-1, keepdims=True)
            ge = w_e if ge is None else jnp.where(out_lane >= e * fe, w_e, ge)
        o_ref[...] = (hg * jax.nn.sigmoid(hg) * hu * ge).astype(o_ref.dtype)


def _ffn_up_call(a, wg, wu, i_layer, ms, tn):
    m, k = a.shape
    f = wg.shape[2]
    t = _Tiling(m, k, f, ms, tn)
    w_spec = pl.BlockSpec((None, t.ck, tn), lambda j, i: (i_layer, t.wrow(j, i), t.wcol(j)))
    vmem = 2 * (t.tm * k * 2 + 2 * t.ck * tn * 4 + t.tm * tn * 2) + 4 * k * tn * 2 + 3 * t.tm * tn * 4
    return pl.pallas_call(
        functools.partial(_ffn_up_kernel, t=t),
        grid=t.grid,
        in_specs=[pl.BlockSpec((t.tm, k), lambda j, i: (t.row(j, i), 0)), w_spec, w_spec],
        out_specs=pl.BlockSpec((t.tm, tn), lambda j, i: (t.row(j, i), t.col(j))),
        out_shape=jax.ShapeDtypeStruct((m, f), BF16),
        scratch_shapes=[pltpu.VMEM((2, k, tn), BF16), pltpu.VMEM((2, k, tn), BF16)],
        compiler_params=_params(("arbitrary", "arbitrary"), vmem),
        name="ffn_up",
    )(a, wg, wu)


def _moe_up_call(a, wg, wu, gates, i_layer, ms, per_tile):
    m, k = a.shape
    n_exp, fe = wg.shape[1], wg.shape[3]
    tn = per_tile * fe
    t = _Tiling(m, k, n_exp * fe, ms, tn)
    w_spec = pl.BlockSpec((None, per_tile, t.ck, fe), lambda j, i: (i_layer, t.wcol(j), t.wrow(j, i), 0))
    vmem = (2 * (t.tm * k * 2 + 2 * t.ck * tn * 4 + t.tm * tn * 2 + t.tm * ROUTER_LANES * 4) + 4 * k * tn * 2
            + 3 * t.tm * tn * 4)
    return pl.pallas_call(
        functools.partial(_moe_up_kernel, t=t, fe=fe),
        grid=t.grid,
        in_specs=[pl.BlockSpec((t.tm, k), lambda j, i: (t.row(j, i), 0)), w_spec, w_spec,
                  pl.BlockSpec((t.tm, ROUTER_LANES), lambda j, i: (t.row(j, i), 0))],
        out_specs=pl.BlockSpec((t.tm, tn), lambda j, i: (t.row(j, i), t.col(j))),
        out_shape=jax.ShapeDtypeStruct((m, n_exp * fe), BF16),
        scratch_shapes=[pltpu.VMEM((2, k, tn), BF16), pltpu.VMEM((2, k, tn), BF16)],
        compiler_params=_params(("arbitrary", "arbitrary"), vmem),
        name="moe_up",
    )(a, wg, wu, gates)


def _gmlp_kernel(u_ref, v_ref, lng_ref, lnb_ref, ws_ref, bs_ref, o_ref, *, chunks):
    u = jax.nn.gelu(u_ref[...].astype(F32))
    v = jax.nn.gelu(v_ref[...].astype(F32))
    mu = jnp.mean(v, axis=-1, keepdims=True)
    var = jnp.mean(jnp.square(v - mu), axis=-1, keepdims=True)
    vn = ((v - mu) * lax.rsqrt(var + EPS) * lng_ref[...] + lnb_ref[...]).astype(BF16)
    for g in range(GMLP_GROUPS):
        cols = slice(g * CHUNK, (g + 1) * CHUNK)
        wsg = ws_ref[g].astype(BF16)
        for c in range(chunks):
            rows = slice(c * CHUNK, (c + 1) * CHUNK)
            s = jnp.dot(wsg, vn[rows, cols], preferred_element_type=F32) + bs_ref[:, cols]
            o_ref[rows, cols] = (u[rows, cols] * s).astype(o_ref.dtype)


def _gmlp_call(px, lng, lnb, ws, bs_full, l, chunks):
    m = px.shape[0]
    t = chunks * CHUNK
    return pl.pallas_call(
        functools.partial(_gmlp_kernel, chunks=chunks),
        grid=(m // t,),
        in_specs=[pl.BlockSpec((t, BRANCH_W), lambda i: (i, 0)),
                  pl.BlockSpec((t, BRANCH_W), lambda i: (i, 1)),
                  pl.BlockSpec((None, 1, BRANCH_W), lambda i: (l, 0, 0)),
                  pl.BlockSpec((None, 1, BRANCH_W), lambda i: (l, 0, 0)),
                  pl.BlockSpec((None, GMLP_GROUPS, CHUNK, CHUNK), lambda i: (l, 0, 0, 0)),
                  pl.BlockSpec((None, CHUNK, BRANCH_W), lambda i: (l, 0, 0))],
        out_specs=pl.BlockSpec((t, BRANCH_W), lambda i: (i, 0)),
        out_shape=jax.ShapeDtypeStruct((m, BRANCH_W), BF16),
        compiler_params=_params(("arbitrary",), 6 * t * BRANCH_W * 2 + 6 * t * BRANCH_W * 4 + 2 * MIB),
        name="gmlp",
    )(px, px, lng.reshape(-1, 1, BRANCH_W), lnb.reshape(-1, 1, BRANCH_W), ws, bs_full)


def _dot_nt(a, b):
    return lax.dot_general(a, b, (((1,), (1,)), ((), ())), preferred_element_type=F32)


def _tree(op, xs):
    while len(xs) > 1:
        xs = [op(xs[i], xs[i + 1]) if i + 1 < len(xs) else xs[i] for i in range(0, len(xs), 2)]
    return xs[0]


def _with_ones(v):
    return jnp.concatenate([v, jnp.ones_like(v)], axis=1)


def _softmax_pv(s, v1_tiles, extra_logit=None):
    width = min(t.shape[1] for t in s)
    parts = [t[:, i:i + width] for t in s for i in range(0, t.shape[1], width)]
    m = jnp.max(_tree(jnp.maximum, parts), axis=-1, keepdims=True)
    if extra_logit is not None:
        m = jnp.maximum(m, extra_logit)
    o = _tree(jnp.add, [jnp.dot(jnp.exp2(t - m).astype(BF16), v1, preferred_element_type=F32)
                        for t, v1 in zip(s, v1_tiles)])
    hd = o.shape[1] // 2
    denom = o[:, hd:]
    if extra_logit is not None:
        denom = denom + jnp.exp2(extra_logit - m)
    return o[:, :hd] / denom


def _na_kernel(q_ref, k0_ref, k1_ref, k2_ref, v0_ref, v1_ref, v2_ref, kc_ref, vc_ref, pair_ref, mask_ref,
               o_ref, bias_ref, *, nb, roff):
    b = pl.program_id(0)
    left = lax.broadcasted_iota(jnp.int32, (GRID_W, 2 * GRID_W), 1) < GRID_W

    def build(kind):
        for h in range(NA_HEADS):
            for qr in range(NA_QROWS):
                rows = slice(qr * GRID_W, (qr + 1) * GRID_W)
                for p in range(NA_KROWS // 2):
                    cols = slice(p * 2 * GRID_W, (p + 1) * 2 * GRID_W)
                    pair = jnp.where(left, pair_ref[h, roff[kind][qr][2 * p]], pair_ref[h, roff[kind][qr][2 * p + 1]])
                    bias_ref[h, rows, cols] = pair * LOG2E + mask_ref[rows, cols]

    for kind, first_block in enumerate((0, 1, nb - 1)):
        pl.when(b == first_block)(functools.partial(build, kind))

    @pl.when(b == nb)
    def _():
        for h in range(NA_HEADS):
            bias_ref[h] = mask_ref[...]

    k_refs = (k0_ref, k1_ref, k2_ref)
    v_refs = (v0_ref, v1_ref, v2_ref)
    for h in range(NA_HEADS):
        cols = slice(h * HEAD_DIM, (h + 1) * HEAD_DIM)
        q = (q_ref[:, cols].astype(F32) * (HEAD_DIM ** -0.5 * LOG2E)).astype(BF16)
        s = [_dot_nt(q, k_refs[j][:, cols]) + bias_ref[h, :, j * NA_QB:(j + 1) * NA_QB] for j in range(NA_PIECES)]
        s.append(_dot_nt(q, kc_ref[:, cols]))
        v1 = [_with_ones(v_refs[j][:, cols]) for j in range(NA_PIECES)] + [_with_ones(vc_ref[:, cols])]
        o_ref[:, cols] = _softmax_pv(s, v1).astype(o_ref.dtype)


def _na_call(px, pairs, l, n_lat, off_q, off_k, off_v):
    m = px.shape[0]
    nb = n_lat // NA_QB
    assert nb >= 3
    w = BRANCH_W
    roff, mask = _na_geometry(n_lat // GRID_W)
    kstart = lambda b: jnp.clip(b - 1, 0, nb - NA_PIECES)
    k_spec = lambda off, j: pl.BlockSpec((NA_QB, w), lambda b: (kstart(b) + j, off // w))
    kind = lambda b: jnp.where(b == 0, 0, jnp.where(b == nb - 1, 2, jnp.where(b == nb, 3, 1)))
    n_rel = 2 * NA_WIN_R - 1
    vmem = (2 * (9 * NA_QB * w * 2 + NA_HEADS * n_rel * GRID_W * 2 * GRID_W * 4 + NA_QB * NA_KB * 4 + NA_QB * w * 2)
            + NA_HEADS * NA_QB * NA_KB * 4 + 16 * NA_QB * NA_QB * 4)
    return pl.pallas_call(
        functools.partial(_na_kernel, nb=nb, roff=roff),
        grid=(m // NA_QB,),
        in_specs=[pl.BlockSpec((NA_QB, w), lambda b: (b, off_q // w)),
                  k_spec(off_k, 0), k_spec(off_k, 1), k_spec(off_k, 2),
                  k_spec(off_v, 0), k_spec(off_v, 1), k_spec(off_v, 2),
                  pl.BlockSpec((NA_QB, w), lambda b: (nb, off_k // w)),
                  pl.BlockSpec((NA_QB, w), lambda b: (nb, off_v // w)),
                  pl.BlockSpec((None, NA_HEADS, n_rel, GRID_W, 2 * GRID_W), lambda b: (l, 0, 0, 0, 0)),
                  pl.BlockSpec((None, NA_QB, NA_KB), lambda b: (kind(b), 0, 0))],
        out_specs=pl.BlockSpec((NA_QB, w), lambda b: (b, 0)),
        out_shape=jax.ShapeDtypeStruct((m, w), BF16),
        scratch_shapes=[pltpu.VMEM((NA_HEADS, NA_QB, NA_KB), F32)],
        compiler_params=_params(("arbitrary",), vmem),
        name="neighbourhood_attention",
    )(px, px, px, px, px, px, px, px, px, pairs, mask)


def _na_geometry(rows):
    col = np.arange(GRID_W)
    c0 = np.clip(col - NA_WIN_C // 2, 0, GRID_W - NA_WIN_C)
    in_win = (col[None, :] >= c0[:, None]) & (col[None, :] < c0[:, None] + NA_WIN_C)
    roffs, masks = [], []
    for r0, s0 in ((0, 0), (NA_QROWS, 0), (rows - NA_QROWS, rows - NA_KROWS)):
        r = r0 + np.arange(NA_QROWS)
        start = np.clip(r - NA_WIN_R // 2, 0, rows - NA_WIN_R)
        key_row = s0 + np.arange(NA_KROWS)
        valid_r = (key_row[None, :] >= start[:, None]) & (key_row[None, :] < start[:, None] + NA_WIN_R)
        roff = np.clip(key_row[None, :] - r[:, None] + (NA_WIN_R - 1), 0, 2 * NA_WIN_R - 2)
        valid = valid_r[:, None, :, None] & in_win[None, :, None, :]
        roffs.append(tuple(tuple(int(v) for v in row) for row in roff))
        masks.append(np.where(valid, 0.0, NEG_INF).reshape(NA_QB, NA_KB))
    masks.append(np.full((NA_QB, NA_KB), NEG_INF))
    return tuple(roffs), jnp.asarray(np.stack(masks), F32)


def _na_bias_pairs(rpb):
    qcol = np.arange(GRID_W)
    kcol = np.arange(2 * GRID_W) % GRID_W
    coff = np.clip(kcol[None, :] - qcol[:, None] + (NA_WIN_C - 1), 0, 2 * NA_WIN_C - 2)
    pick = (coff[:, :, None] == np.arange(2 * NA_WIN_C - 1)).astype(np.float32)
    return jnp.einsum('lhij,cdj->lhicd', rpb.astype(F32), pick, precision=lax.Precision.HIGHEST)


def _swa_kernel(q_ref, csq_ref, kv0_ref, kv1_ref, kv2_ref, kv3_ref, cs0_ref, cs1_ref, cs2_ref, cs3_ref,
                kvc_ref, mask_ref, sink_ref, rot_ref, o_ref):
    rot = rot_ref[...]
    hd = HEAD_DIM

    def rope(x, cs):
        swapped = jnp.dot(x, rot, preferred_element_type=F32)
        return x.astype(F32) * cs[:, :hd] + swapped * cs[:, hd:]

    csq = csq_ref[...]
    kv_refs = (kv0_ref, kv1_ref, kv2_ref, kv3_ref)
    cs_refs = (cs0_ref, cs1_ref, cs2_ref, cs3_ref)
    for kv in range(SWA_KV_HEADS):
        kcols = slice(kv * hd, (kv + 1) * hd)
        vcols = slice(SWA_KV_W + kv * hd, SWA_KV_W + (kv + 1) * hd)
        k_tiles = [rope(kv_refs[j][:, kcols], cs_refs[j][...]).astype(BF16) for j in range(SWA_PIECES)]
        k_tiles.append(kvc_ref[:, kcols])
        v1 = [_with_ones(kv_refs[j][:, vcols]) for j in range(SWA_PIECES)] + [_with_ones(kvc_ref[:, vcols])]
        for h in range(kv * SWA_GROUP, (kv + 1) * SWA_GROUP):
            cols = slice(h * hd, (h + 1) * hd)
            q = (rope(q_ref[:, cols], csq) * (hd ** -0.5 * LOG2E)).astype(BF16)
            s = [_dot_nt(q, k_tiles[j]) + mask_ref[:, j * SWA_BLOCK:(j + 1) * SWA_BLOCK] for j in range(SWA_PIECES)]
            s.append(_dot_nt(q, k_tiles[SWA_PIECES]))
            o_ref[:, cols] = _softmax_pv(s, v1, extra_logit=sink_ref[h] * LOG2E).astype(o_ref.dtype)


def _swa_call(px, pkv, cs_t, mask, sink_col, rot, n_lat, n_ctx, off_q):
    m = px.shape[0]
    blk, w = SWA_BLOCK, BRANCH_W
    nbl = n_lat // blk
    nbq = n_lat // SWA_QB
    assert n_ctx == SWA_QB and nbq >= 2
    piece = lambda b, j: jnp.clip(2 * b - 1 + j, 0, nbl - 1)
    kind = lambda b: jnp.where(b == 0, 0, jnp.where(b == nbq - 1, 2, jnp.where(b == nbq, 3, 1)))
    kv_spec = lambda j: pl.BlockSpec((blk, 2 * SWA_KV_W), lambda b: (piece(b, j), 0))
    cs_spec = lambda j: pl.BlockSpec((blk, 2 * HEAD_DIM), lambda b: (piece(b, j), 0))
    vmem = (2 * (2 * SWA_QB * w * 2 + SWA_QB * SWA_PIECES * blk * 4 + SWA_HEADS * SWA_QB * HEAD_DIM * 4)
            + 16 * SWA_QB * (SWA_PIECES * blk + n_ctx) * 4)
    return pl.pallas_call(
        _swa_kernel,
        grid=(m // SWA_QB,),
        in_specs=[pl.BlockSpec((SWA_QB, w), lambda b: (b, off_q // w)),
                  pl.BlockSpec((SWA_QB, 2 * HEAD_DIM), lambda b: (b, 0)),
                  kv_spec(0), kv_spec(1), kv_spec(2), kv_spec(3),
                  cs_spec(0), cs_spec(1), cs_spec(2), cs_spec(3),
                  pl.BlockSpec((n_ctx, 2 * SWA_KV_W), lambda b: (n_lat // n_ctx, 0)),
                  pl.BlockSpec((None, SWA_QB, SWA_PIECES * blk), lambda b: (kind(b), 0, 0)),
                  pl.BlockSpec((SWA_HEADS, SWA_QB, 1), lambda b: (0, 0, 0)),
                  pl.BlockSpec((HEAD_DIM, HEAD_DIM), lambda b: (0, 0))],
        out_specs=pl.BlockSpec((SWA_QB, w), lambda b: (b, 0)),
        out_shape=jax.ShapeDtypeStruct((m, w), BF16),
        compiler_params=_params(("arbitrary",), vmem),
        name="window_gqa",
    )(px, cs_t, pkv, pkv, pkv, pkv, cs_t, cs_t, cs_t, cs_t, pkv, mask, sink_col, rot)


def _rope_tables(n_lat, n_ctx):
    t = jnp.arange(n_lat, dtype=jnp.int32)
    half = HEAD_DIM // 2
    inv = ROPE_BASE ** (-jnp.arange(0, half, 2, dtype=F32) / half)
    parts_c, parts_s = [], []
    for pos in (t // GRID_W, t % GRID_W):
        ang = pos.astype(F32)[:, None] * inv[None, :]
        parts_c += [jnp.cos(ang), jnp.cos(ang)]
        parts_s += [jnp.sin(ang), jnp.sin(ang)]
    cos_t = jnp.concatenate(parts_c, axis=-1)
    sin_t = jnp.concatenate(parts_s, axis=-1)
    cos_t = jnp.concatenate([cos_t, jnp.ones((n_ctx, HEAD_DIM), F32)], axis=0)
    sin_t = jnp.concatenate([sin_t, jnp.zeros((n_ctx, HEAD_DIM), F32)], axis=0)
    quarter = half // 2
    rot = np.zeros((HEAD_DIM, HEAD_DIM), np.float32)
    for lane in range(HEAD_DIM):
        if lane % half < quarter:
            rot[lane + quarter, lane] = -1.0
        else:
            rot[lane - quarter, lane] = 1.0
    return jnp.concatenate([cos_t, sin_t], axis=1), jnp.asarray(rot, BF16)


def _swa_mask():
    a = np.arange(SWA_QB)
    j = np.arange(SWA_PIECES * SWA_BLOCK)
    in_band = np.abs((j[None, :] - SWA_BLOCK) - a[:, None]) <= SWA_WINDOW
    last = (SWA_PIECES - 1) * SWA_BLOCK
    piece_ok = {0: j >= SWA_BLOCK, 1: j >= 0, 2: j < last, 3: j < 0}
    tabs = [np.where(in_band & piece_ok[kind][None, :], 0.0, NEG_INF) for kind in range(4)]
    return jnp.asarray(np.stack(tabs), F32)


def kernel(x, c, ctx, c_ctx, w_ada, b_ada, g_mix, w_in, gmlp_ln_g, gmlp_ln_b, gmlp_ws, gmlp_bs, na_rpb,
           swa_sink, w_branch, w_out, g_ffn, w_ffn_gate, w_ffn_up, w_ffn_down, w_router, w_exp_gate,
           w_exp_up, w_exp_down, g_final):
    batch, n_lat, d = x.shape
    n_ctx = ctx.shape[1]
    depth = w_ada.shape[0]
    assert batch == 1 and c.shape[0] == 1
    assert n_lat % NA_QB == 0 and n_ctx == NA_QB and n_lat // GRID_W >= NA_KROWS
    m = n_lat + n_ctx

    off_au, off_av = 0, BRANCH_W
    off_bq, off_cq = 2 * BRANCH_W, 3 * BRANCH_W
    off_gate = 4 * BRANCH_W
    off_bk = off_gate + N_BRANCHES * d
    off_bv = off_bk + BRANCH_W
    off_ck = off_bv + BRANCH_W
    off_cv = off_ck + SWA_KV_W
    assert w_in.shape[2] == off_cv + SWA_KV_W and off_au == 0 and off_av == BRANCH_W

    ms_big, ms_small = 8, 16
    gmlp_chunks = 6 if m % (6 * CHUNK) == 0 else 2

    xs = jnp.concatenate([x[0], ctx[0]], axis=0)
    c8 = jnp.concatenate([c, c_ctx[None, :], jnp.zeros((6, d), F32)], axis=0)
    mods = _ada_mods(c8, w_ada, b_ada)[:, :2].reshape(depth, 2, 6, 1, d)

    cs_t, rot = _rope_tables(n_lat, n_ctx)
    swa_mask = _swa_mask()
    na_pairs = _na_bias_pairs(na_rpb)
    bs_full = jnp.repeat(jnp.swapaxes(gmlp_bs, 1, 2), CHUNK, axis=2)
    sink_col = jnp.broadcast_to(swa_sink[:, :, None, None], (depth, SWA_HEADS, SWA_QB, 1))
    w_router_p = jnp.pad(w_router, ((0, 0), (0, 0), (0, ROUTER_LANES - N_EXPERTS)))
    n_moe, n_exp, _, fe = w_exp_gate.shape
    w_exp_down2 = w_exp_down.reshape(n_moe, n_exp * fe, d)

    for l in range(depth):
        i_layer = l // 2
        h = _norm_call(xs, g_mix, mods, l, 0, n_lat)
        px = _mm_call(h, w_in, l, 0, off_ck, ms_big, 1024, "in_proj")
        pkv = _mm_call(h, w_in, l, off_ck, 2 * SWA_KV_W, ms_big, 2 * SWA_KV_W, "in_proj_kv")
        y_a = _gmlp_call(px, gmlp_ln_g, gmlp_ln_b, gmlp_ws, bs_full, l, gmlp_chunks)
        y_b = _na_call(px, na_pairs, l, n_lat, off_bq, off_bk, off_bv)
        y_c = _swa_call(px, pkv, cs_t, swa_mask, sink_col[l], rot, n_lat, n_ctx, off_cq)
        acc = _merge_call(y_a, y_b, y_c, px, w_branch, l, off_gate, ms_small, 1024)
        xs = _mm_res_call(acc, w_out, l, xs, mods, l, 2, n_lat, ms_small, 1024)
        if l % 2 == 0:
            h = _norm_call(xs, g_ffn, mods, l, 3, n_lat)
            hid = _ffn_up_call(h, w_ffn_gate, w_ffn_up, i_layer, ms_big, 512)
            xs = _mm_res_call(hid, w_ffn_down, i_layer, xs, mods, l, 5, n_lat, ms_big, 1024)
        else:
            h, gates = _norm_call(xs, g_ffn, mods, l, 3, n_lat, w_router=w_router_p[i_layer])
            hid = _moe_up_call(h, w_exp_gate, w_exp_up, gates, i_layer, ms_small, 2)
            xs = _mm_res_call(hid, w_exp_down2, i_layer, xs, mods, l, 5, n_lat, ms_big, 1024)
    return _final_norm_call(xs, g_final, n_lat)[None]
```

```python
import functools

import numpy as np
import jax
import jax.numpy as jnp
from jax import lax
from jax.experimental import pallas as pl
from jax.experimental.pallas import tpu as pltpu

F32 = jnp.float32
BF16 = jnp.bfloat16

GRID_W = 64
HEAD_DIM = 128
BRANCH_W = 1024
N_BRANCHES = 3
CHUNK = 128
GMLP_GROUPS = 8
NA_HEADS = BRANCH_W // HEAD_DIM
NA_WIN_R = 8
NA_WIN_C = 16
SWA_HEADS = BRANCH_W // HEAD_DIM
SWA_KV_HEADS = 2
SWA_GROUP = SWA_HEADS // SWA_KV_HEADS
SWA_KV_W = SWA_KV_HEADS * HEAD_DIM
SWA_WINDOW = 128
SWA_BLOCK = 128
ROPE_BASE = 10000.0
N_EXPERTS = 8
TOP_K = 2
NEG_INF = -1e30
EPS = 1e-6

NA_QROWS = 4
NA_KROWS = 12
NA_QB = NA_QROWS * GRID_W
NA_KB = NA_KROWS * GRID_W
NA_PIECES = NA_KB // NA_QB

SWA_QB = 2 * SWA_BLOCK
SWA_PIECES = 4

LOG2E = 1.4426950408889634

ROUTER_LANES = 128

MIB = 1024 * 1024
V7X_VMEM_BYTES = 64 * MIB
COMPILER_SCRATCH_BYTES = 6 * MIB


def _params(sem, vmem_bytes):
    limit = vmem_bytes + COMPILER_SCRATCH_BYTES
    assert limit <= V7X_VMEM_BYTES, limit
    return pltpu.CompilerParams(dimension_semantics=sem, vmem_limit_bytes=limit)


def _ada_kernel(c_ref, w_ref, b_ref, o_ref):
    cs = c_ref[...]
    cs = (cs * jax.nn.sigmoid(cs)).astype(BF16)
    w = w_ref[...].astype(BF16)
    o_ref[...] = jnp.dot(cs, w, preferred_element_type=F32) + b_ref[...]


def _ada_mods(c8, w_ada, b_ada, tn=512):
    depth, d, n6 = w_ada.shape
    return pl.pallas_call(
        _ada_kernel,
        grid=(depth, n6 // tn),
        in_specs=[
            pl.BlockSpec((8, d), lambda l, j: (0, 0)),
            pl.BlockSpec((None, d, tn), lambda l, j: (l, 0, j)),
            pl.BlockSpec((None, 1, tn), lambda l, j: (l, 0, j)),
        ],
        out_specs=pl.BlockSpec((None, 8, tn), lambda l, j: (l, 0, j)),
        out_shape=jax.ShapeDtypeStruct((depth, 8, n6), F32),
        compiler_params=_params(("arbitrary", "arbitrary"), 2 * d * tn * 4 + d * tn * 2 + MIB),
        name="ada_mods",
    )(c8, w_ada, b_ada.reshape(depth, 1, n6))


def _norm_mod(x, g_ref, sh_ref, sc_ref):
    ms = jnp.mean(x * x, axis=-1, keepdims=True)
    y = x * lax.rsqrt(ms + EPS) * g_ref[...]
    return y * (1.0 + sc_ref[...]) + sh_ref[...]


def _route_top2(h, wr_ref):
    h_hi = h.astype(BF16)
    h_lo = (h - h_hi.astype(F32)).astype(BF16)
    logits = (jnp.dot(h_hi, wr_ref[0], preferred_element_type=F32)
              + jnp.dot(h_lo, wr_ref[0], preferred_element_type=F32)
              + jnp.dot(h_hi, wr_ref[1], preferred_element_type=F32))
    lane = lax.broadcasted_iota(jnp.int32, logits.shape, 1)
    logits = jnp.where(lane < N_EXPERTS, logits, -jnp.inf)
    m1 = jnp.max(logits, axis=-1, keepdims=True)
    i1 = jnp.min(jnp.where(logits == m1, lane, ROUTER_LANES), axis=-1, keepdims=True)
    first = lane == i1
    rest = jnp.where(first, -jnp.inf, logits)
    m2 = jnp.max(rest, axis=-1, keepdims=True)
    i2 = jnp.min(jnp.where(rest == m2, lane, ROUTER_LANES), axis=-1, keepdims=True)
    second = lane == i2
    e2 = jnp.exp(m2 - m1)
    w1 = 1.0 / (1.0 + e2)
    return jnp.where(first, w1, 0.0) + jnp.where(second, e2 * w1, 0.0)


def _swiglu(hg, hu):
    return hg * jax.nn.sigmoid(hg) * hu


def _expert_weights(gates, first_expert, per_tile, fe):
    lane = lax.broadcasted_iota(jnp.int32, gates.shape, 1)
    out_lane = lax.broadcasted_iota(jnp.int32, (1, per_tile * fe), 1)
    ge = None
    for e in range(per_tile):
        w_e = jnp.sum(jnp.where(lane == first_expert + e, gates, 0.0), axis=-1, keepdims=True)
        ge = w_e if ge is None else jnp.where(out_lane >= e * fe, w_e, ge)
    return ge


def _cast_chunks(w_refs, wb_refs, slot, rows):
    for w_ref, wb_ref in zip(w_refs, wb_refs):
        if len(w_ref.shape) == 3:
            fe = w_ref.shape[2]
            for e in range(w_ref.shape[0]):
                wb_ref[slot, rows, e * fe:(e + 1) * fe] = w_ref[e].astype(BF16)
        else:
            wb_ref[slot, rows, :] = w_ref[...].astype(BF16)


def _norm_first_kernel(*refs, p_steps, ck, n_w, kind, fe):
    x_ref, g_ref, sh_ref, sc_ref = refs[:4]
    w_refs = refs[4:4 + n_w]
    pos = 4 + n_w
    moe = kind == "moe"
    wr_ref = refs[pos] if moe else None
    pos += int(moe)
    h_ref, o_ref = refs[pos:pos + 2]
    pos += 2
    gate_ref = refs[pos] if moe else None
    wb_refs = refs[pos + int(moe):]
    s = pl.program_id(0)

    @pl.when(s < p_steps)
    def _():
        _cast_chunks(w_refs, wb_refs, 0, pl.ds(pl.multiple_of(s * ck, ck), ck))

    @pl.when(s >= p_steps)
    def _():
        hf = _norm_mod(x_ref[...], g_ref, sh_ref, sc_ref)
        h = hf.astype(BF16)
        h_ref[...] = h
        prods = [jnp.dot(h, wb_ref[0], preferred_element_type=F32) for wb_ref in wb_refs]
        if kind == "plain":
            out = prods[0]
        else:
            out = _swiglu(prods[0], prods[1])
        if moe:
            gates = _route_top2(hf, wr_ref)
            gate_ref[...] = gates
            out = out * _expert_weights(gates, 0, o_ref.shape[1] // fe, fe)
        o_ref[...] = out.astype(o_ref.dtype)


def _final_norm_kernel(x_ref, g_ref, o_ref):
    x = x_ref[...]
    ms = jnp.mean(x * x, axis=-1, keepdims=True)
    o_ref[...] = x * lax.rsqrt(ms + EPS) * g_ref[...]


NORM_ROWS = 256
NORM_STAGE_STEPS = 8


def _norm_first_call(x, g, mods, l, which, n_lat, weights, w_block, w_index, tn, kind, name, w_router=None, fe=0):
    m, d = x.shape
    tr, p_steps = NORM_ROWS, NORM_STAGE_STEPS
    ck = d // p_steps
    assert m % tr == 0 and n_lat % tr == 0 and d % p_steps == 0
    lat_tiles = n_lat // tr
    tile = lambda s: jnp.maximum(s - p_steps, 0)
    chunk = lambda s: jnp.minimum(s, p_steps - 1)
    row_kind = lambda s: jnp.where(tile(s) >= lat_tiles, 1, 0)
    mod_spec = lambda k: pl.BlockSpec((None, None, None, 1, d), lambda s: (l, row_kind(s), k, 0, 0))
    in_specs = [pl.BlockSpec((tr, d), lambda s: (tile(s), 0)),
                pl.BlockSpec((None, 1, d), lambda s: (l, 0, 0)),
                mod_spec(which), mod_spec(which + 1)]
    in_specs += [pl.BlockSpec(w_block(ck), lambda s: w_index(chunk(s))) for _ in weights]
    args = [x, g.reshape(g.shape[0], 1, d), mods, mods, *weights]
    out_specs = [pl.BlockSpec((tr, d), lambda s: (tile(s), 0)), pl.BlockSpec((tr, tn), lambda s: (tile(s), 0))]
    out_shape = [jax.ShapeDtypeStruct((m, d), BF16), jax.ShapeDtypeStruct((m, tn), BF16)]
    vmem = (2 * (tr * d * 4 + tr * d * 2 + tr * tn * 2 + len(weights) * ck * tn * 4) + len(weights) * d * tn * 2
            + 3 * tr * d * 4 + 3 * tr * tn * 4)
    if kind == "moe":
        in_specs.append(pl.BlockSpec((2, d, ROUTER_LANES), lambda s: (0, 0, 0)))
        args.append(w_router)
        out_specs.append(pl.BlockSpec((tr, ROUTER_LANES), lambda s: (tile(s), 0)))
        out_shape.append(jax.ShapeDtypeStruct((m, ROUTER_LANES), F32))
        vmem += 2 * 2 * d * ROUTER_LANES * 2 + 2 * tr * d * 4
    return pl.pallas_call(
        functools.partial(_norm_first_kernel, p_steps=p_steps, ck=ck, n_w=len(weights), kind=kind, fe=fe),
        grid=(p_steps + m // tr,),
        in_specs=in_specs,
        out_specs=out_specs,
        out_shape=out_shape,
        scratch_shapes=[pltpu.VMEM((1, d, tn), BF16) for _ in weights],
        compiler_params=_params(("arbitrary",), vmem),
        name=name,
    )(*args)


def _final_norm_call(x, g, n_lat, tr=256):
    d = x.shape[1]
    return pl.pallas_call(
        _final_norm_kernel,
        grid=(n_lat // tr,),
        in_specs=[pl.BlockSpec((tr, d), lambda i: (i, 0)),
                  pl.BlockSpec((1, d), lambda i: (0, 0))],
        out_specs=pl.BlockSpec((tr, d), lambda i: (i, 0)),
        out_shape=jax.ShapeDtypeStruct((n_lat, d), F32),
        compiler_params=_params(("arbitrary",), 6 * tr * d * 4),
        name="final_norm",
    )(x, g.reshape(1, d))


class _Tiling:
    def __init__(self, m, k, n, ms, tn):
        assert m % ms == 0 and k % ms == 0 and n % tn == 0
        self.ms, self.tn, self.nt = ms, tn, n // tn
        self.tm, self.ck = m // ms, k // ms
        assert self.tm % 16 == 0 and self.ck % 16 == 0
        self.grid = (self.nt + 1, ms)

    def row(self, j, i):
        return jnp.where(j == 0, 0, i)

    def col(self, j):
        return jnp.maximum(j - 1, 0)

    def wrow(self, j, i):
        return jnp.where(j == self.nt, self.ms - 1, i)

    def wcol(self, j):
        return jnp.minimum(j, self.nt - 1)


def _stage_weight(t, w_refs, wb_refs):
    j, i = pl.program_id(0), pl.program_id(1)

    @pl.when(j < t.nt)
    def _():
        _cast_chunks(w_refs, wb_refs, j % 2, pl.ds(pl.multiple_of(i * t.ck, t.ck), t.ck))


def _mm_kernel(a_ref, w_ref, o_ref, wb_ref, *, t):
    j = pl.program_id(0)
    _stage_weight(t, (w_ref,), (wb_ref,))

    @pl.when(j > 0)
    def _():
        o_ref[...] = jnp.dot(a_ref[...], wb_ref[(j + 1) % 2], preferred_element_type=F32).astype(o_ref.dtype)


def _mm_call(a, w, l, col0, ncols, ms, tn, name):
    m, k = a.shape
    t = _Tiling(m, k, ncols, ms, tn)
    assert col0 % tn == 0
    vmem = 2 * (t.tm * k * 2 + t.ck * tn * 4 + t.tm * tn * 2) + 2 * k * tn * 2 + t.tm * tn * 4
    return pl.pallas_call(
        functools.partial(_mm_kernel, t=t),
        grid=t.grid,
        in_specs=[pl.BlockSpec((t.tm, k), lambda j, i: (t.row(j, i), 0)),
                  pl.BlockSpec((None, t.ck, tn), lambda j, i: (l, t.wrow(j, i), col0 // tn + t.wcol(j)))],
        out_specs=pl.BlockSpec((t.tm, tn), lambda j, i: (t.row(j, i), t.col(j))),
        out_shape=jax.ShapeDtypeStruct((m, ncols), BF16),
        scratch_shapes=[pltpu.VMEM((2, k, tn), BF16)],
        compiler_params=_params(("arbitrary", "arbitrary"), vmem),
        name=name,
    )(a, w)


def _mm_res_kernel(*refs, t, n_lat, ks):
    a_refs = refs[:len(ks)]
    w_ref, x_ref, gx_ref, gc_ref, o_ref, wb_ref = refs[len(ks):]
    j, i = pl.program_id(0), pl.program_id(1)
    _stage_weight(t, (w_ref,), (wb_ref,))

    @pl.when(j > 0)
    def _():
        slot = (j + 1) % 2
        y, k0 = None, 0
        for a_ref, k in zip(a_refs, ks):
            part = jnp.dot(a_ref[...], wb_ref[slot, k0:k0 + k, :], preferred_element_type=F32)
            y = part if y is None else y + part
            k0 += k
        row = i * t.tm + lax.broadcasted_iota(jnp.int32, (t.tm, 1), 0)
        gate = jnp.where(row < n_lat, gx_ref[...], gc_ref[...])
        o_ref[...] = x_ref[...] + gate * y


def _mm_res_call(a_parts, w, l, x, mods, lm, which, n_lat, ms, tn):
    m = a_parts[0].shape[0]
    ks = tuple(a.shape[1] for a in a_parts)
    k = sum(ks)
    d = w.shape[2]
    assert w.shape[1] == k
    t = _Tiling(m, k, d, ms, tn)
    gate_spec = lambda kind: pl.BlockSpec((None, None, None, 1, tn),
                                          lambda j, i: (lm, kind, which, 0, t.col(j)))
    xo_spec = pl.BlockSpec((t.tm, tn), lambda j, i: (t.row(j, i), t.col(j)))
    vmem = 2 * (t.tm * k * 2 + t.ck * tn * 4 + 2 * t.tm * tn * 4) + 2 * k * tn * 2 + 2 * t.tm * tn * 4
    return pl.pallas_call(
        functools.partial(_mm_res_kernel, t=t, n_lat=n_lat, ks=ks),
        grid=t.grid,
        in_specs=[pl.BlockSpec((t.tm, kp), lambda j, i: (t.row(j, i), 0)) for kp in ks]
                 + [pl.BlockSpec((None, t.ck, tn), lambda j, i: (l, t.wrow(j, i), t.wcol(j))),
                    xo_spec, gate_spec(0), gate_spec(1)],
        out_specs=xo_spec,
        out_shape=jax.ShapeDtypeStruct((m, d), F32),
        scratch_shapes=[pltpu.VMEM((2, k, tn), BF16)],
        compiler_params=_params(("arbitrary", "arbitrary"), vmem),
        name="proj_residual",
    )(*a_parts, w, x, mods, mods)


def _merge_kernel(ya_ref, yb_ref, yc_ref, ga_ref, gb_ref, gc_ref, w_ref, o_ref, wb_ref, *, t, bw):
    j = pl.program_id(0)
    _stage_weight(t, (w_ref,), (wb_ref,))

    @pl.when(j > 0)
    def _():
        slot = (j + 1) % 2
        acc = None
        for br, (y_ref, g_ref) in enumerate(((ya_ref, ga_ref), (yb_ref, gb_ref), (yc_ref, gc_ref))):
            part = jnp.dot(y_ref[...], wb_ref[slot, br * bw:(br + 1) * bw, :], preferred_element_type=F32)
            part = jax.nn.sigmoid(g_ref[...].astype(F32)) * part
            acc = part if acc is None else acc + part
        o_ref[...] = acc.astype(o_ref.dtype)


def _merge_call(ya, yb, yc, px, w_branch, l, off_gate, ms, tn):
    m, bw = ya.shape
    d = w_branch.shape[3]
    k = N_BRANCHES * bw
    t = _Tiling(m, k, d, ms, tn)
    y_spec = pl.BlockSpec((t.tm, bw), lambda j, i: (t.row(j, i), 0))
    g_spec = lambda br: pl.BlockSpec((t.tm, tn), lambda j, i: (t.row(j, i), (off_gate + br * d) // tn + t.col(j)))
    vmem = (2 * (3 * t.tm * bw * 2 + 3 * t.tm * tn * 2 + t.ck * tn * 4 + t.tm * tn * 2) + 2 * k * tn * 2
            + 2 * t.tm * tn * 4)
    return pl.pallas_call(
        functools.partial(_merge_kernel, t=t, bw=bw),
        grid=t.grid,
        in_specs=[y_spec, y_spec, y_spec, g_spec(0), g_spec(1), g_spec(2),
                  pl.BlockSpec((None, t.ck, tn), lambda j, i: (l, t.wrow(j, i), t.wcol(j)))],
        out_specs=pl.BlockSpec((t.tm, tn), lambda j, i: (t.row(j, i), t.col(j))),
        out_shape=jax.ShapeDtypeStruct((m, d), BF16),
        scratch_shapes=[pltpu.VMEM((2, k, tn), BF16)],
        compiler_params=_params(("arbitrary", "arbitrary"), vmem),
        name="branch_merge",
    )(ya, yb, yc, px, px, px, w_branch.reshape(w_branch.shape[0], k, d))


def _ffn_up_kernel(a_ref, wg_ref, wu_ref, o_ref, wgb_ref, wub_ref, *, t):
    j = pl.program_id(0)
    _stage_weight(t, (wg_ref, wu_ref), (wgb_ref, wub_ref))

    @pl.when(j > 0)
    def _():
        slot = (j + 1) % 2
        a = a_ref[...]
        hg = jnp.dot(a, wgb_ref[slot], preferred_element_type=F32)
        hu = jnp.dot(a, wub_ref[slot], preferred_element_type=F32)
        o_ref[...] = _swiglu(hg, hu).astype(o_ref.dtype)


def _moe_up_kernel(a_ref, wg_ref, wu_ref, gate_ref, o_ref, wgb_ref, wub_ref, *, t, fe, tile0):
    j = pl.program_id(0)
    _stage_weight(t, (wg_ref, wu_ref), (wgb_ref, wub_ref))

    @pl.when(j > 0)
    def _():
        slot = (j + 1) % 2
        a = a_ref[...]
        hg = jnp.dot(a, wgb_ref[slot], preferred_element_type=F32)
        hu = jnp.dot(a, wub_ref[slot], preferred_element_type=F32)
        per_tile = t.tn // fe
        ge = _expert_weights(gate_ref[...], (tile0 + j - 1) * per_tile, per_tile, fe)
        o_ref[...] = (_swiglu(hg, hu) * ge).astype(o_ref.dtype)


def _ffn_up_call(a, wg, wu, i_layer, ms, tn, tile0):
    m, k = a.shape
    f = wg.shape[2] - tile0 * tn
    t = _Tiling(m, k, f, ms, tn)
    w_spec = pl.BlockSpec((None, t.ck, tn), lambda j, i: (i_layer, t.wrow(j, i), tile0 + t.wcol(j)))
    vmem = 2 * (t.tm * k * 2 + 2 * t.ck * tn * 4 + t.tm * tn * 2) + 4 * k * tn * 2 + 3 * t.tm * tn * 4
    return pl.pallas_call(
        functools.partial(_ffn_up_kernel, t=t),
        grid=t.grid,
        in_specs=[pl.BlockSpec((t.tm, k), lambda j, i: (t.row(j, i), 0)), w_spec, w_spec],
        out_specs=pl.BlockSpec((t.tm, tn), lambda j, i: (t.row(j, i), t.col(j))),
        out_shape=jax.ShapeDtypeStruct((m, f), BF16),
        scratch_shapes=[pltpu.VMEM((2, k, tn), BF16), pltpu.VMEM((2, k, tn), BF16)],
        compiler_params=_params(("arbitrary", "arbitrary"), vmem),
        name="ffn_up",
    )(a, wg, wu)


def _moe_up_call(a, wg, wu, gates, i_layer, ms, per_tile, tile0):
    m, k = a.shape
    n_exp, fe = wg.shape[1], wg.shape[3]
    tn = per_tile * fe
    t = _Tiling(m, k, n_exp * fe - tile0 * tn, ms, tn)
    w_spec = pl.BlockSpec((None, per_tile, t.ck, fe),
                          lambda j, i: (i_layer, tile0 + t.wcol(j), t.wrow(j, i), 0))
    vmem = (2 * (t.tm * k * 2 + 2 * t.ck * tn * 4 + t.tm * tn * 2 + t.tm * ROUTER_LANES * 4) + 4 * k * tn * 2
            + 3 * t.tm * tn * 4)
    return pl.pallas_call(
        functools.partial(_moe_up_kernel, t=t, fe=fe, tile0=tile0),
        grid=t.grid,
        in_specs=[pl.BlockSpec((t.tm, k), lambda j, i: (t.row(j, i), 0)), w_spec, w_spec,
                  pl.BlockSpec((t.tm, ROUTER_LANES), lambda j, i: (t.row(j, i), 0))],
        out_specs=pl.BlockSpec((t.tm, tn), lambda j, i: (t.row(j, i), t.col(j))),
        out_shape=jax.ShapeDtypeStruct((m, t.nt * tn), BF16),
        scratch_shapes=[pltpu.VMEM((2, k, tn), BF16), pltpu.VMEM((2, k, tn), BF16)],
        compiler_params=_params(("arbitrary", "arbitrary"), vmem),
        name="moe_up",
    )(a, wg, wu, gates)


def _gmlp_kernel(u_ref, v_ref, lng_ref, lnb_ref, ws_ref, bs_ref, o_ref, *, chunks):
    u = jax.nn.gelu(u_ref[...].astype(F32))
    v = jax.nn.gelu(v_ref[...].astype(F32))
    mu = jnp.mean(v, axis=-1, keepdims=True)
    var = jnp.mean(jnp.square(v - mu), axis=-1, keepdims=True)
    vn = ((v - mu) * lax.rsqrt(var + EPS) * lng_ref[...] + lnb_ref[...]).astype(BF16)
    for g in range(GMLP_GROUPS):
        cols = slice(g * CHUNK, (g + 1) * CHUNK)
        wsg = ws_ref[g].astype(BF16)
        for c in range(chunks):
            rows = slice(c * CHUNK, (c + 1) * CHUNK)
            s = jnp.dot(wsg, vn[rows, cols], preferred_element_type=F32) + bs_ref[:, cols]
            o_ref[rows, cols] = (u[rows, cols] * s).astype(o_ref.dtype)


def _gmlp_call(px, lng, lnb, ws, bs_full, l, chunks):
    m = px.shape[0]
    t = chunks * CHUNK
    return pl.pallas_call(
        functools.partial(_gmlp_kernel, chunks=chunks),
        grid=(m // t,),
        in_specs=[pl.BlockSpec((t, BRANCH_W), lambda i: (i, 0)),
                  pl.BlockSpec((t, BRANCH_W), lambda i: (i, 1)),
                  pl.BlockSpec((None, 1, BRANCH_W), lambda i: (l, 0, 0)),
                  pl.BlockSpec((None, 1, BRANCH_W), lambda i: (l, 0, 0)),
                  pl.BlockSpec((None, GMLP_GROUPS, CHUNK, CHUNK), lambda i: (l, 0, 0, 0)),
                  pl.BlockSpec((None, CHUNK, BRANCH_W), lambda i: (l, 0, 0))],
        out_specs=pl.BlockSpec((t, BRANCH_W), lambda i: (i, 0)),
        out_shape=jax.ShapeDtypeStruct((m, BRANCH_W), BF16),
        compiler_params=_params(("arbitrary",), 6 * t * BRANCH_W * 2 + 6 * t * BRANCH_W * 4 + 2 * MIB),
        name="gmlp",
    )(px, px, lng.reshape(-1, 1, BRANCH_W), lnb.reshape(-1, 1, BRANCH_W), ws, bs_full)


def _dot_nt(a, b):
    return lax.dot_general(a, b, (((1,), (1,)), ((), ())), preferred_element_type=F32)


def _tree(op, xs):
    while len(xs) > 1:
        xs = [op(xs[i], xs[i + 1]) if i + 1 < len(xs) else xs[i] for i in range(0, len(xs), 2)]
    return xs[0]


def _with_ones(v):
    return jnp.concatenate([v, jnp.ones_like(v)], axis=1)


def _softmax_pv(s, v1_tiles, extra_logit=None):
    width = min(t.shape[1] for t in s)
    parts = [t[:, i:i + width] for t in s for i in range(0, t.shape[1], width)]
    m = jnp.max(_tree(jnp.maximum, parts), axis=-1, keepdims=True)
    if extra_logit is not None:
        m = jnp.maximum(m, extra_logit)
    o = _tree(jnp.add, [jnp.dot(jnp.exp2(t - m).astype(BF16), v1, preferred_element_type=F32)
                        for t, v1 in zip(s, v1_tiles)])
    hd = o.shape[1] // 2
    denom = o[:, hd:]
    if extra_logit is not None:
        denom = denom + jnp.exp2(extra_logit - m)
    return o[:, :hd] / denom


def _na_kernel(q_ref, k0_ref, k1_ref, k2_ref, v0_ref, v1_ref, v2_ref, kc_ref, vc_ref, pair_ref, mask_ref,
               o_ref, bias_ref, *, nb, roff):
    b = pl.program_id(0)
    left = lax.broadcasted_iota(jnp.int32, (GRID_W, 2 * GRID_W), 1) < GRID_W

    def build(kind):
        for h in range(NA_HEADS):
            for qr in range(NA_QROWS):
                rows = slice(qr * GRID_W, (qr + 1) * GRID_W)
                for p in range(NA_KROWS // 2):
                    cols = slice(p * 2 * GRID_W, (p + 1) * 2 * GRID_W)
                    pair = jnp.where(left, pair_ref[h, roff[kind][qr][2 * p]], pair_ref[h, roff[kind][qr][2 * p + 1]])
                    bias_ref[h, rows, cols] = pair * LOG2E + mask_ref[rows, cols]

    for kind, first_block in enumerate((0, 1, nb - 1)):
        pl.when(b == first_block)(functools.partial(build, kind))

    @pl.when(b == nb)
    def _():
        for h in range(NA_HEADS):
            bias_ref[h] = mask_ref[...]

    k_refs = (k0_ref, k1_ref, k2_ref)
    v_refs = (v0_ref, v1_ref, v2_ref)
    for h in range(NA_HEADS):
        cols = slice(h * HEAD_DIM, (h + 1) * HEAD_DIM)
        q = (q_ref[:, cols].astype(F32) * (HEAD_DIM ** -0.5 * LOG2E)).astype(BF16)
        s = [_dot_nt(q, k_refs[j][:, cols]) + bias_ref[h, :, j * NA_QB:(j + 1) * NA_QB] for j in range(NA_PIECES)]
        s.append(_dot_nt(q, kc_ref[:, cols]))
        v1 = [_with_ones(v_refs[j][:, cols]) for j in range(NA_PIECES)] + [_with_ones(vc_ref[:, cols])]
        o_ref[:, cols] = _softmax_pv(s, v1).astype(o_ref.dtype)


def _na_call(px, pairs, l, n_lat, off_q, off_k, off_v):
    m = px.shape[0]
    nb = n_lat // NA_QB
    assert nb >= 3
    w = BRANCH_W
    roff, mask = _na_geometry(n_lat // GRID_W)
    kstart = lambda b: jnp.clip(b - 1, 0, nb - NA_PIECES)
    k_spec = lambda off, j: pl.BlockSpec((NA_QB, w), lambda b: (kstart(b) + j, off // w))
    kind = lambda b: jnp.where(b == 0, 0, jnp.where(b == nb - 1, 2, jnp.where(b == nb, 3, 1)))
    n_rel = 2 * NA_WIN_R - 1
    vmem = (2 * (9 * NA_QB * w * 2 + NA_HEADS * n_rel * GRID_W * 2 * GRID_W * 4 + NA_QB * NA_KB * 4 + NA_QB * w * 2)
            + NA_HEADS * NA_QB * NA_KB * 4 + 16 * NA_QB * NA_QB * 4)
    return pl.pallas_call(
        functools.partial(_na_kernel, nb=nb, roff=roff),
        grid=(m // NA_QB,),
        in_specs=[pl.BlockSpec((NA_QB, w), lambda b: (b, off_q // w)),
                  k_spec(off_k, 0), k_spec(off_k, 1), k_spec(off_k, 2),
                  k_spec(off_v, 0), k_spec(off_v, 1), k_spec(off_v, 2),
                  pl.BlockSpec((NA_QB, w), lambda b: (nb, off_k // w)),
                  pl.BlockSpec((NA_QB, w), lambda b: (nb, off_v // w)),
                  pl.BlockSpec((None, NA_HEADS, n_rel, GRID_W, 2 * GRID_W), lambda b: (l, 0, 0, 0, 0)),
                  pl.BlockSpec((None, NA_QB, NA_KB), lambda b: (kind(b), 0, 0))],
        out_specs=pl.BlockSpec((NA_QB, w), lambda b: (b, 0)),
        out_shape=jax.ShapeDtypeStruct((m, w), BF16),
        scratch_shapes=[pltpu.VMEM((NA_HEADS, NA_QB, NA_KB), F32)],
        compiler_params=_params(("arbitrary",), vmem),
        name="neighbourhood_attention",
    )(px, px, px, px, px, px, px, px, px, pairs, mask)


def _na_geometry(rows):
    col = np.arange(GRID_W)
    c0 = np.clip(col - NA_WIN_C // 2, 0, GRID_W - NA_WIN_C)
    in_win = (col[None, :] >= c0[:, None]) & (col[None, :] < c0[:, None] + NA_WIN_C)
    roffs, masks = [], []
    for r0, s0 in ((0, 0), (NA_QROWS, 0), (rows - NA_QROWS, rows - NA_KROWS)):
        r = r0 + np.arange(NA_QROWS)
        start = np.clip(r - NA_WIN_R // 2, 0, rows - NA_WIN_R)
        key_row = s0 + np.arange(NA_KROWS)
        valid_r = (key_row[None, :] >= start[:, None]) & (key_row[None, :] < start[:, None] + NA_WIN_R)
        roff = np.clip(key_row[None, :] - r[:, None] + (NA_WIN_R - 1), 0, 2 * NA_WIN_R - 2)
        valid = valid_r[:, None, :, None] & in_win[None, :, None, :]
        roffs.append(tuple(tuple(int(v) for v in row) for row in roff))
        masks.append(np.where(valid, 0.0, NEG_INF).reshape(NA_QB, NA_KB))
    masks.append(np.full((NA_QB, NA_KB), NEG_INF))
    return tuple(roffs), jnp.asarray(np.stack(masks), F32)


def _na_bias_pairs(rpb):
    qcol = np.arange(GRID_W)
    kcol = np.arange(2 * GRID_W) % GRID_W
    coff = np.clip(kcol[None, :] - qcol[:, None] + (NA_WIN_C - 1), 0, 2 * NA_WIN_C - 2)
    pick = (coff[:, :, None] == np.arange(2 * NA_WIN_C - 1)).astype(np.float32)
    return jnp.einsum('lhij,cdj->lhicd', rpb.astype(F32), pick, precision=lax.Precision.HIGHEST)


def _swa_kernel(q_ref, csq_ref, kv0_ref, kv1_ref, kv2_ref, kv3_ref, cs0_ref, cs1_ref, cs2_ref, cs3_ref,
                kvc_ref, mask_ref, sink_ref, rot_ref, o_ref):
    rot = rot_ref[...]
    hd = HEAD_DIM

    def rope(x, cs):
        swapped = jnp.dot(x, rot, preferred_element_type=F32)
        return x.astype(F32) * cs[:, :hd] + swapped * cs[:, hd:]

    csq = csq_ref[...]
    kv_refs = (kv0_ref, kv1_ref, kv2_ref, kv3_ref)
    cs_refs = (cs0_ref, cs1_ref, cs2_ref, cs3_ref)
    for kv in range(SWA_KV_HEADS):
        kcols = slice(kv * hd, (kv + 1) * hd)
        vcols = slice(SWA_KV_W + kv * hd, SWA_KV_W + (kv + 1) * hd)
        k_tiles = [rope(kv_refs[j][:, kcols], cs_refs[j][...]).astype(BF16) for j in range(SWA_PIECES)]
        k_tiles.append(kvc_ref[:, kcols])
        v1 = [_with_ones(kv_refs[j][:, vcols]) for j in range(SWA_PIECES)] + [_with_ones(kvc_ref[:, vcols])]
        for h in range(kv * SWA_GROUP, (kv + 1) * SWA_GROUP):
            cols = slice(h * hd, (h + 1) * hd)
            q = (rope(q_ref[:, cols], csq) * (hd ** -0.5 * LOG2E)).astype(BF16)
            s = [_dot_nt(q, k_tiles[j]) + mask_ref[:, j * SWA_BLOCK:(j + 1) * SWA_BLOCK] for j in range(SWA_PIECES)]
            s.append(_dot_nt(q, k_tiles[SWA_PIECES]))
            o_ref[:, cols] = _softmax_pv(s, v1, extra_logit=sink_ref[h] * LOG2E).astype(o_ref.dtype)


def _swa_call(px, pkv, cs_t, mask, sink_col, rot, n_lat, n_ctx, off_q):
    m = px.shape[0]
    blk, w = SWA_BLOCK, BRANCH_W
    nbl = n_lat // blk
    nbq = n_lat // SWA_QB
    assert n_ctx == SWA_QB and nbq >= 2
    piece = lambda b, j: jnp.clip(2 * b - 1 + j, 0, nbl - 1)
    kind = lambda b: jnp.where(b == 0, 0, jnp.where(b == nbq - 1, 2, jnp.where(b == nbq, 3, 1)))
    kv_spec = lambda j: pl.BlockSpec((blk, 2 * SWA_KV_W), lambda b: (piece(b, j), 0))
    cs_spec = lambda j: pl.BlockSpec((blk, 2 * HEAD_DIM), lambda b: (piece(b, j), 0))
    vmem = (2 * (2 * SWA_QB * w * 2 + SWA_QB * SWA_PIECES * blk * 4 + SWA_HEADS * SWA_QB * HEAD_DIM * 4)
            + 16 * SWA_QB * (SWA_PIECES * blk + n_ctx) * 4)
    return pl.pallas_call(
        _swa_kernel,
        grid=(m // SWA_QB,),
        in_specs=[pl.BlockSpec((SWA_QB, w), lambda b: (b, off_q // w)),
                  pl.BlockSpec((SWA_QB, 2 * HEAD_DIM), lambda b: (b, 0)),
                  kv_spec(0), kv_spec(1), kv_spec(2), kv_spec(3),
                  cs_spec(0), cs_spec(1), cs_spec(2), cs_spec(3),
                  pl.BlockSpec((n_ctx, 2 * SWA_KV_W), lambda b: (n_lat // n_ctx, 0)),
                  pl.BlockSpec((None, SWA_QB, SWA_PIECES * blk), lambda b: (kind(b), 0, 0)),
                  pl.BlockSpec((SWA_HEADS, SWA_QB, 1), lambda b: (0, 0, 0)),
                  pl.BlockSpec((HEAD_DIM, HEAD_DIM), lambda b: (0, 0))],
        out_specs=pl.BlockSpec((SWA_QB, w), lambda b: (b, 0)),
        out_shape=jax.ShapeDtypeStruct((m, w), BF16),
        compiler_params=_params(("arbitrary",), vmem),
        name="window_gqa",
    )(px, cs_t, pkv, pkv, pkv, pkv, cs_t, cs_t, cs_t, cs_t, pkv, mask, sink_col, rot)


def _rope_tables(n_lat, n_ctx):
    t = jnp.arange(n_lat, dtype=jnp.int32)
    half = HEAD_DIM // 2
    inv = ROPE_BASE ** (-jnp.arange(0, half, 2, dtype=F32) / half)
    parts_c, parts_s = [], []
    for pos in (t // GRID_W, t % GRID_W):
        ang = pos.astype(F32)[:, None] * inv[None, :]
        parts_c += [jnp.cos(ang), jnp.cos(ang)]
        parts_s += [jnp.sin(ang), jnp.sin(ang)]
    cos_t = jnp.concatenate(parts_c, axis=-1)
    sin_t = jnp.concatenate(parts_s, axis=-1)
    cos_t = jnp.concatenate([cos_t, jnp.ones((n_ctx, HEAD_DIM), F32)], axis=0)
    sin_t = jnp.concatenate([sin_t, jnp.zeros((n_ctx, HEAD_DIM), F32)], axis=0)
    quarter = half // 2
    rot = np.zeros((HEAD_DIM, HEAD_DIM), np.float32)
    for lane in range(HEAD_DIM):
        if lane % half < quarter:
            rot[lane + quarter, lane] = -1.0
        else:
            rot[lane - quarter, lane] = 1.0
    return jnp.concatenate([cos_t, sin_t], axis=1), jnp.asarray(rot, BF16)


def _swa_mask():
    a = np.arange(SWA_QB)
    j = np.arange(SWA_PIECES * SWA_BLOCK)
    in_band = np.abs((j[None, :] - SWA_BLOCK) - a[:, None]) <= SWA_WINDOW
    last = (SWA_PIECES - 1) * SWA_BLOCK
    piece_ok = {0: j >= SWA_BLOCK, 1: j >= 0, 2: j < last, 3: j < 0}
    tabs = [np.where(in_band & piece_ok[kind][None, :], 0.0, NEG_INF) for kind in range(4)]
    return jnp.asarray(np.stack(tabs), F32)


def kernel(x, c, ctx, c_ctx, w_ada, b_ada, g_mix, w_in, gmlp_ln_g, gmlp_ln_b, gmlp_ws, gmlp_bs, na_rpb,
           swa_sink, w_branch, w_out, g_ffn, w_ffn_gate, w_ffn_up, w_ffn_down, w_router, w_exp_gate,
           w_exp_up, w_exp_down, g_final):
    batch, n_lat, d = x.shape
    n_ctx = ctx.shape[1]
    depth = w_ada.shape[0]
    assert batch == 1 and c.shape[0] == 1
    assert n_lat % NA_QB == 0 and n_ctx == NA_QB and n_lat // GRID_W >= NA_KROWS
    m = n_lat + n_ctx

    off_au, off_av = 0, BRANCH_W
    off_bq, off_cq = 2 * BRANCH_W, 3 * BRANCH_W
    off_gate = 4 * BRANCH_W
    off_bk = off_gate + N_BRANCHES * d
    off_bv = off_bk + BRANCH_W
    off_ck = off_bv + BRANCH_W
    off_cv = off_ck + SWA_KV_W
    assert w_in.shape[2] == off_cv + SWA_KV_W and off_au == 0 and off_av == BRANCH_W

    ms_big, ms_small = 8, 16
    gmlp_chunks = 6 if m % (6 * CHUNK) == 0 else 2

    xs = jnp.concatenate([x[0], ctx[0]], axis=0)
    c8 = jnp.concatenate([c, c_ctx[None, :], jnp.zeros((6, d), F32)], axis=0)
    mods = _ada_mods(c8, w_ada, b_ada)[:, :2].reshape(depth, 2, 6, 1, d)

    cs_t, rot = _rope_tables(n_lat, n_ctx)
    swa_mask = _swa_mask()
    na_pairs = _na_bias_pairs(na_rpb)
    bs_full = jnp.repeat(jnp.swapaxes(gmlp_bs, 1, 2), CHUNK, axis=2)
    sink_col = jnp.broadcast_to(swa_sink[:, :, None, None], (depth, SWA_HEADS, SWA_QB, 1))
    w_router_p = jnp.pad(w_router, ((0, 0), (0, 0), (0, ROUTER_LANES - N_EXPERTS)))
    w_router_hi = w_router_p.astype(BF16)
    w_router_lo = (w_router_p - w_router_hi.astype(F32)).astype(BF16)
    w_router_t = jnp.stack([w_router_hi, w_router_lo], axis=1)
    n_moe, n_exp, _, fe = w_exp_gate.shape
    w_exp_down2 = w_exp_down.reshape(n_moe, n_exp * fe, d)

    for l in range(depth):
        i_layer = l // 2
        kvw = 2 * SWA_KV_W
        h, pkv = _norm_first_call(xs, g_mix, mods, l, 0, n_lat, [w_in], lambda ck: (None, ck, kvw),
                                  lambda c: (l, c, off_ck // kvw), kvw, "plain", "norm_in_proj_kv")
        px = _mm_call(h, w_in, l, 0, off_ck, ms_big, 1024, "in_proj")
        y_a = _gmlp_call(px, gmlp_ln_g, gmlp_ln_b, gmlp_ws, bs_full, l, gmlp_chunks)
        y_b = _na_call(px, na_pairs, l, n_lat, off_bq, off_bk, off_bv)
        y_c = _swa_call(px, pkv, cs_t, swa_mask, sink_col[l], rot, n_lat, n_ctx, off_cq)
        acc = _merge_call(y_a, y_b, y_c, px, w_branch, l, off_gate, ms_small, 1024)
        xs = _mm_res_call([acc], w_out, l, xs, mods, l, 2, n_lat, ms_small, 1024)
        if l % 2 == 0:
            tn_up = 512
            h, hid0 = _norm_first_call(xs, g_ffn, mods, l, 3, n_lat, [w_ffn_gate, w_ffn_up],
                                       lambda ck: (None, ck, tn_up), lambda c: (i_layer, c, 0), tn_up,
                                       "swiglu", "norm_ffn_up")
            hid1 = _ffn_up_call(h, w_ffn_gate, w_ffn_up, i_layer, ms_big, tn_up, 1)
            xs = _mm_res_call([hid0, hid1], w_ffn_down, i_layer, xs, mods, l, 5, n_lat, ms_big, 1024)
        else:
            pair = 2
            h, hid0, gates = _norm_first_call(xs, g_ffn, mods, l, 3, n_lat, [w_exp_gate, w_exp_up],
                                              lambda ck: (None, pair, ck, fe), lambda c: (i_layer, 0, c, 0),
                                              pair * fe, "moe", "norm_moe_up", w_router=w_router_t[i_layer], fe=fe)
            hid1 = _moe_up_call(h, w_exp_gate, w_exp_up, gates, i_layer, ms_small, pair, 1)
            xs = _mm_res_call([hid0, hid1], w_exp_down2, i_layer, xs, mods, l, 5, n_lat, ms_big, 1024)
    return _final_norm_call(xs, g_final, n_lat)[None]
```

```python
import functools

import numpy as np
import jax
import jax.numpy as jnp
from jax import lax
from jax.experimental import pallas as pl
from jax.experimental.pallas import tpu as pltpu

F32 = jnp.float32
BF16 = jnp.bfloat16

GRID_W = 64
HEAD_DIM = 128
BRANCH_W = 1024
N_BRANCHES = 3
CHUNK = 128
GMLP_GROUPS = 8
NA_HEADS = BRANCH_W // HEAD_DIM
NA_WIN_R = 8
NA_WIN_C = 16
SWA_HEADS = BRANCH_W // HEAD_DIM
SWA_KV_HEADS = 2
SWA_GROUP = SWA_HEADS // SWA_KV_HEADS
SWA_KV_W = SWA_KV_HEADS * HEAD_DIM
SWA_WINDOW = 128
SWA_BLOCK = 128
ROPE_BASE = 10000.0
N_EXPERTS = 8
TOP_K = 2
NEG_INF = -1e30
EPS = 1e-6

NA_QROWS = 4
NA_KROWS = 12
NA_QB = NA_QROWS * GRID_W
NA_KB = NA_KROWS * GRID_W
NA_PIECES = NA_KB // NA_QB

SWA_QB = 2 * SWA_BLOCK
SWA_PIECES = 4

LOG2E = 1.4426950408889634

ROUTER_LANES = 128

MIB = 1024 * 1024
V7X_VMEM_BYTES = 64 * MIB
COMPILER_SCRATCH_BYTES = 6 * MIB


def _params(sem, vmem_bytes):
    limit = vmem_bytes + COMPILER_SCRATCH_BYTES
    assert limit <= V7X_VMEM_BYTES, limit
    return pltpu.CompilerParams(dimension_semantics=sem, vmem_limit_bytes=limit)


def _ada_kernel(c_ref, w_ref, b_ref, o_ref):
    cs = c_ref[...]
    cs = (cs * jax.nn.sigmoid(cs)).astype(BF16)
    w = w_ref[...].astype(BF16)
    o_ref[...] = jnp.dot(cs, w, preferred_element_type=F32) + b_ref[...]


def _ada_mods(c8, w_ada, b_ada, tn=512):
    depth, d, n6 = w_ada.shape
    return pl.pallas_call(
        _ada_kernel,
        grid=(depth, n6 // tn),
        in_specs=[
            pl.BlockSpec((8, d), lambda l, j: (0, 0)),
            pl.BlockSpec((None, d, tn), lambda l, j: (l, 0, j)),
            pl.BlockSpec((None, 1, tn), lambda l, j: (l, 0, j)),
        ],
        out_specs=pl.BlockSpec((None, 8, tn), lambda l, j: (l, 0, j)),
        out_shape=jax.ShapeDtypeStruct((depth, 8, n6), F32),
        compiler_params=_params(("arbitrary", "arbitrary"), 2 * d * tn * 4 + d * tn * 2 + MIB),
        name="ada_mods",
    )(c8, w_ada, b_ada.reshape(depth, 1, n6))


def _norm_mod(x, g_ref, sh_ref, sc_ref):
    ms = jnp.mean(x * x, axis=-1, keepdims=True)
    y = x * lax.rsqrt(ms + EPS) * g_ref[...]
    return y * (1.0 + sc_ref[...]) + sh_ref[...]


def _route_top2(h, wr_ref):
    h_hi = h.astype(BF16)
    h_lo = (h - h_hi.astype(F32)).astype(BF16)
    logits = (jnp.dot(h_hi, wr_ref[0], preferred_element_type=F32)
              + jnp.dot(h_lo, wr_ref[0], preferred_element_type=F32)
              + jnp.dot(h_hi, wr_ref[1], preferred_element_type=F32))
    lane = lax.broadcasted_iota(jnp.int32, logits.shape, 1)
    logits = jnp.where(lane < N_EXPERTS, logits, -jnp.inf)
    m1 = jnp.max(logits, axis=-1, keepdims=True)
    i1 = jnp.min(jnp.where(logits == m1, lane, ROUTER_LANES), axis=-1, keepdims=True)
    first = lane == i1
    rest = jnp.where(first, -jnp.inf, logits)
    m2 = jnp.max(rest, axis=-1, keepdims=True)
    i2 = jnp.min(jnp.where(rest == m2, lane, ROUTER_LANES), axis=-1, keepdims=True)
    second = lane == i2
    e2 = jnp.exp(m2 - m1)
    w1 = 1.0 / (1.0 + e2)
    return jnp.where(first, w1, 0.0) + jnp.where(second, e2 * w1, 0.0)


def _swiglu(hg, hu):
    return hg * jax.nn.sigmoid(hg) * hu


def _expert_weights(gates, first_expert, per_tile, fe):
    lane = lax.broadcasted_iota(jnp.int32, gates.shape, 1)
    out_lane = lax.broadcasted_iota(jnp.int32, (1, per_tile * fe), 1)
    ge = None
    for e in range(per_tile):
        w_e = jnp.sum(jnp.where(lane == first_expert + e, gates, 0.0), axis=-1, keepdims=True)
        ge = w_e if ge is None else jnp.where(out_lane >= e * fe, w_e, ge)
    return ge


def _cast_chunks(w_refs, wb_refs, slot, rows):
    for w_ref, wb_ref in zip(w_refs, wb_refs):
        if len(w_ref.shape) == 3:
            fe = w_ref.shape[2]
            for e in range(w_ref.shape[0]):
                wb_ref[slot, rows, e * fe:(e + 1) * fe] = w_ref[e].astype(BF16)
        else:
            wb_ref[slot, rows, :] = w_ref[...].astype(BF16)


def _norm_first_kernel(*refs, p_steps, ck, n_w, kind, fe):
    x_ref, g_ref, sh_ref, sc_ref = refs[:4]
    w_refs = refs[4:4 + n_w]
    pos = 4 + n_w
    moe = kind == "moe"
    wr_ref = refs[pos] if moe else None
    pos += int(moe)
    h_ref, o_ref = refs[pos:pos + 2]
    pos += 2
    gate_ref = refs[pos] if moe else None
    wb_refs = refs[pos + int(moe):]
    s = pl.program_id(0)

    @pl.when(s < p_steps)
    def _():
        _cast_chunks(w_refs, wb_refs, 0, pl.ds(pl.multiple_of(s * ck, ck), ck))

    @pl.when(s >= p_steps)
    def _():
        hf = _norm_mod(x_ref[...], g_ref, sh_ref, sc_ref)
        h = hf.astype(BF16)
        h_ref[...] = h
        prods = [jnp.dot(h, wb_ref[0], preferred_element_type=F32) for wb_ref in wb_refs]
        if kind == "plain":
            out = prods[0] if len(prods) == 1 else jnp.concatenate(prods, axis=1)
        else:
            out = _swiglu(prods[0], prods[1])
        if moe:
            gates = _route_top2(hf, wr_ref)
            gate_ref[...] = gates
            out = out * _expert_weights(gates, 0, o_ref.shape[1] // fe, fe)
        o_ref[...] = out.astype(o_ref.dtype)


def _final_norm_kernel(x_ref, g_ref, o_ref):
    x = x_ref[...]
    ms = jnp.mean(x * x, axis=-1, keepdims=True)
    o_ref[...] = x * lax.rsqrt(ms + EPS) * g_ref[...]


NORM_ROWS = 256
NORM_STAGE_STEPS = 8


def _norm_first_call(x, g, mods, l, which, n_lat, weights, w_block, w_index, tn, kind, name, w_router=None, fe=0):
    out_cols = tn * len(weights) if kind == "plain" else tn
    m, d = x.shape
    tr, p_steps = NORM_ROWS, NORM_STAGE_STEPS
    ck = d // p_steps
    assert m % tr == 0 and n_lat % tr == 0 and d % p_steps == 0
    lat_tiles = n_lat // tr
    tile = lambda s: jnp.maximum(s - p_steps, 0)
    chunk = lambda s: jnp.minimum(s, p_steps - 1)
    row_kind = lambda s: jnp.where(tile(s) >= lat_tiles, 1, 0)
    mod_spec = lambda k: pl.BlockSpec((None, None, None, 1, d), lambda s: (l, row_kind(s), k, 0, 0))
    in_specs = [pl.BlockSpec((tr, d), lambda s: (tile(s), 0)),
                pl.BlockSpec((None, 1, d), lambda s: (l, 0, 0)),
                mod_spec(which), mod_spec(which + 1)]
    in_specs += [pl.BlockSpec(w_block(ck), functools.partial(lambda n, s: w_index(n, chunk(s)), n))
                 for n in range(len(weights))]
    args = [x, g.reshape(g.shape[0], 1, d), mods, mods, *weights]
    out_specs = [pl.BlockSpec((tr, d), lambda s: (tile(s), 0)),
                 pl.BlockSpec((tr, out_cols), lambda s: (tile(s), 0))]
    out_shape = [jax.ShapeDtypeStruct((m, d), BF16), jax.ShapeDtypeStruct((m, out_cols), BF16)]
    vmem = (2 * (tr * d * 4 + tr * d * 2 + tr * out_cols * 2 + len(weights) * ck * tn * 4)
            + len(weights) * d * tn * 2 + 3 * tr * d * 4 + 3 * tr * out_cols * 4)
    if kind == "moe":
        in_specs.append(pl.BlockSpec((2, d, ROUTER_LANES), lambda s: (0, 0, 0)))
        args.append(w_router)
        out_specs.append(pl.BlockSpec((tr, ROUTER_LANES), lambda s: (tile(s), 0)))
        out_shape.append(jax.ShapeDtypeStruct((m, ROUTER_LANES), F32))
        vmem += 2 * 2 * d * ROUTER_LANES * 2 + 2 * tr * d * 4
    return pl.pallas_call(
        functools.partial(_norm_first_kernel, p_steps=p_steps, ck=ck, n_w=len(weights), kind=kind, fe=fe),
        grid=(p_steps + m // tr,),
        in_specs=in_specs,
        out_specs=out_specs,
        out_shape=out_shape,
        scratch_shapes=[pltpu.VMEM((1, d, tn), BF16) for _ in weights],
        compiler_params=_params(("arbitrary",), vmem),
        name=name,
    )(*args)


def _final_norm_call(x, g, n_lat, tr=256):
    d = x.shape[1]
    return pl.pallas_call(
        _final_norm_kernel,
        grid=(n_lat // tr,),
        in_specs=[pl.BlockSpec((tr, d), lambda i: (i, 0)),
                  pl.BlockSpec((1, d), lambda i: (0, 0))],
        out_specs=pl.BlockSpec((tr, d), lambda i: (i, 0)),
        out_shape=jax.ShapeDtypeStruct((n_lat, d), F32),
        compiler_params=_params(("arbitrary",), 6 * tr * d * 4),
        name="final_norm",
    )(x, g.reshape(1, d))


class _Tiling:
    def __init__(self, m, k, n, ms, tn):
        assert m % ms == 0 and k % ms == 0 and n % tn == 0
        self.ms, self.tn, self.nt = ms, tn, n // tn
        self.tm, self.ck = m // ms, k // ms
        assert self.tm % 16 == 0 and self.ck % 16 == 0
        self.grid = (self.nt + 1, ms)

    def row(self, j, i):
        return jnp.where(j == 0, 0, i)

    def col(self, j):
        return jnp.maximum(j - 1, 0)

    def wrow(self, j, i):
        return jnp.where(j == self.nt, self.ms - 1, i)

    def wcol(self, j):
        return jnp.minimum(j, self.nt - 1)


def _stage_weight(t, w_refs, wb_refs):
    j, i = pl.program_id(0), pl.program_id(1)

    @pl.when(j < t.nt)
    def _():
        _cast_chunks(w_refs, wb_refs, j % 2, pl.ds(pl.multiple_of(i * t.ck, t.ck), t.ck))


def _mm_kernel(a_ref, w_ref, o_ref, wb_ref, *, t):
    j = pl.program_id(0)
    _stage_weight(t, (w_ref,), (wb_ref,))

    @pl.when(j > 0)
    def _():
        o_ref[...] = jnp.dot(a_ref[...], wb_ref[(j + 1) % 2], preferred_element_type=F32).astype(o_ref.dtype)


def _mm_call(a, w, l, col0, ncols, ms, tn, name):
    m, k = a.shape
    t = _Tiling(m, k, ncols, ms, tn)
    assert col0 % tn == 0
    vmem = 2 * (t.tm * k * 2 + t.ck * tn * 4 + t.tm * tn * 2) + 2 * k * tn * 2 + t.tm * tn * 4
    return pl.pallas_call(
        functools.partial(_mm_kernel, t=t),
        grid=t.grid,
        in_specs=[pl.BlockSpec((t.tm, k), lambda j, i: (t.row(j, i), 0)),
                  pl.BlockSpec((None, t.ck, tn), lambda j, i: (l, t.wrow(j, i), col0 // tn + t.wcol(j)))],
        out_specs=pl.BlockSpec((t.tm, tn), lambda j, i: (t.row(j, i), t.col(j))),
        out_shape=jax.ShapeDtypeStruct((m, ncols), BF16),
        scratch_shapes=[pltpu.VMEM((2, k, tn), BF16)],
        compiler_params=_params(("arbitrary", "arbitrary"), vmem),
        name=name,
    )(a, w)


def _mm_res_kernel(*refs, t, n_lat, ks):
    a_refs = refs[:len(ks)]
    w_ref, x_ref, gx_ref, gc_ref, o_ref, wb_ref = refs[len(ks):]
    j, i = pl.program_id(0), pl.program_id(1)
    _stage_weight(t, (w_ref,), (wb_ref,))

    @pl.when(j > 0)
    def _():
        slot = (j + 1) % 2
        y, k0 = None, 0
        for a_ref, k in zip(a_refs, ks):
            part = jnp.dot(a_ref[...], wb_ref[slot, k0:k0 + k, :], preferred_element_type=F32)
            y = part if y is None else y + part
            k0 += k
        row = i * t.tm + lax.broadcasted_iota(jnp.int32, (t.tm, 1), 0)
        gate = jnp.where(row < n_lat, gx_ref[...], gc_ref[...])
        o_ref[...] = x_ref[...] + gate * y


def _mm_res_call(a_parts, w, l, x, mods, lm, which, n_lat, ms, tn):
    m = a_parts[0].shape[0]
    ks = tuple(a.shape[1] for a in a_parts)
    k = sum(ks)
    d = w.shape[2]
    assert w.shape[1] == k
    t = _Tiling(m, k, d, ms, tn)
    gate_spec = lambda kind: pl.BlockSpec((None, None, None, 1, tn),
                                          lambda j, i: (lm, kind, which, 0, t.col(j)))
    xo_spec = pl.BlockSpec((t.tm, tn), lambda j, i: (t.row(j, i), t.col(j)))
    vmem = 2 * (t.tm * k * 2 + t.ck * tn * 4 + 2 * t.tm * tn * 4) + 2 * k * tn * 2 + 2 * t.tm * tn * 4
    return pl.pallas_call(
        functools.partial(_mm_res_kernel, t=t, n_lat=n_lat, ks=ks),
        grid=t.grid,
        in_specs=[pl.BlockSpec((t.tm, kp), lambda j, i: (t.row(j, i), 0)) for kp in ks]
                 + [pl.BlockSpec((None, t.ck, tn), lambda j, i: (l, t.wrow(j, i), t.wcol(j))),
                    xo_spec, gate_spec(0), gate_spec(1)],
        out_specs=xo_spec,
        out_shape=jax.ShapeDtypeStruct((m, d), F32),
        scratch_shapes=[pltpu.VMEM((2, k, tn), BF16)],
        compiler_params=_params(("arbitrary", "arbitrary"), vmem),
        name="proj_residual",
    )(*a_parts, w, x, mods, mods)


def _merge_kernel(ya_ref, yb_ref, yc_ref, ga_ref, gb_ref, gc_ref, w_ref, o_ref, wb_ref, *, t, bw):
    j = pl.program_id(0)
    _stage_weight(t, (w_ref,), (wb_ref,))

    @pl.when(j > 0)
    def _():
        slot = (j + 1) % 2
        acc = None
        for br, (y_ref, g_ref) in enumerate(((ya_ref, ga_ref), (yb_ref, gb_ref), (yc_ref, gc_ref))):
            part = jnp.dot(y_ref[...], wb_ref[slot, br * bw:(br + 1) * bw, :], preferred_element_type=F32)
            part = jax.nn.sigmoid(g_ref[...].astype(F32)) * part
            acc = part if acc is None else acc + part
        o_ref[...] = acc.astype(o_ref.dtype)


def _merge_call(ya, yb, yc, px, w_branch, l, off_gate, ms, tn):
    m, bw = ya.shape
    d = w_branch.shape[3]
    k = N_BRANCHES * bw
    t = _Tiling(m, k, d, ms, tn)
    y_spec = pl.BlockSpec((t.tm, bw), lambda j, i: (t.row(j, i), 0))
    g_spec = lambda br: pl.BlockSpec((t.tm, tn), lambda j, i: (t.row(j, i), (off_gate + br * d) // tn + t.col(j)))
    vmem = (2 * (3 * t.tm * bw * 2 + 3 * t.tm * tn * 2 + t.ck * tn * 4 + t.tm * tn * 2) + 2 * k * tn * 2
            + 2 * t.tm * tn * 4)
    return pl.pallas_call(
        functools.partial(_merge_kernel, t=t, bw=bw),
        grid=t.grid,
        in_specs=[y_spec, y_spec, y_spec, g_spec(0), g_spec(1), g_spec(2),
                  pl.BlockSpec((None, t.ck, tn), lambda j, i: (l, t.wrow(j, i), t.wcol(j)))],
        out_specs=pl.BlockSpec((t.tm, tn), lambda j, i: (t.row(j, i), t.col(j))),
        out_shape=jax.ShapeDtypeStruct((m, d), BF16),
        scratch_shapes=[pltpu.VMEM((2, k, tn), BF16)],
        compiler_params=_params(("arbitrary", "arbitrary"), vmem),
        name="branch_merge",
    )(ya, yb, yc, px, px, px, w_branch.reshape(w_branch.shape[0], k, d))


def _ffn_up_kernel(a_ref, wg_ref, wu_ref, o_ref, wgb_ref, wub_ref, *, t):
    j = pl.program_id(0)
    _stage_weight(t, (wg_ref, wu_ref), (wgb_ref, wub_ref))

    @pl.when(j > 0)
    def _():
        slot = (j + 1) % 2
        a = a_ref[...]
        hg = jnp.dot(a, wgb_ref[slot], preferred_element_type=F32)
        hu = jnp.dot(a, wub_ref[slot], preferred_element_type=F32)
        o_ref[...] = _swiglu(hg, hu).astype(o_ref.dtype)


def _moe_up_kernel(a_ref, wg_ref, wu_ref, gate_ref, o_ref, wgb_ref, wub_ref, *, t, fe, tile0):
    j = pl.program_id(0)
    _stage_weight(t, (wg_ref, wu_ref), (wgb_ref, wub_ref))

    @pl.when(j > 0)
    def _():
        slot = (j + 1) % 2
        a = a_ref[...]
        hg = jnp.dot(a, wgb_ref[slot], preferred_element_type=F32)
        hu = jnp.dot(a, wub_ref[slot], preferred_element_type=F32)
        per_tile = t.tn // fe
        ge = _expert_weights(gate_ref[...], (tile0 + j - 1) * per_tile, per_tile, fe)
        o_ref[...] = (_swiglu(hg, hu) * ge).astype(o_ref.dtype)


def _ffn_up_call(a, wg, wu, i_layer, ms, tn, tile0):
    m, k = a.shape
    f = wg.shape[2] - tile0 * tn
    t = _Tiling(m, k, f, ms, tn)
    w_spec = pl.BlockSpec((None, t.ck, tn), lambda j, i: (i_layer, t.wrow(j, i), tile0 + t.wcol(j)))
    vmem = 2 * (t.tm * k * 2 + 2 * t.ck * tn * 4 + t.tm * tn * 2) + 4 * k * tn * 2 + 3 * t.tm * tn * 4
    return pl.pallas_call(
        functools.partial(_ffn_up_kernel, t=t),
        grid=t.grid,
        in_specs=[pl.BlockSpec((t.tm, k), lambda j, i: (t.row(j, i), 0)), w_spec, w_spec],
        out_specs=pl.BlockSpec((t.tm, tn), lambda j, i: (t.row(j, i), t.col(j))),
        out_shape=jax.ShapeDtypeStruct((m, f), BF16),
        scratch_shapes=[pltpu.VMEM((2, k, tn), BF16), pltpu.VMEM((2, k, tn), BF16)],
        compiler_params=_params(("arbitrary", "arbitrary"), vmem),
        name="ffn_up",
    )(a, wg, wu)


def _moe_up_call(a, wg, wu, gates, i_layer, ms, per_tile, tile0):
    m, k = a.shape
    n_exp, fe = wg.shape[1], wg.shape[3]
    tn = per_tile * fe
    t = _Tiling(m, k, n_exp * fe - tile0 * tn, ms, tn)
    w_spec = pl.BlockSpec((None, per_tile, t.ck, fe),
                          lambda j, i: (i_layer, tile0 + t.wcol(j), t.wrow(j, i), 0))
    vmem = (2 * (t.tm * k * 2 + 2 * t.ck * tn * 4 + t.tm * tn * 2 + t.tm * ROUTER_LANES * 4) + 4 * k * tn * 2
            + 3 * t.tm * tn * 4)
    return pl.pallas_call(
        functools.partial(_moe_up_kernel, t=t, fe=fe, tile0=tile0),
        grid=t.grid,
        in_specs=[pl.BlockSpec((t.tm, k), lambda j, i: (t.row(j, i), 0)), w_spec, w_spec,
                  pl.BlockSpec((t.tm, ROUTER_LANES), lambda j, i: (t.row(j, i), 0))],
        out_specs=pl.BlockSpec((t.tm, tn), lambda j, i: (t.row(j, i), t.col(j))),
        out_shape=jax.ShapeDtypeStruct((m, t.nt * tn), BF16),
        scratch_shapes=[pltpu.VMEM((2, k, tn), BF16), pltpu.VMEM((2, k, tn), BF16)],
        compiler_params=_params(("arbitrary", "arbitrary"), vmem),
        name="moe_up",
    )(a, wg, wu, gates)


def _gmlp_kernel(u_ref, v_ref, lng_ref, lnb_ref, ws_ref, bs_ref, o_ref, *, chunks):
    u = jax.nn.gelu(u_ref[...].astype(F32))
    v = jax.nn.gelu(v_ref[...].astype(F32))
    mu = jnp.mean(v, axis=-1, keepdims=True)
    var = jnp.mean(jnp.square(v - mu), axis=-1, keepdims=True)
    vn = ((v - mu) * lax.rsqrt(var + EPS) * lng_ref[...] + lnb_ref[...]).astype(BF16)
    for g in range(GMLP_GROUPS):
        cols = slice(g * CHUNK, (g + 1) * CHUNK)
        wsg = ws_ref[g].astype(BF16)
        for c in range(chunks):
            rows = slice(c * CHUNK, (c + 1) * CHUNK)
            s = jnp.dot(wsg, vn[rows, cols], preferred_element_type=F32) + bs_ref[:, cols]
            o_ref[rows, cols] = (u[rows, cols] * s).astype(o_ref.dtype)


def _gmlp_call(px, lng, lnb, ws, bs_full, l, chunks):
    m = px.shape[0]
    t = chunks * CHUNK
    return pl.pallas_call(
        functools.partial(_gmlp_kernel, chunks=chunks),
        grid=(m // t,),
        in_specs=[pl.BlockSpec((t, BRANCH_W), lambda i: (i, 0)),
                  pl.BlockSpec((t, BRANCH_W), lambda i: (i, 1)),
                  pl.BlockSpec((None, 1, BRANCH_W), lambda i: (l, 0, 0)),
                  pl.BlockSpec((None, 1, BRANCH_W), lambda i: (l, 0, 0)),
                  pl.BlockSpec((None, GMLP_GROUPS, CHUNK, CHUNK), lambda i: (l, 0, 0, 0)),
                  pl.BlockSpec((None, CHUNK, BRANCH_W), lambda i: (l, 0, 0))],
        out_specs=pl.BlockSpec((t, BRANCH_W), lambda i: (i, 0)),
        out_shape=jax.ShapeDtypeStruct((m, BRANCH_W), BF16),
        compiler_params=_params(("arbitrary",), 6 * t * BRANCH_W * 2 + 6 * t * BRANCH_W * 4 + 2 * MIB),
        name="gmlp",
    )(px, px, lng.reshape(-1, 1, BRANCH_W), lnb.reshape(-1, 1, BRANCH_W), ws, bs_full)


def _dot_nt(a, b):
    return lax.dot_general(a, b, (((1,), (1,)), ((), ())), preferred_element_type=F32)


def _tree(op, xs):
    while len(xs) > 1:
        xs = [op(xs[i], xs[i + 1]) if i + 1 < len(xs) else xs[i] for i in range(0, len(xs), 2)]
    return xs[0]


def _with_ones(v):
    return jnp.concatenate([v, jnp.ones_like(v)], axis=1)


def _softmax_pv(s, v1_tiles, extra_logit=None):
    width = min(t.shape[1] for t in s)
    parts = [t[:, i:i + width] for t in s for i in range(0, t.shape[1], width)]
    m = jnp.max(_tree(jnp.maximum, parts), axis=-1, keepdims=True)
    if extra_logit is not None:
        m = jnp.maximum(m, extra_logit)
    o = _tree(jnp.add, [jnp.dot(jnp.exp2(t - m).astype(BF16), v1, preferred_element_type=F32)
                        for t, v1 in zip(s, v1_tiles)])
    hd = o.shape[1] // 2
    denom = o[:, hd:]
    if extra_logit is not None:
        denom = denom + jnp.exp2(extra_logit - m)
    return o[:, :hd] / denom


def _na_kernel(q_ref, k0_ref, k1_ref, k2_ref, v0_ref, v1_ref, v2_ref, kc_ref, vc_ref, pair_ref, mask_ref,
               o_ref, bias_ref, *, nb, roff):
    b = pl.program_id(0)
    left = lax.broadcasted_iota(jnp.int32, (GRID_W, 2 * GRID_W), 1) < GRID_W

    def build(kind):
        for h in range(NA_HEADS):
            for qr in range(NA_QROWS):
                rows = slice(qr * GRID_W, (qr + 1) * GRID_W)
                for p in range(NA_KROWS // 2):
                    cols = slice(p * 2 * GRID_W, (p + 1) * 2 * GRID_W)
                    pair = jnp.where(left, pair_ref[h, roff[kind][qr][2 * p]], pair_ref[h, roff[kind][qr][2 * p + 1]])
                    bias_ref[h, rows, cols] = pair * LOG2E + mask_ref[rows, cols]

    for kind, first_block in enumerate((0, 1, nb - 1)):
        pl.when(b == first_block)(functools.partial(build, kind))

    @pl.when(b == nb)
    def _():
        for h in range(NA_HEADS):
            bias_ref[h] = mask_ref[...]

    k_refs = (k0_ref, k1_ref, k2_ref)
    v_refs = (v0_ref, v1_ref, v2_ref)
    for h in range(NA_HEADS):
        cols = slice(h * HEAD_DIM, (h + 1) * HEAD_DIM)
        q = (q_ref[:, cols].astype(F32) * (HEAD_DIM ** -0.5 * LOG2E)).astype(BF16)
        s = [_dot_nt(q, k_refs[j][:, cols]) + bias_ref[h, :, j * NA_QB:(j + 1) * NA_QB] for j in range(NA_PIECES)]
        s.append(_dot_nt(q, kc_ref[:, cols]))
        v1 = [_with_ones(v_refs[j][:, cols]) for j in range(NA_PIECES)] + [_with_ones(vc_ref[:, cols])]
        o_ref[:, cols] = _softmax_pv(s, v1).astype(o_ref.dtype)


def _na_call(px, pv, pairs, l, n_lat, off_q, off_k, off_v):
    m = px.shape[0]
    nb = n_lat // NA_QB
    assert nb >= 3
    w = BRANCH_W
    roff, mask = _na_geometry(n_lat // GRID_W)
    kstart = lambda b: jnp.clip(b - 1, 0, nb - NA_PIECES)
    k_spec = lambda off, j: pl.BlockSpec((NA_QB, w), lambda b: (kstart(b) + j, off // w))
    kind = lambda b: jnp.where(b == 0, 0, jnp.where(b == nb - 1, 2, jnp.where(b == nb, 3, 1)))
    n_rel = 2 * NA_WIN_R - 1
    vmem = (2 * (9 * NA_QB * w * 2 + NA_HEADS * n_rel * GRID_W * 2 * GRID_W * 4 + NA_QB * NA_KB * 4 + NA_QB * w * 2)
            + NA_HEADS * NA_QB * NA_KB * 4 + 16 * NA_QB * NA_QB * 4)
    return pl.pallas_call(
        functools.partial(_na_kernel, nb=nb, roff=roff),
        grid=(m // NA_QB,),
        in_specs=[pl.BlockSpec((NA_QB, w), lambda b: (b, off_q // w)),
                  k_spec(off_k, 0), k_spec(off_k, 1), k_spec(off_k, 2),
                  k_spec(off_v, 0), k_spec(off_v, 1), k_spec(off_v, 2),
                  pl.BlockSpec((NA_QB, w), lambda b: (nb, off_k // w)),
                  pl.BlockSpec((NA_QB, w), lambda b: (nb, off_v // w)),
                  pl.BlockSpec((None, NA_HEADS, n_rel, GRID_W, 2 * GRID_W), lambda b: (l, 0, 0, 0, 0)),
                  pl.BlockSpec((None, NA_QB, NA_KB), lambda b: (kind(b), 0, 0))],
        out_specs=pl.BlockSpec((NA_QB, w), lambda b: (b, 0)),
        out_shape=jax.ShapeDtypeStruct((m, w), BF16),
        scratch_shapes=[pltpu.VMEM((NA_HEADS, NA_QB, NA_KB), F32)],
        compiler_params=_params(("arbitrary",), vmem),
        name="neighbourhood_attention",
    )(px, px, px, px, pv, pv, pv, px, pv, pairs, mask)


def _na_geometry(rows):
    col = np.arange(GRID_W)
    c0 = np.clip(col - NA_WIN_C // 2, 0, GRID_W - NA_WIN_C)
    in_win = (col[None, :] >= c0[:, None]) & (col[None, :] < c0[:, None] + NA_WIN_C)
    roffs, masks = [], []
    for r0, s0 in ((0, 0), (NA_QROWS, 0), (rows - NA_QROWS, rows - NA_KROWS)):
        r = r0 + np.arange(NA_QROWS)
        start = np.clip(r - NA_WIN_R // 2, 0, rows - NA_WIN_R)
        key_row = s0 + np.arange(NA_KROWS)
        valid_r = (key_row[None, :] >= start[:, None]) & (key_row[None, :] < start[:, None] + NA_WIN_R)
        roff = np.clip(key_row[None, :] - r[:, None] + (NA_WIN_R - 1), 0, 2 * NA_WIN_R - 2)
        valid = valid_r[:, None, :, None] & in_win[None, :, None, :]
        roffs.append(tuple(tuple(int(v) for v in row) for row in roff))
        masks.append(np.where(valid, 0.0, NEG_INF).reshape(NA_QB, NA_KB))
    masks.append(np.full((NA_QB, NA_KB), NEG_INF))
    return tuple(roffs), jnp.asarray(np.stack(masks), F32)


def _na_bias_pairs(rpb):
    qcol = np.arange(GRID_W)
    kcol = np.arange(2 * GRID_W) % GRID_W
    coff = np.clip(kcol[None, :] - qcol[:, None] + (NA_WIN_C - 1), 0, 2 * NA_WIN_C - 2)
    pick = (coff[:, :, None] == np.arange(2 * NA_WIN_C - 1)).astype(np.float32)
    return jnp.einsum('lhij,cdj->lhicd', rpb.astype(F32), pick, precision=lax.Precision.HIGHEST)


def _swa_kernel(q_ref, csq_ref, kv0_ref, kv1_ref, kv2_ref, kv3_ref, cs0_ref, cs1_ref, cs2_ref, cs3_ref,
                kvc_ref, mask_ref, sink_ref, rot_ref, o_ref):
    rot = rot_ref[...]
    hd = HEAD_DIM

    def rope(x, cs):
        swapped = jnp.dot(x, rot, preferred_element_type=F32)
        return x.astype(F32) * cs[:, :hd] + swapped * cs[:, hd:]

    csq = csq_ref[...]
    kv_refs = (kv0_ref, kv1_ref, kv2_ref, kv3_ref)
    cs_refs = (cs0_ref, cs1_ref, cs2_ref, cs3_ref)
    for kv in range(SWA_KV_HEADS):
        kcols = slice(kv * hd, (kv + 1) * hd)
        vcols = slice(SWA_KV_W + kv * hd, SWA_KV_W + (kv + 1) * hd)
        k_tiles = [rope(kv_refs[j][:, kcols], cs_refs[j][...]).astype(BF16) for j in range(SWA_PIECES)]
        k_tiles.append(kvc_ref[:, kcols])
        v1 = [_with_ones(kv_refs[j][:, vcols]) for j in range(SWA_PIECES)] + [_with_ones(kvc_ref[:, vcols])]
        for h in range(kv * SWA_GROUP, (kv + 1) * SWA_GROUP):
            cols = slice(h * hd, (h + 1) * hd)
            q = (rope(q_ref[:, cols], csq) * (hd ** -0.5 * LOG2E)).astype(BF16)
            s = [_dot_nt(q, k_tiles[j]) + mask_ref[:, j * SWA_BLOCK:(j + 1) * SWA_BLOCK] for j in range(SWA_PIECES)]
            s.append(_dot_nt(q, k_tiles[SWA_PIECES]))
            o_ref[:, cols] = _softmax_pv(s, v1, extra_logit=sink_ref[h] * LOG2E).astype(o_ref.dtype)


def _swa_call(px, pkv, cs_t, mask, sink_col, rot, n_lat, n_ctx, off_q, off_kv):
    kvb = off_kv // (2 * SWA_KV_W)
    assert off_kv % (2 * SWA_KV_W) == 0
    m = px.shape[0]
    blk, w = SWA_BLOCK, BRANCH_W
    nbl = n_lat // blk
    nbq = n_lat // SWA_QB
    assert n_ctx == SWA_QB and nbq >= 2
    piece = lambda b, j: jnp.clip(2 * b - 1 + j, 0, nbl - 1)
    kind = lambda b: jnp.where(b == 0, 0, jnp.where(b == nbq - 1, 2, jnp.where(b == nbq, 3, 1)))
    kv_spec = lambda j: pl.BlockSpec((blk, 2 * SWA_KV_W), lambda b: (piece(b, j), kvb))
    cs_spec = lambda j: pl.BlockSpec((blk, 2 * HEAD_DIM), lambda b: (piece(b, j), 0))
    vmem = (2 * (2 * SWA_QB * w * 2 + SWA_QB * SWA_PIECES * blk * 4 + SWA_HEADS * SWA_QB * HEAD_DIM * 4)
            + 16 * SWA_QB * (SWA_PIECES * blk + n_ctx) * 4)
    return pl.pallas_call(
        _swa_kernel,
        grid=(m // SWA_QB,),
        in_specs=[pl.BlockSpec((SWA_QB, w), lambda b: (b, off_q // w)),
                  pl.BlockSpec((SWA_QB, 2 * HEAD_DIM), lambda b: (b, 0)),
                  kv_spec(0), kv_spec(1), kv_spec(2), kv_spec(3),
                  cs_spec(0), cs_spec(1), cs_spec(2), cs_spec(3),
                  pl.BlockSpec((n_ctx, 2 * SWA_KV_W), lambda b: (n_lat // n_ctx, kvb)),
                  pl.BlockSpec((None, SWA_QB, SWA_PIECES * blk), lambda b: (kind(b), 0, 0)),
                  pl.BlockSpec((SWA_HEADS, SWA_QB, 1), lambda b: (0, 0, 0)),
                  pl.BlockSpec((HEAD_DIM, HEAD_DIM), lambda b: (0, 0))],
        out_specs=pl.BlockSpec((SWA_QB, w), lambda b: (b, 0)),
        out_shape=jax.ShapeDtypeStruct((m, w), BF16),
        compiler_params=_params(("arbitrary",), vmem),
        name="window_gqa",
    )(px, cs_t, pkv, pkv, pkv, pkv, cs_t, cs_t, cs_t, cs_t, pkv, mask, sink_col, rot)


def _rope_tables(n_lat, n_ctx):
    t = jnp.arange(n_lat, dtype=jnp.int32)
    half = HEAD_DIM // 2
    inv = ROPE_BASE ** (-jnp.arange(0, half, 2, dtype=F32) / half)
    parts_c, parts_s = [], []
    for pos in (t // GRID_W, t % GRID_W):
        ang = pos.astype(F32)[:, None] * inv[None, :]
        parts_c += [jnp.cos(ang), jnp.cos(ang)]
        parts_s += [jnp.sin(ang), jnp.sin(ang)]
    cos_t = jnp.concatenate(parts_c, axis=-1)
    sin_t = jnp.concatenate(parts_s, axis=-1)
    cos_t = jnp.concatenate([cos_t, jnp.ones((n_ctx, HEAD_DIM), F32)], axis=0)
    sin_t = jnp.concatenate([sin_t, jnp.zeros((n_ctx, HEAD_DIM), F32)], axis=0)
    quarter = half // 2
    rot = np.zeros((HEAD_DIM, HEAD_DIM), np.float32)
    for lane in range(HEAD_DIM):
        if lane % half < quarter:
            rot[lane + quarter, lane] = -1.0
        else:
            rot[lane - quarter, lane] = 1.0
    return jnp.concatenate([cos_t, sin_t], axis=1), jnp.asarray(rot, BF16)


def _swa_mask():
    a = np.arange(SWA_QB)
    j = np.arange(SWA_PIECES * SWA_BLOCK)
    in_band = np.abs((j[None, :] - SWA_BLOCK) - a[:, None]) <= SWA_WINDOW
    last = (SWA_PIECES - 1) * SWA_BLOCK
    piece_ok = {0: j >= SWA_BLOCK, 1: j >= 0, 2: j < last, 3: j < 0}
    tabs = [np.where(in_band & piece_ok[kind][None, :], 0.0, NEG_INF) for kind in range(4)]
    return jnp.asarray(np.stack(tabs), F32)


def kernel(x, c, ctx, c_ctx, w_ada, b_ada, g_mix, w_in, gmlp_ln_g, gmlp_ln_b, gmlp_ws, gmlp_bs, na_rpb,
           swa_sink, w_branch, w_out, g_ffn, w_ffn_gate, w_ffn_up, w_ffn_down, w_router, w_exp_gate,
           w_exp_up, w_exp_down, g_final):
    batch, n_lat, d = x.shape
    n_ctx = ctx.shape[1]
    depth = w_ada.shape[0]
    assert batch == 1 and c.shape[0] == 1
    assert n_lat % NA_QB == 0 and n_ctx == NA_QB and n_lat // GRID_W >= NA_KROWS
    m = n_lat + n_ctx

    off_au, off_av = 0, BRANCH_W
    off_bq, off_cq = 2 * BRANCH_W, 3 * BRANCH_W
    off_gate = 4 * BRANCH_W
    off_bk = off_gate + N_BRANCHES * d
    off_bv = off_bk + BRANCH_W
    off_ck = off_bv + BRANCH_W
    off_cv = off_ck + SWA_KV_W
    assert w_in.shape[2] == off_cv + SWA_KV_W and off_au == 0 and off_av == BRANCH_W

    ms_big, ms_small = 8, 16
    gmlp_chunks = 6 if m % (6 * CHUNK) == 0 else 2

    xs = jnp.concatenate([x[0], ctx[0]], axis=0)
    c8 = jnp.concatenate([c, c_ctx[None, :], jnp.zeros((6, d), F32)], axis=0)
    mods = _ada_mods(c8, w_ada, b_ada)[:, :2].reshape(depth, 2, 6, 1, d)

    cs_t, rot = _rope_tables(n_lat, n_ctx)
    swa_mask = _swa_mask()
    na_pairs = _na_bias_pairs(na_rpb)
    bs_full = jnp.repeat(jnp.swapaxes(gmlp_bs, 1, 2), CHUNK, axis=2)
    sink_col = jnp.broadcast_to(swa_sink[:, :, None, None], (depth, SWA_HEADS, SWA_QB, 1))
    w_router_p = jnp.pad(w_router, ((0, 0), (0, 0), (0, ROUTER_LANES - N_EXPERTS)))
    w_router_hi = w_router_p.astype(BF16)
    w_router_lo = (w_router_p - w_router_hi.astype(F32)).astype(BF16)
    w_router_t = jnp.stack([w_router_hi, w_router_lo], axis=1)
    n_moe, n_exp, _, fe = w_exp_gate.shape
    w_exp_down2 = w_exp_down.reshape(n_moe, n_exp * fe, d)

    for l in range(depth):
        i_layer = l // 2
        kvw = 2 * SWA_KV_W
        tail = [w_in] * ((w_in.shape[2] - off_bv) // kvw)
        h, pkv = _norm_first_call(xs, g_mix, mods, l, 0, n_lat, tail, lambda ck: (None, ck, kvw),
                                  lambda n, c: (l, c, off_bv // kvw + n), kvw, "plain", "norm_in_proj_tail")
        px = _mm_call(h, w_in, l, 0, off_bv, ms_big, 1024, "in_proj")
        y_a = _gmlp_call(px, gmlp_ln_g, gmlp_ln_b, gmlp_ws, bs_full, l, gmlp_chunks)
        y_b = _na_call(px, pkv, na_pairs, l, n_lat, off_bq, off_bk, 0)
        y_c = _swa_call(px, pkv, cs_t, swa_mask, sink_col[l], rot, n_lat, n_ctx, off_cq, off_ck - off_bv)
        acc = _merge_call(y_a, y_b, y_c, px, w_branch, l, off_gate, ms_small, 1024)
        xs = _mm_res_call([acc], w_out, l, xs, mods, l, 2, n_lat, ms_small, 1024)
        if l % 2 == 0:
            tn_up = 512
            h, hid0 = _norm_first_call(xs, g_ffn, mods, l, 3, n_lat, [w_ffn_gate, w_ffn_up],
                                       lambda ck: (None, ck, tn_up), lambda n, c: (i_layer, c, 0), tn_up,
                                       "swiglu", "norm_ffn_up")
            hid1 = _ffn_up_call(h, w_ffn_gate, w_ffn_up, i_layer, ms_big, tn_up, 1)
            xs = _mm_res_call([hid0, hid1], w_ffn_down, i_layer, xs, mods, l, 5, n_lat, ms_small, 2048)
        else:
            pair = 2
            h, hid0, gates = _norm_first_call(xs, g_ffn, mods, l, 3, n_lat, [w_exp_gate, w_exp_up],
                                              lambda ck: (None, pair, ck, fe), lambda n, c: (i_layer, 0, c, 0),
                                              pair * fe, "moe", "norm_moe_up", w_router=w_router_t[i_layer], fe=fe)
            hid1 = _moe_up_call(h, w_exp_gate, w_exp_up, gates, i_layer, ms_small, pair, 1)
            xs = _mm_res_call([hid0, hid1], w_exp_down2, i_layer, xs, mods, l, 5, n_lat, ms_big, 1024)
    return _final_norm_call(xs, g_final, n_lat)[None]
```

```python
import functools

import numpy as np
import jax
import jax.numpy as jnp
from jax import lax
from jax.experimental import pallas as pl
from jax.experimental.pallas import tpu as pltpu

F32 = jnp.float32
BF16 = jnp.bfloat16
FP8 = jnp.float8_e4m3fn
FP8_MAX = 448.0

GRID_W = 64
HEAD_DIM = 128
BRANCH_W = 1024
N_BRANCHES = 3
CHUNK = 128
GMLP_GROUPS = 8
NA_HEADS = BRANCH_W // HEAD_DIM
NA_WIN_R = 8
NA_WIN_C = 16
SWA_HEADS = BRANCH_W // HEAD_DIM
SWA_KV_HEADS = 2
SWA_GROUP = SWA_HEADS // SWA_KV_HEADS
SWA_KV_W = SWA_KV_HEADS * HEAD_DIM
SWA_WINDOW = 128
SWA_BLOCK = 128
ROPE_BASE = 10000.0
N_EXPERTS = 8
TOP_K = 2
NEG_INF = -1e30
EPS = 1e-6

NA_QROWS = 4
NA_KROWS = 12
NA_QB = NA_QROWS * GRID_W
NA_KB = NA_KROWS * GRID_W
NA_PIECES = NA_KB // NA_QB

SWA_QB = 2 * SWA_BLOCK
SWA_PIECES = 4

LOG2E = 1.4426950408889634

ROUTER_LANES = 128

MIB = 1024 * 1024
V7X_VMEM_BYTES = 64 * MIB
COMPILER_SCRATCH_BYTES = 6 * MIB


def _params(sem, vmem_bytes):
    limit = vmem_bytes + COMPILER_SCRATCH_BYTES
    assert limit <= V7X_VMEM_BYTES, limit
    return pltpu.CompilerParams(dimension_semantics=sem, vmem_limit_bytes=limit)


def _ada_kernel(c_ref, w_ref, b_ref, o_ref):
    cs = c_ref[...]
    cs = (cs * jax.nn.sigmoid(cs)).astype(BF16)
    w = w_ref[...].astype(BF16)
    o_ref[...] = jnp.dot(cs, w, preferred_element_type=F32) + b_ref[...]


def _ada_mods(c8, w_ada, b_ada, tn=512):
    depth, d, n6 = w_ada.shape
    return pl.pallas_call(
        _ada_kernel,
        grid=(depth, n6 // tn),
        in_specs=[
            pl.BlockSpec((8, d), lambda l, j: (0, 0)),
            pl.BlockSpec((None, d, tn), lambda l, j: (l, 0, j)),
            pl.BlockSpec((None, 1, tn), lambda l, j: (l, 0, j)),
        ],
        out_specs=pl.BlockSpec((None, 8, tn), lambda l, j: (l, 0, j)),
        out_shape=jax.ShapeDtypeStruct((depth, 8, n6), F32),
        compiler_params=_params(("arbitrary", "arbitrary"), 2 * d * tn * 4 + d * tn * 2 + MIB),
        name="ada_mods",
    )(c8, w_ada, b_ada.reshape(depth, 1, n6))


def _norm_mod(x, g_ref, sh_ref, sc_ref):
    ms = jnp.mean(x * x, axis=-1, keepdims=True)
    y = x * lax.rsqrt(ms + EPS) * g_ref[...]
    return y * (1.0 + sc_ref[...]) + sh_ref[...]


def _route_top2(h, wr_ref):
    h_hi = h.astype(BF16)
    h_lo = (h - h_hi.astype(F32)).astype(BF16)
    logits = (jnp.dot(h_hi, wr_ref[0], preferred_element_type=F32)
              + jnp.dot(h_lo, wr_ref[0], preferred_element_type=F32)
              + jnp.dot(h_hi, wr_ref[1], preferred_element_type=F32))
    lane = lax.broadcasted_iota(jnp.int32, logits.shape, 1)
    logits = jnp.where(lane < N_EXPERTS, logits, -jnp.inf)
    m1 = jnp.max(logits, axis=-1, keepdims=True)
    i1 = jnp.min(jnp.where(logits == m1, lane, ROUTER_LANES), axis=-1, keepdims=True)
    first = lane == i1
    rest = jnp.where(first, -jnp.inf, logits)
    m2 = jnp.max(rest, axis=-1, keepdims=True)
    i2 = jnp.min(jnp.where(rest == m2, lane, ROUTER_LANES), axis=-1, keepdims=True)
    second = lane == i2
    e2 = jnp.exp(m2 - m1)
    w1 = 1.0 / (1.0 + e2)
    return jnp.where(first, w1, 0.0) + jnp.where(second, e2 * w1, 0.0)


def _swiglu(hg, hu):
    return hg * jax.nn.sigmoid(hg) * hu


def _expert_weights(gates, first_expert, per_tile, fe):
    lane = lax.broadcasted_iota(jnp.int32, gates.shape, 1)
    out_lane = lax.broadcasted_iota(jnp.int32, (1, per_tile * fe), 1)
    ge = None
    for e in range(per_tile):
        w_e = jnp.sum(jnp.where(lane == first_expert + e, gates, 0.0), axis=-1, keepdims=True)
        ge = w_e if ge is None else jnp.where(out_lane >= e * fe, w_e, ge)
    return ge


def _cast_chunks(w_refs, wb_refs, slot, rows):
    for w_ref, wb_ref in zip(w_refs, wb_refs):
        if len(w_ref.shape) == 3:
            fe = w_ref.shape[2]
            for e in range(w_ref.shape[0]):
                wb_ref[slot, rows, e * fe:(e + 1) * fe] = w_ref[e].astype(BF16)
        else:
            wb_ref[slot, rows, :] = w_ref[...].astype(BF16)


def _norm_first_kernel(*refs, p_steps, ck, n_w, kind, fe):
    x_ref, g_ref, sh_ref, sc_ref = refs[:4]
    w_refs = refs[4:4 + n_w]
    pos = 4 + n_w
    moe = kind == "moe"
    wr_ref = refs[pos] if moe else None
    pos += int(moe)
    h_ref, o_ref = refs[pos:pos + 2]
    pos += 2
    gate_ref = refs[pos] if moe else None
    pos += int(moe)
    quant = kind == "plain8"
    h8_ref, hs_ref = refs[pos:pos + 2] if quant else (None, None)
    wb_refs = refs[pos + 2 * int(quant):]
    kind = "plain" if quant else kind
    s = pl.program_id(0)

    @pl.when(s < p_steps)
    def _():
        _cast_chunks(w_refs, wb_refs, 0, pl.ds(pl.multiple_of(s * ck, ck), ck))

    @pl.when(s >= p_steps)
    def _():
        hf = _norm_mod(x_ref[...], g_ref, sh_ref, sc_ref)
        h = hf.astype(BF16)
        h_ref[...] = h
        if quant:
            amax = jnp.max(jnp.abs(hf), axis=-1, keepdims=True)
            scale = jnp.where(amax > 0.0, amax * (1.0 / FP8_MAX), 1.0)
            h8_ref[...] = (hf / scale).astype(FP8)
            hs_ref[...] = jnp.broadcast_to(scale, hs_ref.shape)
        prods = [jnp.dot(h, wb_ref[0], preferred_element_type=F32) for wb_ref in wb_refs]
        if kind == "plain":
            out = prods[0] if len(prods) == 1 else jnp.concatenate(prods, axis=1)
        else:
            out = _swiglu(prods[0], prods[1])
        if moe:
            gates = _route_top2(hf, wr_ref)
            gate_ref[...] = gates
            out = out * _expert_weights(gates, 0, o_ref.shape[1] // fe, fe)
        o_ref[...] = out.astype(o_ref.dtype)


def _final_norm_kernel(x_ref, g_ref, o_ref):
    x = x_ref[...]
    ms = jnp.mean(x * x, axis=-1, keepdims=True)
    o_ref[...] = x * lax.rsqrt(ms + EPS) * g_ref[...]


NORM_ROWS = 256
NORM_STAGE_STEPS = 8


def _norm_first_call(x, g, mods, l, which, n_lat, weights, w_block, w_index, tn, kind, name, w_router=None, fe=0):
    out_cols = tn * len(weights) if kind in ("plain", "plain8") else tn
    m, d = x.shape
    tr, p_steps = NORM_ROWS, NORM_STAGE_STEPS
    ck = d // p_steps
    assert m % tr == 0 and n_lat % tr == 0 and d % p_steps == 0
    lat_tiles = n_lat // tr
    tile = lambda s: jnp.maximum(s - p_steps, 0)
    chunk = lambda s: jnp.minimum(s, p_steps - 1)
    row_kind = lambda s: jnp.where(tile(s) >= lat_tiles, 1, 0)
    mod_spec = lambda k: pl.BlockSpec((None, None, None, 1, d), lambda s: (l, row_kind(s), k, 0, 0))
    in_specs = [pl.BlockSpec((tr, d), lambda s: (tile(s), 0)),
                pl.BlockSpec((None, 1, d), lambda s: (l, 0, 0)),
                mod_spec(which), mod_spec(which + 1)]
    in_specs += [pl.BlockSpec(w_block(ck), functools.partial(lambda n, s: w_index(n, chunk(s)), n))
                 for n in range(len(weights))]
    args = [x, g.reshape(g.shape[0], 1, d), mods, mods, *weights]
    out_specs = [pl.BlockSpec((tr, d), lambda s: (tile(s), 0)),
                 pl.BlockSpec((tr, out_cols), lambda s: (tile(s), 0))]
    out_shape = [jax.ShapeDtypeStruct((m, d), BF16), jax.ShapeDtypeStruct((m, out_cols), BF16)]
    vmem = (2 * (tr * d * 4 + tr * d * 2 + tr * out_cols * 2 + len(weights) * ck * tn * 4)
            + len(weights) * d * tn * 2 + 3 * tr * d * 4 + 3 * tr * out_cols * 4)
    if kind == "moe":
        in_specs.append(pl.BlockSpec((2, d, ROUTER_LANES), lambda s: (0, 0, 0)))
        args.append(w_router)
        out_specs.append(pl.BlockSpec((tr, ROUTER_LANES), lambda s: (tile(s), 0)))
        out_shape.append(jax.ShapeDtypeStruct((m, ROUTER_LANES), F32))
        vmem += 2 * 2 * d * ROUTER_LANES * 2 + 2 * tr * d * 4
    if kind == "plain8":
        out_specs += [pl.BlockSpec((tr, d), lambda s: (tile(s), 0)),
                      pl.BlockSpec((tr, ROUTER_LANES), lambda s: (tile(s), 0))]
        out_shape += [jax.ShapeDtypeStruct((m, d), FP8), jax.ShapeDtypeStruct((m, ROUTER_LANES), F32)]
        vmem += 2 * (tr * d + tr * ROUTER_LANES * 4)
    return pl.pallas_call(
        functools.partial(_norm_first_kernel, p_steps=p_steps, ck=ck, n_w=len(weights), kind=kind, fe=fe),
        grid=(p_steps + m // tr,),
        in_specs=in_specs,
        out_specs=out_specs,
        out_shape=out_shape,
        scratch_shapes=[pltpu.VMEM((1, d, tn), BF16) for _ in weights],
        compiler_params=_params(("arbitrary",), vmem),
        name=name,
    )(*args)


def _final_norm_call(x, g, n_lat, tr=256):
    d = x.shape[1]
    return pl.pallas_call(
        _final_norm_kernel,
        grid=(n_lat // tr,),
        in_specs=[pl.BlockSpec((tr, d), lambda i: (i, 0)),
                  pl.BlockSpec((1, d), lambda i: (0, 0))],
        out_specs=pl.BlockSpec((tr, d), lambda i: (i, 0)),
        out_shape=jax.ShapeDtypeStruct((n_lat, d), F32),
        compiler_params=_params(("arbitrary",), 6 * tr * d * 4),
        name="final_norm",
    )(x, g.reshape(1, d))


class _Tiling:
    def __init__(self, m, k, n, ms, tn):
        assert m % ms == 0 and k % ms == 0 and n % tn == 0
        self.ms, self.tn, self.nt = ms, tn, n // tn
        self.tm, self.ck = m // ms, k // ms
        assert self.tm % 16 == 0 and self.ck % 16 == 0
        self.grid = (self.nt + 1, ms)

    def row(self, j, i):
        return jnp.where(j == 0, 0, i)

    def col(self, j):
        return jnp.maximum(j - 1, 0)

    def wrow(self, j, i):
        return jnp.where(j == self.nt, self.ms - 1, i)

    def wcol(self, j):
        return jnp.minimum(j, self.nt - 1)


def _stage_weight(t, w_refs, wb_refs):
    j, i = pl.program_id(0), pl.program_id(1)

    @pl.when(j < t.nt)
    def _():
        _cast_chunks(w_refs, wb_refs, j % 2, pl.ds(pl.multiple_of(i * t.ck, t.ck), t.ck))


def _mm_kernel(a_ref, w_ref, o_ref, wb_ref, *, t):
    j = pl.program_id(0)
    _stage_weight(t, (w_ref,), (wb_ref,))

    @pl.when(j > 0)
    def _():
        o_ref[...] = jnp.dot(a_ref[...], wb_ref[(j + 1) % 2], preferred_element_type=F32).astype(o_ref.dtype)


def _mm_call(a, w, l, w_tile, n_tiles, ms, tn, name):
    m, k = a.shape
    t = _Tiling(m, k, n_tiles * tn, ms, tn)
    vmem = 2 * (t.tm * k * 2 + t.ck * tn * 4 + t.tm * tn * 2) + 2 * k * tn * 2 + t.tm * tn * 4
    return pl.pallas_call(
        functools.partial(_mm_kernel, t=t),
        grid=t.grid,
        in_specs=[pl.BlockSpec((t.tm, k), lambda j, i: (t.row(j, i), 0)),
                  pl.BlockSpec((None, t.ck, tn), lambda j, i: (l, t.wrow(j, i), w_tile(t.wcol(j))))],
        out_specs=pl.BlockSpec((t.tm, tn), lambda j, i: (t.row(j, i), t.col(j))),
        out_shape=jax.ShapeDtypeStruct((m, n_tiles * tn), BF16),
        scratch_shapes=[pltpu.VMEM((2, k, tn), BF16)],
        compiler_params=_params(("arbitrary", "arbitrary"), vmem),
        name=name,
    )(a, w)


def _col_amax_kernel(w_ref, o_ref):
    o_ref[...] = jnp.max(jnp.abs(w_ref[...]), axis=0, keepdims=True)


def _col_amax_call(w, col0, ncols, tn=512):
    depth, k, _ = w.shape
    assert col0 % tn == 0 and ncols % tn == 0
    return pl.pallas_call(
        _col_amax_kernel,
        grid=(depth, ncols // tn),
        in_specs=[pl.BlockSpec((None, k, tn), lambda l, j: (l, 0, col0 // tn + j))],
        out_specs=pl.BlockSpec((None, 1, tn), lambda l, j: (l, 0, j)),
        out_shape=jax.ShapeDtypeStruct((depth, 1, ncols), F32),
        compiler_params=_params(("arbitrary", "arbitrary"), 3 * k * tn * 4),
        name="col_amax",
    )(w)


def _mm8_kernel(a_ref, hs_ref, w_ref, isw_ref, sw_ref, o_ref, wb_ref, *, t):
    j, i = pl.program_id(0), pl.program_id(1)

    @pl.when(j < t.nt)
    def _():
        rows = pl.ds(pl.multiple_of(i * t.ck, t.ck), t.ck)
        wb_ref[j % 2, rows, :] = (w_ref[...] * isw_ref[...]).astype(FP8)

    @pl.when(j > 0)
    def _():
        y = jnp.dot(a_ref[...], wb_ref[(j + 1) % 2], preferred_element_type=F32)
        o_ref[...] = (y * hs_ref[:, :1] * sw_ref[...]).astype(o_ref.dtype)


def _mm8_call(a8, a_scale, w, w_scale, l, col0, ncols, ms, tn, name):
    m, k = a8.shape
    t = _Tiling(m, k, ncols, ms, tn)
    assert col0 % tn == 0 and t.tm % 32 == 0 and t.ck % 32 == 0
    inv_scale = 1.0 / w_scale
    lanes = a_scale.shape[1]
    vmem = (2 * (t.tm * k + t.ck * tn * 4 + t.tm * tn * 2 + t.tm * lanes * 4) + 2 * k * tn
            + 2 * t.tm * tn * 4)
    scale_spec = lambda tile: pl.BlockSpec((None, 1, tn), lambda j, i: (l, 0, tile(j)))
    return pl.pallas_call(
        functools.partial(_mm8_kernel, t=t),
        grid=t.grid,
        in_specs=[pl.BlockSpec((t.tm, k), lambda j, i: (t.row(j, i), 0)),
                  pl.BlockSpec((t.tm, lanes), lambda j, i: (t.row(j, i), 0)),
                  pl.BlockSpec((None, t.ck, tn), lambda j, i: (l, t.wrow(j, i), col0 // tn + t.wcol(j))),
                  scale_spec(t.wcol), scale_spec(t.col)],
        out_specs=pl.BlockSpec((t.tm, tn), lambda j, i: (t.row(j, i), t.col(j))),
        out_shape=jax.ShapeDtypeStruct((m, ncols), BF16),
        scratch_shapes=[pltpu.VMEM((2, k, tn), FP8)],
        compiler_params=_params(("arbitrary", "arbitrary"), vmem),
        name=name,
    )(a8, a_scale, w, inv_scale, w_scale)


def _mm_res_kernel(*refs, t, n_lat, ks):
    a_refs = refs[:len(ks)]
    w_ref, x_ref, gx_ref, gc_ref, o_ref, wb_ref = refs[len(ks):]
    j, i = pl.program_id(0), pl.program_id(1)
    _stage_weight(t, (w_ref,), (wb_ref,))

    @pl.when(j > 0)
    def _():
        slot = (j + 1) % 2
        y, k0 = None, 0
        for a_ref, k in zip(a_refs, ks):
            part = jnp.dot(a_ref[...], wb_ref[slot, k0:k0 + k, :], preferred_element_type=F32)
            y = part if y is None else y + part
            k0 += k
        row = i * t.tm + lax.broadcasted_iota(jnp.int32, (t.tm, 1), 0)
        gate = jnp.where(row < n_lat, gx_ref[...], gc_ref[...])
        o_ref[...] = x_ref[...] + gate * y


def _mm_res_call(a_parts, w, l, x, mods, lm, which, n_lat, ms, tn):
    m = a_parts[0].shape[0]
    ks = tuple(a.shape[1] for a in a_parts)
    k = sum(ks)
    d = w.shape[2]
    assert w.shape[1] == k
    t = _Tiling(m, k, d, ms, tn)
    gate_spec = lambda kind: pl.BlockSpec((None, None, None, 1, tn),
                                          lambda j, i: (lm, kind, which, 0, t.col(j)))
    xo_spec = pl.BlockSpec((t.tm, tn), lambda j, i: (t.row(j, i), t.col(j)))
    vmem = 2 * (t.tm * k * 2 + t.ck * tn * 4 + 2 * t.tm * tn * 4) + 2 * k * tn * 2 + 2 * t.tm * tn * 4
    return pl.pallas_call(
        functools.partial(_mm_res_kernel, t=t, n_lat=n_lat, ks=ks),
        grid=t.grid,
        in_specs=[pl.BlockSpec((t.tm, kp), lambda j, i: (t.row(j, i), 0)) for kp in ks]
                 + [pl.BlockSpec((None, t.ck, tn), lambda j, i: (l, t.wrow(j, i), t.wcol(j))),
                    xo_spec, gate_spec(0), gate_spec(1)],
        out_specs=xo_spec,
        out_shape=jax.ShapeDtypeStruct((m, d), F32),
        scratch_shapes=[pltpu.VMEM((2, k, tn), BF16)],
        compiler_params=_params(("arbitrary", "arbitrary"), vmem),
        name="proj_residual",
    )(*a_parts, w, x, mods, mods)


def _merge_kernel(ya_ref, yb_ref, yc_ref, ga_ref, gb_ref, gc_ref, w_ref, o_ref, wb_ref, *, t, bw):
    j = pl.program_id(0)
    _stage_weight(t, (w_ref,), (wb_ref,))

    @pl.when(j > 0)
    def _():
        slot = (j + 1) % 2
        acc = None
        for br, (y_ref, g_ref) in enumerate(((ya_ref, ga_ref), (yb_ref, gb_ref), (yc_ref, gc_ref))):
            part = jnp.dot(y_ref[...], wb_ref[slot, br * bw:(br + 1) * bw, :], preferred_element_type=F32)
            part = jax.nn.sigmoid(g_ref[...].astype(F32)) * part
            acc = part if acc is None else acc + part
        o_ref[...] = acc.astype(o_ref.dtype)


def _merge_call(ya, yb, yc, px, w_branch, l, off_gate, ms, tn):
    m, bw = ya.shape
    d = w_branch.shape[3]
    k = N_BRANCHES * bw
    t = _Tiling(m, k, d, ms, tn)
    y_spec = pl.BlockSpec((t.tm, bw), lambda j, i: (t.row(j, i), 0))
    g_spec = lambda br: pl.BlockSpec((t.tm, tn), lambda j, i: (t.row(j, i), (off_gate + br * d) // tn + t.col(j)))
    vmem = (2 * (3 * t.tm * bw * 2 + 3 * t.tm * tn * 2 + t.ck * tn * 4 + t.tm * tn * 2) + 2 * k * tn * 2
            + 2 * t.tm * tn * 4)
    return pl.pallas_call(
        functools.partial(_merge_kernel, t=t, bw=bw),
        grid=t.grid,
        in_specs=[y_spec, y_spec, y_spec, g_spec(0), g_spec(1), g_spec(2),
                  pl.BlockSpec((None, t.ck, tn), lambda j, i: (l, t.wrow(j, i), t.wcol(j)))],
        out_specs=pl.BlockSpec((t.tm, tn), lambda j, i: (t.row(j, i), t.col(j))),
        out_shape=jax.ShapeDtypeStruct((m, d), BF16),
        scratch_shapes=[pltpu.VMEM((2, k, tn), BF16)],
        compiler_params=_params(("arbitrary", "arbitrary"), vmem),
        name="branch_merge",
    )(ya, yb, yc, px, px, px, w_branch.reshape(w_branch.shape[0], k, d))


def _ffn_up_kernel(a_ref, wg_ref, wu_ref, o_ref, wgb_ref, wub_ref, *, t):
    j = pl.program_id(0)
    _stage_weight(t, (wg_ref, wu_ref), (wgb_ref, wub_ref))

    @pl.when(j > 0)
    def _():
        slot = (j + 1) % 2
        a = a_ref[...]
        hg = jnp.dot(a, wgb_ref[slot], preferred_element_type=F32)
        hu = jnp.dot(a, wub_ref[slot], preferred_element_type=F32)
        o_ref[...] = _swiglu(hg, hu).astype(o_ref.dtype)


def _moe_up_kernel(a_ref, wg_ref, wu_ref, gate_ref, o_ref, wgb_ref, wub_ref, *, t, fe, tile0):
    j = pl.program_id(0)
    _stage_weight(t, (wg_ref, wu_ref), (wgb_ref, wub_ref))

    @pl.when(j > 0)
    def _():
        slot = (j + 1) % 2
        a = a_ref[...]
        hg = jnp.dot(a, wgb_ref[slot], preferred_element_type=F32)
        hu = jnp.dot(a, wub_ref[slot], preferred_element_type=F32)
        per_tile = t.tn // fe
        ge = _expert_weights(gate_ref[...], (tile0 + j - 1) * per_tile, per_tile, fe)
        o_ref[...] = (_swiglu(hg, hu) * ge).astype(o_ref.dtype)


def _ffn_up_call(a, wg, wu, i_layer, ms, tn, tile0):
    m, k = a.shape
    f = wg.shape[2] - tile0 * tn
    t = _Tiling(m, k, f, ms, tn)
    w_spec = pl.BlockSpec((None, t.ck, tn), lambda j, i: (i_layer, t.wrow(j, i), tile0 + t.wcol(j)))
    vmem = 2 * (t.tm * k * 2 + 2 * t.ck * tn * 4 + t.tm * tn * 2) + 4 * k * tn * 2 + 3 * t.tm * tn * 4
    return pl.pallas_call(
        functools.partial(_ffn_up_kernel, t=t),
        grid=t.grid,
        in_specs=[pl.BlockSpec((t.tm, k), lambda j, i: (t.row(j, i), 0)), w_spec, w_spec],
        out_specs=pl.BlockSpec((t.tm, tn), lambda j, i: (t.row(j, i), t.col(j))),
        out_shape=jax.ShapeDtypeStruct((m, f), BF16),
        scratch_shapes=[pltpu.VMEM((2, k, tn), BF16), pltpu.VMEM((2, k, tn), BF16)],
        compiler_params=_params(("arbitrary", "arbitrary"), vmem),
        name="ffn_up",
    )(a, wg, wu)


def _moe_up_call(a, wg, wu, gates, i_layer, ms, per_tile, tile0):
    m, k = a.shape
    n_exp, fe = wg.shape[1], wg.shape[3]
    tn = per_tile * fe
    t = _Tiling(m, k, n_exp * fe - tile0 * tn, ms, tn)
    w_spec = pl.BlockSpec((None, per_tile, t.ck, fe),
                          lambda j, i: (i_layer, tile0 + t.wcol(j), t.wrow(j, i), 0))
    vmem = (2 * (t.tm * k * 2 + 2 * t.ck * tn * 4 + t.tm * tn * 2 + t.tm * ROUTER_LANES * 4) + 4 * k * tn * 2
            + 3 * t.tm * tn * 4)
    return pl.pallas_call(
        functools.partial(_moe_up_kernel, t=t, fe=fe, tile0=tile0),
        grid=t.grid,
        in_specs=[pl.BlockSpec((t.tm, k), lambda j, i: (t.row(j, i), 0)), w_spec, w_spec,
                  pl.BlockSpec((t.tm, ROUTER_LANES), lambda j, i: (t.row(j, i), 0))],
        out_specs=pl.BlockSpec((t.tm, tn), lambda j, i: (t.row(j, i), t.col(j))),
        out_shape=jax.ShapeDtypeStruct((m, t.nt * tn), BF16),
        scratch_shapes=[pltpu.VMEM((2, k, tn), BF16), pltpu.VMEM((2, k, tn), BF16)],
        compiler_params=_params(("arbitrary", "arbitrary"), vmem),
        name="moe_up",
    )(a, wg, wu, gates)


def _gmlp_kernel(u_ref, v_ref, lng_ref, lnb_ref, ws_ref, bs_ref, o_ref, *, chunks):
    u = jax.nn.gelu(u_ref[...].astype(F32))
    v = jax.nn.gelu(v_ref[...].astype(F32))
    mu = jnp.mean(v, axis=-1, keepdims=True)
    var = jnp.mean(jnp.square(v - mu), axis=-1, keepdims=True)
    vn = ((v - mu) * lax.rsqrt(var + EPS) * lng_ref[...] + lnb_ref[...]).astype(BF16)
    for g in range(GMLP_GROUPS):
        cols = slice(g * CHUNK, (g + 1) * CHUNK)
        wsg = ws_ref[g].astype(BF16)
        for c in range(chunks):
            rows = slice(c * CHUNK, (c + 1) * CHUNK)
            s = jnp.dot(wsg, vn[rows, cols], preferred_element_type=F32) + bs_ref[:, cols]
            o_ref[rows, cols] = (u[rows, cols] * s).astype(o_ref.dtype)


def _gmlp_call(px, lng, lnb, ws, bs_full, l, chunks):
    m = px.shape[0]
    t = chunks * CHUNK
    return pl.pallas_call(
        functools.partial(_gmlp_kernel, chunks=chunks),
        grid=(m // t,),
        in_specs=[pl.BlockSpec((t, BRANCH_W), lambda i: (i, 0)),
                  pl.BlockSpec((t, BRANCH_W), lambda i: (i, 1)),
                  pl.BlockSpec((None, 1, BRANCH_W), lambda i: (l, 0, 0)),
                  pl.BlockSpec((None, 1, BRANCH_W), lambda i: (l, 0, 0)),
                  pl.BlockSpec((None, GMLP_GROUPS, CHUNK, CHUNK), lambda i: (l, 0, 0, 0)),
                  pl.BlockSpec((None, CHUNK, BRANCH_W), lambda i: (l, 0, 0))],
        out_specs=pl.BlockSpec((t, BRANCH_W), lambda i: (i, 0)),
        out_shape=jax.ShapeDtypeStruct((m, BRANCH_W), BF16),
        compiler_params=_params(("arbitrary",), 6 * t * BRANCH_W * 2 + 6 * t * BRANCH_W * 4 + 2 * MIB),
        name="gmlp",
    )(px, px, lng.reshape(-1, 1, BRANCH_W), lnb.reshape(-1, 1, BRANCH_W), ws, bs_full)


def _dot_nt(a, b):
    return lax.dot_general(a, b, (((1,), (1,)), ((), ())), preferred_element_type=F32)


def _tree(op, xs):
    while len(xs) > 1:
        xs = [op(xs[i], xs[i + 1]) if i + 1 < len(xs) else xs[i] for i in range(0, len(xs), 2)]
    return xs[0]


def _with_ones(v):
    return jnp.concatenate([v, jnp.ones_like(v)], axis=1)


def _softmax_pv(s, v1_tiles, extra_logit=None):
    width = min(t.shape[1] for t in s)
    parts = [t[:, i:i + width] for t in s for i in range(0, t.shape[1], width)]
    m = jnp.max(_tree(jnp.maximum, parts), axis=-1, keepdims=True)
    if extra_logit is not None:
        m = jnp.maximum(m, extra_logit)
    o = _tree(jnp.add, [jnp.dot(jnp.exp2(t - m).astype(BF16), v1, preferred_element_type=F32)
                        for t, v1 in zip(s, v1_tiles)])
    hd = o.shape[1] // 2
    denom = o[:, hd:]
    if extra_logit is not None:
        denom = denom + jnp.exp2(extra_logit - m)
    return o[:, :hd] / denom


def _na_kernel(q_ref, k0_ref, k1_ref, k2_ref, v0_ref, v1_ref, v2_ref, kc_ref, vc_ref, pair_ref, mask_ref,
               o_ref, bias_ref, *, nb, roff):
    b = pl.program_id(0)
    left = lax.broadcasted_iota(jnp.int32, (GRID_W, 2 * GRID_W), 1) < GRID_W

    def build(kind):
        for h in range(NA_HEADS):
            for qr in range(NA_QROWS):
                rows = slice(qr * GRID_W, (qr + 1) * GRID_W)
                for p in range(NA_KROWS // 2):
                    cols = slice(p * 2 * GRID_W, (p + 1) * 2 * GRID_W)
                    pair = jnp.where(left, pair_ref[h, roff[kind][qr][2 * p]], pair_ref[h, roff[kind][qr][2 * p + 1]])
                    bias_ref[h, rows, cols] = pair * LOG2E + mask_ref[rows, cols]

    for kind, first_block in enumerate((0, 1, nb - 1)):
        pl.when(b == first_block)(functools.partial(build, kind))

    @pl.when(b == nb)
    def _():
        for h in range(NA_HEADS):
            bias_ref[h] = mask_ref[...]

    k_refs = (k0_ref, k1_ref, k2_ref)
    v_refs = (v0_ref, v1_ref, v2_ref)
    for h in range(NA_HEADS):
        cols = slice(h * HEAD_DIM, (h + 1) * HEAD_DIM)
        q = (q_ref[:, cols].astype(F32) * (HEAD_DIM ** -0.5 * LOG2E)).astype(BF16)
        s = [_dot_nt(q, k_refs[j][:, cols]) + bias_ref[h, :, j * NA_QB:(j + 1) * NA_QB] for j in range(NA_PIECES)]
        s.append(_dot_nt(q, kc_ref[:, cols]))
        v1 = [_with_ones(v_refs[j][:, cols]) for j in range(NA_PIECES)] + [_with_ones(vc_ref[:, cols])]
        o_ref[:, cols] = _softmax_pv(s, v1).astype(o_ref.dtype)


def _na_call(px, pv, pairs, l, n_lat, off_q, off_k, off_v):
    m = px.shape[0]
    nb = n_lat // NA_QB
    assert nb >= 3
    w = BRANCH_W
    roff, mask = _na_geometry(n_lat // GRID_W)
    kstart = lambda b: jnp.clip(b - 1, 0, nb - NA_PIECES)
    k_spec = lambda off, j: pl.BlockSpec((NA_QB, w), lambda b: (kstart(b) + j, off // w))
    kind = lambda b: jnp.where(b == 0, 0, jnp.where(b == nb - 1, 2, jnp.where(b == nb, 3, 1)))
    n_rel = 2 * NA_WIN_R - 1
    vmem = (2 * (9 * NA_QB * w * 2 + NA_HEADS * n_rel * GRID_W * 2 * GRID_W * 4 + NA_QB * NA_KB * 4 + NA_QB * w * 2)
            + NA_HEADS * NA_QB * NA_KB * 4 + 16 * NA_QB * NA_QB * 4)
    return pl.pallas_call(
        functools.partial(_na_kernel, nb=nb, roff=roff),
        grid=(m // NA_QB,),
        in_specs=[pl.BlockSpec((NA_QB, w), lambda b: (b, off_q // w)),
                  k_spec(off_k, 0), k_spec(off_k, 1), k_spec(off_k, 2),
                  k_spec(off_v, 0), k_spec(off_v, 1), k_spec(off_v, 2),
                  pl.BlockSpec((NA_QB, w), lambda b: (nb, off_k // w)),
                  pl.BlockSpec((NA_QB, w), lambda b: (nb, off_v // w)),
                  pl.BlockSpec((None, NA_HEADS, n_rel, GRID_W, 2 * GRID_W), lambda b: (l, 0, 0, 0, 0)),
                  pl.BlockSpec((None, NA_QB, NA_KB), lambda b: (kind(b), 0, 0))],
        out_specs=pl.BlockSpec((NA_QB, w), lambda b: (b, 0)),
        out_shape=jax.ShapeDtypeStruct((m, w), BF16),
        scratch_shapes=[pltpu.VMEM((NA_HEADS, NA_QB, NA_KB), F32)],
        compiler_params=_params(("arbitrary",), vmem),
        name="neighbourhood_attention",
    )(px, px, px, px, pv, pv, pv, px, pv, pairs, mask)


def _na_geometry(rows):
    col = np.arange(GRID_W)
    c0 = np.clip(col - NA_WIN_C // 2, 0, GRID_W - NA_WIN_C)
    in_win = (col[None, :] >= c0[:, None]) & (col[None, :] < c0[:, None] + NA_WIN_C)
    roffs, masks = [], []
    for r0, s0 in ((0, 0), (NA_QROWS, 0), (rows - NA_QROWS, rows - NA_KROWS)):
        r = r0 + np.arange(NA_QROWS)
        start = np.clip(r - NA_WIN_R // 2, 0, rows - NA_WIN_R)
        key_row = s0 + np.arange(NA_KROWS)
        valid_r = (key_row[None, :] >= start[:, None]) & (key_row[None, :] < start[:, None] + NA_WIN_R)
        roff = np.clip(key_row[None, :] - r[:, None] + (NA_WIN_R - 1), 0, 2 * NA_WIN_R - 2)
        valid = valid_r[:, None, :, None] & in_win[None, :, None, :]
        roffs.append(tuple(tuple(int(v) for v in row) for row in roff))
        masks.append(np.where(valid, 0.0, NEG_INF).reshape(NA_QB, NA_KB))
    masks.append(np.full((NA_QB, NA_KB), NEG_INF))
    return tuple(roffs), jnp.asarray(np.stack(masks), F32)


def _na_bias_pairs(rpb):
    qcol = np.arange(GRID_W)
    kcol = np.arange(2 * GRID_W) % GRID_W
    coff = np.clip(kcol[None, :] - qcol[:, None] + (NA_WIN_C - 1), 0, 2 * NA_WIN_C - 2)
    pick = (coff[:, :, None] == np.arange(2 * NA_WIN_C - 1)).astype(np.float32)
    return jnp.einsum('lhij,cdj->lhicd', rpb.astype(F32), pick, precision=lax.Precision.HIGHEST)


def _swa_kernel(q_ref, csq_ref, kv0_ref, kv1_ref, kv2_ref, kv3_ref, cs0_ref, cs1_ref, cs2_ref, cs3_ref,
                kvc_ref, mask_ref, sink_ref, rot_ref, o_ref):
    rot = rot_ref[...]
    hd = HEAD_DIM

    def rope(x, cs):
        swapped = jnp.dot(x, rot, preferred_element_type=F32)
        return x.astype(F32) * cs[:, :hd] + swapped * cs[:, hd:]

    csq = csq_ref[...]
    kv_refs = (kv0_ref, kv1_ref, kv2_ref, kv3_ref)
    cs_refs = (cs0_ref, cs1_ref, cs2_ref, cs3_ref)
    for kv in range(SWA_KV_HEADS):
        kcols = slice(kv * hd, (kv + 1) * hd)
        vcols = slice(SWA_KV_W + kv * hd, SWA_KV_W + (kv + 1) * hd)
        k_tiles = [rope(kv_refs[j][:, kcols], cs_refs[j][...]).astype(BF16) for j in range(SWA_PIECES)]
        k_tiles.append(kvc_ref[:, kcols])
        v1 = [_with_ones(kv_refs[j][:, vcols]) for j in range(SWA_PIECES)] + [_with_ones(kvc_ref[:, vcols])]
        for h in range(kv * SWA_GROUP, (kv + 1) * SWA_GROUP):
            cols = slice(h * hd, (h + 1) * hd)
            q = (rope(q_ref[:, cols], csq) * (hd ** -0.5 * LOG2E)).astype(BF16)
            s = [_dot_nt(q, k_tiles[j]) + mask_ref[:, j * SWA_BLOCK:(j + 1) * SWA_BLOCK] for j in range(SWA_PIECES)]
            s.append(_dot_nt(q, k_tiles[SWA_PIECES]))
            o_ref[:, cols] = _softmax_pv(s, v1, extra_logit=sink_ref[h] * LOG2E).astype(o_ref.dtype)


def _swa_call(px, pkv, cs_t, mask, sink_col, rot, n_lat, n_ctx, off_q, off_kv):
    kvb = off_kv // (2 * SWA_KV_W)
    assert off_kv % (2 * SWA_KV_W) == 0
    m = px.shape[0]
    blk, w = SWA_BLOCK, BRANCH_W
    nbl = n_lat // blk
    nbq = n_lat // SWA_QB
    assert n_ctx == SWA_QB and nbq >= 2
    piece = lambda b, j: jnp.clip(2 * b - 1 + j, 0, nbl - 1)
    kind = lambda b: jnp.where(b == 0, 0, jnp.where(b == nbq - 1, 2, jnp.where(b == nbq, 3, 1)))
    kv_spec = lambda j: pl.BlockSpec((blk, 2 * SWA_KV_W), lambda b: (piece(b, j), kvb))
    cs_spec = lambda j: pl.BlockSpec((blk, 2 * HEAD_DIM), lambda b: (piece(b, j), 0))
    vmem = (2 * (2 * SWA_QB * w * 2 + SWA_QB * SWA_PIECES * blk * 4 + SWA_HEADS * SWA_QB * HEAD_DIM * 4)
            + 16 * SWA_QB * (SWA_PIECES * blk + n_ctx) * 4)
    return pl.pallas_call(
        _swa_kernel,
        grid=(m // SWA_QB,),
        in_specs=[pl.BlockSpec((SWA_QB, w), lambda b: (b, off_q // w)),
                  pl.BlockSpec((SWA_QB, 2 * HEAD_DIM), lambda b: (b, 0)),
                  kv_spec(0), kv_spec(1), kv_spec(2), kv_spec(3),
                  cs_spec(0), cs_spec(1), cs_spec(2), cs_spec(3),
                  pl.BlockSpec((n_ctx, 2 * SWA_KV_W), lambda b: (n_lat // n_ctx, kvb)),
                  pl.BlockSpec((None, SWA_QB, SWA_PIECES * blk), lambda b: (kind(b), 0, 0)),
                  pl.BlockSpec((SWA_HEADS, SWA_QB, 1), lambda b: (0, 0, 0)),
                  pl.BlockSpec((HEAD_DIM, HEAD_DIM), lambda b: (0, 0))],
        out_specs=pl.BlockSpec((SWA_QB, w), lambda b: (b, 0)),
        out_shape=jax.ShapeDtypeStruct((m, w), BF16),
        compiler_params=_params(("arbitrary",), vmem),
        name="window_gqa",
    )(px, cs_t, pkv, pkv, pkv, pkv, cs_t, cs_t, cs_t, cs_t, pkv, mask, sink_col, rot)


def _rope_tables(n_lat, n_ctx):
    t = jnp.arange(n_lat, dtype=jnp.int32)
    half = HEAD_DIM // 2
    inv = ROPE_BASE ** (-jnp.arange(0, half, 2, dtype=F32) / half)
    parts_c, parts_s = [], []
    for pos in (t // GRID_W, t % GRID_W):
        ang = pos.astype(F32)[:, None] * inv[None, :]
        parts_c += [jnp.cos(ang), jnp.cos(ang)]
        parts_s += [jnp.sin(ang), jnp.sin(ang)]
    cos_t = jnp.concatenate(parts_c, axis=-1)
    sin_t = jnp.concatenate(parts_s, axis=-1)
    cos_t = jnp.concatenate([cos_t, jnp.ones((n_ctx, HEAD_DIM), F32)], axis=0)
    sin_t = jnp.concatenate([sin_t, jnp.zeros((n_ctx, HEAD_DIM), F32)], axis=0)
    quarter = half // 2
    rot = np.zeros((HEAD_DIM, HEAD_DIM), np.float32)
    for lane in range(HEAD_DIM):
        if lane % half < quarter:
            rot[lane + quarter, lane] = -1.0
        else:
            rot[lane - quarter, lane] = 1.0
    return jnp.concatenate([cos_t, sin_t], axis=1), jnp.asarray(rot, BF16)


def _swa_mask():
    a = np.arange(SWA_QB)
    j = np.arange(SWA_PIECES * SWA_BLOCK)
    in_band = np.abs((j[None, :] - SWA_BLOCK) - a[:, None]) <= SWA_WINDOW
    last = (SWA_PIECES - 1) * SWA_BLOCK
    piece_ok = {0: j >= SWA_BLOCK, 1: j >= 0, 2: j < last, 3: j < 0}
    tabs = [np.where(in_band & piece_ok[kind][None, :], 0.0, NEG_INF) for kind in range(4)]
    return jnp.asarray(np.stack(tabs), F32)


def kernel(x, c, ctx, c_ctx, w_ada, b_ada, g_mix, w_in, gmlp_ln_g, gmlp_ln_b, gmlp_ws, gmlp_bs, na_rpb,
           swa_sink, w_branch, w_out, g_ffn, w_ffn_gate, w_ffn_up, w_ffn_down, w_router, w_exp_gate,
           w_exp_up, w_exp_down, g_final):
    batch, n_lat, d = x.shape
    n_ctx = ctx.shape[1]
    depth = w_ada.shape[0]
    assert batch == 1 and c.shape[0] == 1
    assert n_lat % NA_QB == 0 and n_ctx == NA_QB and n_lat // GRID_W >= NA_KROWS
    m = n_lat + n_ctx

    off_au, off_av = 0, BRANCH_W
    off_bq, off_cq = 2 * BRANCH_W, 3 * BRANCH_W
    off_gate = 4 * BRANCH_W
    off_bk = off_gate + N_BRANCHES * d
    off_bv = off_bk + BRANCH_W
    off_ck = off_bv + BRANCH_W
    off_cv = off_ck + SWA_KV_W
    assert w_in.shape[2] == off_cv + SWA_KV_W and off_au == 0 and off_av == BRANCH_W

    ms_big, ms_small = 8, 16
    gmlp_chunks = 6 if m % (6 * CHUNK) == 0 else 2

    xs = jnp.concatenate([x[0], ctx[0]], axis=0)
    c8 = jnp.concatenate([c, c_ctx[None, :], jnp.zeros((6, d), F32)], axis=0)
    mods = _ada_mods(c8, w_ada, b_ada)[:, :2].reshape(depth, 2, 6, 1, d)

    cs_t, rot = _rope_tables(n_lat, n_ctx)
    swa_mask = _swa_mask()
    na_pairs = _na_bias_pairs(na_rpb)
    bs_full = jnp.repeat(jnp.swapaxes(gmlp_bs, 1, 2), CHUNK, axis=2)
    sink_col = jnp.broadcast_to(swa_sink[:, :, None, None], (depth, SWA_HEADS, SWA_QB, 1))
    w_router_p = jnp.pad(w_router, ((0, 0), (0, 0), (0, ROUTER_LANES - N_EXPERTS)))
    w_router_hi = w_router_p.astype(BF16)
    w_router_lo = (w_router_p - w_router_hi.astype(F32)).astype(BF16)
    w_router_t = jnp.stack([w_router_hi, w_router_lo], axis=1)
    n_moe, n_exp, _, fe = w_exp_gate.shape
    w_exp_down2 = w_exp_down.reshape(n_moe, n_exp * fe, d)
    w_gate_amax = _col_amax_call(w_in, off_gate, N_BRANCHES * d)
    w_gate_scale = jnp.where(w_gate_amax > 0.0, w_gate_amax * (1.0 / FP8_MAX), 1.0)

    for l in range(depth):
        i_layer = l // 2
        kvw = 2 * SWA_KV_W
        tail = [w_in] * ((w_in.shape[2] - off_bv) // kvw)
        h, pkv, h8, h8_scale = _norm_first_call(xs, g_mix, mods, l, 0, n_lat, tail, lambda ck: (None, ck, kvw),
                                                lambda n, c: (l, c, off_bv // kvw + n), kvw, "plain8",
                                                "norm_in_proj_tail")
        tn_in = 1024
        px = _mm_call(h, w_in, l, lambda c: jnp.where(c < off_gate // tn_in, c, off_bk // tn_in),
                      off_gate // tn_in + 1, ms_big, tn_in, "in_proj")
        pg = _mm8_call(h8, h8_scale, w_in, w_gate_scale, l, off_gate, N_BRANCHES * d, ms_big, tn_in, "in_proj_gates")
        y_a = _gmlp_call(px, gmlp_ln_g, gmlp_ln_b, gmlp_ws, bs_full, l, gmlp_chunks)
        y_b = _na_call(px, pkv, na_pairs, l, n_lat, off_bq, (off_gate // tn_in) * tn_in, 0)
        y_c = _swa_call(px, pkv, cs_t, swa_mask, sink_col[l], rot, n_lat, n_ctx, off_cq, off_ck - off_bv)
        acc = _merge_call(y_a, y_b, y_c, pg, w_branch, l, 0, ms_small, 1024)
        xs = _mm_res_call([acc], w_out, l, xs, mods, l, 2, n_lat, ms_small, 1024)
        if l % 2 == 0:
            tn_up = 512
            h, hid0 = _norm_first_call(xs, g_ffn, mods, l, 3, n_lat, [w_ffn_gate, w_ffn_up],
                                       lambda ck: (None, ck, tn_up), lambda n, c: (i_layer, c, 0), tn_up,
                                       "swiglu", "norm_ffn_up")
            hid1 = _ffn_up_call(h, w_ffn_gate, w_ffn_up, i_layer, ms_big, tn_up, 1)
            xs = _mm_res_call([hid0, hid1], w_ffn_down, i_layer, xs, mods, l, 5, n_lat, ms_small, 2048)
        else:
            pair = 2
            h, hid0, gates = _norm_first_call(xs, g_ffn, mods, l, 3, n_lat, [w_exp_gate, w_exp_up],
                                              lambda ck: (None, pair, ck, fe), lambda n, c: (i_layer, 0, c, 0),
                                              pair * fe, "moe", "norm_moe_up", w_router=w_router_t[i_layer], fe=fe)
            hid1 = _moe_up_call(h, w_exp_gate, w_exp_up, gates, i_layer, ms_small, pair, 1)
            xs = _mm_res_call([hid0, hid1], w_exp_down2, i_layer, xs, mods, l, 5, n_lat, ms_big, 1024)
    return _final_norm_call(xs, g_final, n_lat)[None]
```

```python
import functools

import numpy as np
import jax
import jax.numpy as jnp
from jax import lax
from jax.experimental import pallas as pl
from jax.experimental.pallas import tpu as pltpu

F32 = jnp.float32
BF16 = jnp.bfloat16
FP8 = jnp.float8_e4m3fn
FP8_MAX = 448.0

GRID_W = 64
HEAD_DIM = 128
BRANCH_W = 1024
N_BRANCHES = 3
CHUNK = 128
GMLP_GROUPS = 8
NA_HEADS = BRANCH_W // HEAD_DIM
NA_WIN_R = 8
NA_WIN_C = 16
SWA_HEADS = BRANCH_W // HEAD_DIM
SWA_KV_HEADS = 2
SWA_GROUP = SWA_HEADS // SWA_KV_HEADS
SWA_KV_W = SWA_KV_HEADS * HEAD_DIM
SWA_WINDOW = 128
SWA_BLOCK = 128
ROPE_BASE = 10000.0
N_EXPERTS = 8
TOP_K = 2
NEG_INF = -1e30
EPS = 1e-6

NA_QROWS = 4
NA_KROWS = 12
NA_QB = NA_QROWS * GRID_W
NA_KB = NA_KROWS * GRID_W
NA_PIECES = NA_KB // NA_QB

SWA_QB = 2 * SWA_BLOCK
SWA_PIECES = 4

LOG2E = 1.4426950408889634

ROUTER_LANES = 128

MIB = 1024 * 1024
V7X_VMEM_BYTES = 64 * MIB
COMPILER_SCRATCH_BYTES = 6 * MIB


def _params(sem, vmem_bytes):
    limit = vmem_bytes + COMPILER_SCRATCH_BYTES
    assert limit <= V7X_VMEM_BYTES, limit
    return pltpu.CompilerParams(dimension_semantics=sem, vmem_limit_bytes=limit)


def _ada_kernel(c_ref, w_ref, b_ref, o_ref):
    cs = c_ref[...]
    cs = (cs * jax.nn.sigmoid(cs)).astype(BF16)
    w = w_ref[...].astype(BF16)
    o_ref[...] = jnp.dot(cs, w, preferred_element_type=F32) + b_ref[...]


def _ada_mods(c8, w_ada, b_ada, tn=512):
    depth, d, n6 = w_ada.shape
    return pl.pallas_call(
        _ada_kernel,
        grid=(depth, n6 // tn),
        in_specs=[
            pl.BlockSpec((8, d), lambda l, j: (0, 0)),
            pl.BlockSpec((None, d, tn), lambda l, j: (l, 0, j)),
            pl.BlockSpec((None, 1, tn), lambda l, j: (l, 0, j)),
        ],
        out_specs=pl.BlockSpec((None, 8, tn), lambda l, j: (l, 0, j)),
        out_shape=jax.ShapeDtypeStruct((depth, 8, n6), F32),
        compiler_params=_params(("arbitrary", "arbitrary"), 2 * d * tn * 4 + d * tn * 2 + MIB),
        name="ada_mods",
    )(c8, w_ada, b_ada.reshape(depth, 1, n6))


def _norm_mod(x, g_ref, sh_ref, sc_ref):
    ms = jnp.mean(x * x, axis=-1, keepdims=True)
    y = x * lax.rsqrt(ms + EPS) * g_ref[...]
    return y * (1.0 + sc_ref[...]) + sh_ref[...]


def _route_top2(h, wr_ref):
    h_hi = h.astype(BF16)
    h_lo = (h - h_hi.astype(F32)).astype(BF16)
    logits = (jnp.dot(h_hi, wr_ref[0], preferred_element_type=F32)
              + jnp.dot(h_lo, wr_ref[0], preferred_element_type=F32)
              + jnp.dot(h_hi, wr_ref[1], preferred_element_type=F32))
    lane = lax.broadcasted_iota(jnp.int32, logits.shape, 1)
    logits = jnp.where(lane < N_EXPERTS, logits, -jnp.inf)
    m1 = jnp.max(logits, axis=-1, keepdims=True)
    i1 = jnp.min(jnp.where(logits == m1, lane, ROUTER_LANES), axis=-1, keepdims=True)
    first = lane == i1
    rest = jnp.where(first, -jnp.inf, logits)
    m2 = jnp.max(rest, axis=-1, keepdims=True)
    i2 = jnp.min(jnp.where(rest == m2, lane, ROUTER_LANES), axis=-1, keepdims=True)
    second = lane == i2
    e2 = jnp.exp(m2 - m1)
    w1 = 1.0 / (1.0 + e2)
    return jnp.where(first, w1, 0.0) + jnp.where(second, e2 * w1, 0.0)


def _swiglu(hg, hu):
    return hg * jax.nn.sigmoid(hg) * hu


def _expert_weights(gates, first_expert, per_tile, fe):
    lane = lax.broadcasted_iota(jnp.int32, gates.shape, 1)
    out_lane = lax.broadcasted_iota(jnp.int32, (1, per_tile * fe), 1)
    ge = None
    for e in range(per_tile):
        w_e = jnp.sum(jnp.where(lane == first_expert + e, gates, 0.0), axis=-1, keepdims=True)
        ge = w_e if ge is None else jnp.where(out_lane >= e * fe, w_e, ge)
    return ge


def _cast_chunks(w_refs, wb_refs, slot, rows):
    for w_ref, wb_ref in zip(w_refs, wb_refs):
        if len(w_ref.shape) == 3:
            fe = w_ref.shape[2]
            for e in range(w_ref.shape[0]):
                wb_ref[slot, rows, e * fe:(e + 1) * fe] = w_ref[e].astype(BF16)
        else:
            wb_ref[slot, rows, :] = w_ref[...].astype(BF16)


def _norm_first_kernel(*refs, p_steps, ck, n_w, kind, fe):
    x_ref, g_ref, sh_ref, sc_ref = refs[:4]
    w_refs = refs[4:4 + n_w]
    pos = 4 + n_w
    moe = kind == "moe"
    wr_ref = refs[pos] if moe else None
    pos += int(moe)
    h_ref, o_ref = refs[pos:pos + 2]
    pos += 2
    gate_ref = refs[pos] if moe else None
    pos += int(moe)
    quant = kind == "plain8"
    h8_ref, hs_ref = refs[pos:pos + 2] if quant else (None, None)
    wb_refs = refs[pos + 2 * int(quant):]
    kind = "plain" if quant else kind
    s = pl.program_id(0)

    @pl.when(s < p_steps)
    def _():
        _cast_chunks(w_refs, wb_refs, 0, pl.ds(pl.multiple_of(s * ck, ck), ck))

    @pl.when(s >= p_steps)
    def _():
        hf = _norm_mod(x_ref[...], g_ref, sh_ref, sc_ref)
        h = hf.astype(BF16)
        h_ref[...] = h
        if quant:
            amax = jnp.max(jnp.abs(hf), axis=-1, keepdims=True)
            nonzero = amax > 0.0
            h8_ref[...] = (hf * jnp.where(nonzero, FP8_MAX / amax, 1.0)).astype(FP8)
            hs_ref[...] = jnp.broadcast_to(jnp.where(nonzero, amax * (1.0 / FP8_MAX), 1.0), hs_ref.shape)
        prods = [jnp.dot(h, wb_ref[0], preferred_element_type=F32) for wb_ref in wb_refs]
        if kind == "plain":
            out = prods[0] if len(prods) == 1 else jnp.concatenate(prods, axis=1)
        else:
            out = _swiglu(prods[0], prods[1])
        if moe:
            gates = _route_top2(hf, wr_ref)
            gate_ref[...] = gates
            out = out * _expert_weights(gates, 0, o_ref.shape[1] // fe, fe)
        o_ref[...] = out.astype(o_ref.dtype)


def _final_norm_kernel(x_ref, g_ref, o_ref):
    x = x_ref[...]
    ms = jnp.mean(x * x, axis=-1, keepdims=True)
    o_ref[...] = x * lax.rsqrt(ms + EPS) * g_ref[...]


NORM_ROWS = 256
NORM_STAGE_STEPS = 8


def _norm_first_call(x, g, mods, l, which, n_lat, weights, w_block, w_index, tn, kind, name, w_router=None, fe=0):
    out_cols = tn * len(weights) if kind in ("plain", "plain8") else tn
    m, d = x.shape
    tr, p_steps = NORM_ROWS, NORM_STAGE_STEPS
    ck = d // p_steps
    assert m % tr == 0 and n_lat % tr == 0 and d % p_steps == 0
    lat_tiles = n_lat // tr
    tile = lambda s: jnp.maximum(s - p_steps, 0)
    chunk = lambda s: jnp.minimum(s, p_steps - 1)
    row_kind = lambda s: jnp.where(tile(s) >= lat_tiles, 1, 0)
    mod_spec = lambda k: pl.BlockSpec((None, None, None, 1, d), lambda s: (l, row_kind(s), k, 0, 0))
    in_specs = [pl.BlockSpec((tr, d), lambda s: (tile(s), 0)),
                pl.BlockSpec((None, 1, d), lambda s: (l, 0, 0)),
                mod_spec(which), mod_spec(which + 1)]
    in_specs += [pl.BlockSpec(w_block(ck), functools.partial(lambda n, s: w_index(n, chunk(s)), n))
                 for n in range(len(weights))]
    args = [x, g.reshape(g.shape[0], 1, d), mods, mods, *weights]
    out_specs = [pl.BlockSpec((tr, d), lambda s: (tile(s), 0)),
                 pl.BlockSpec((tr, out_cols), lambda s: (tile(s), 0))]
    out_shape = [jax.ShapeDtypeStruct((m, d), BF16), jax.ShapeDtypeStruct((m, out_cols), BF16)]
    vmem = (2 * (tr * d * 4 + tr * d * 2 + tr * out_cols * 2 + len(weights) * ck * tn * 4)
            + len(weights) * d * tn * 2 + 3 * tr * d * 4 + 3 * tr * out_cols * 4)
    if kind == "moe":
        in_specs.append(pl.BlockSpec((2, d, ROUTER_LANES), lambda s: (0, 0, 0)))
        args.append(w_router)
        out_specs.append(pl.BlockSpec((tr, ROUTER_LANES), lambda s: (tile(s), 0)))
        out_shape.append(jax.ShapeDtypeStruct((m, ROUTER_LANES), F32))
        vmem += 2 * 2 * d * ROUTER_LANES * 2 + 2 * tr * d * 4
    if kind == "plain8":
        out_specs += [pl.BlockSpec((tr, d), lambda s: (tile(s), 0)),
                      pl.BlockSpec((tr, ROUTER_LANES), lambda s: (tile(s), 0))]
        out_shape += [jax.ShapeDtypeStruct((m, d), FP8), jax.ShapeDtypeStruct((m, ROUTER_LANES), F32)]
        vmem += 2 * (tr * d + tr * ROUTER_LANES * 4)
    return pl.pallas_call(
        functools.partial(_norm_first_kernel, p_steps=p_steps, ck=ck, n_w=len(weights), kind=kind, fe=fe),
        grid=(p_steps + m // tr,),
        in_specs=in_specs,
        out_specs=out_specs,
        out_shape=out_shape,
        scratch_shapes=[pltpu.VMEM((1, d, tn), BF16) for _ in weights],
        compiler_params=_params(("arbitrary",), vmem),
        name=name,
    )(*args)


def _final_norm_call(x, g, n_lat, tr=256):
    d = x.shape[1]
    return pl.pallas_call(
        _final_norm_kernel,
        grid=(n_lat // tr,),
        in_specs=[pl.BlockSpec((tr, d), lambda i: (i, 0)),
                  pl.BlockSpec((1, d), lambda i: (0, 0))],
        out_specs=pl.BlockSpec((tr, d), lambda i: (i, 0)),
        out_shape=jax.ShapeDtypeStruct((n_lat, d), F32),
        compiler_params=_params(("arbitrary",), 6 * tr * d * 4),
        name="final_norm",
    )(x, g.reshape(1, d))


class _Tiling:
    def __init__(self, m, k, n, ms, tn):
        assert m % ms == 0 and k % ms == 0 and n % tn == 0
        self.ms, self.tn, self.nt = ms, tn, n // tn
        self.tm, self.ck = m // ms, k // ms
        assert self.tm % 16 == 0 and self.ck % 16 == 0
        self.grid = (self.nt + 1, ms)

    def row(self, j, i):
        return jnp.where(j == 0, 0, i)

    def col(self, j):
        return jnp.maximum(j - 1, 0)

    def wrow(self, j, i):
        return jnp.where(j == self.nt, self.ms - 1, i)

    def wcol(self, j):
        return jnp.minimum(j, self.nt - 1)


def _stage_weight(t, w_refs, wb_refs):
    j, i = pl.program_id(0), pl.program_id(1)

    @pl.when(j < t.nt)
    def _():
        _cast_chunks(w_refs, wb_refs, j % 2, pl.ds(pl.multiple_of(i * t.ck, t.ck), t.ck))


def _mm_kernel(a_ref, w_ref, o_ref, wb_ref, *, t):
    j = pl.program_id(0)
    _stage_weight(t, (w_ref,), (wb_ref,))

    @pl.when(j > 0)
    def _():
        o_ref[...] = jnp.dot(a_ref[...], wb_ref[(j + 1) % 2], preferred_element_type=F32).astype(o_ref.dtype)


def _mm_call(a, w, l, w_tile, n_tiles, ms, tn, name):
    m, k = a.shape
    t = _Tiling(m, k, n_tiles * tn, ms, tn)
    vmem = 2 * (t.tm * k * 2 + t.ck * tn * 4 + t.tm * tn * 2) + 2 * k * tn * 2 + t.tm * tn * 4
    return pl.pallas_call(
        functools.partial(_mm_kernel, t=t),
        grid=t.grid,
        in_specs=[pl.BlockSpec((t.tm, k), lambda j, i: (t.row(j, i), 0)),
                  pl.BlockSpec((None, t.ck, tn), lambda j, i: (l, t.wrow(j, i), w_tile(t.wcol(j))))],
        out_specs=pl.BlockSpec((t.tm, tn), lambda j, i: (t.row(j, i), t.col(j))),
        out_shape=jax.ShapeDtypeStruct((m, n_tiles * tn), BF16),
        scratch_shapes=[pltpu.VMEM((2, k, tn), BF16)],
        compiler_params=_params(("arbitrary", "arbitrary"), vmem),
        name=name,
    )(a, w)


def _mm8_kernel(a_ref, hs_ref, wa_ref, wc_ref, o_ref, wb_ref, amax_ref, *, nt, ck):
    j, i = pl.program_id(0), pl.program_id(1)

    @pl.when(j < nt)
    def _():
        slot = j % 3

        @pl.when(i == 0)
        def _():
            amax_ref[slot] = jnp.zeros(amax_ref.shape[1:], F32)

        amax_ref[slot] = jnp.maximum(amax_ref[slot], jnp.max(jnp.abs(wa_ref[...]), axis=0, keepdims=True))

    @pl.when((j >= 1) & (j <= nt))
    def _():
        amax = amax_ref[(j + 2) % 3]
        inv = jnp.where(amax > 0.0, FP8_MAX / amax, 1.0)
        rows = pl.ds(pl.multiple_of(i * ck, ck), ck)
        wb_ref[(j + 1) % 2, rows, :] = (wc_ref[...] * inv).astype(FP8)

    @pl.when(j >= 2)
    def _():
        amax = amax_ref[(j + 1) % 3]
        scale = jnp.where(amax > 0.0, amax * (1.0 / FP8_MAX), 1.0)
        y = jnp.dot(a_ref[...], wb_ref[j % 2], preferred_element_type=F32)
        o_ref[...] = (y * hs_ref[:, :1] * scale).astype(o_ref.dtype)


def _mm8_call(a8, a_scale, w, l, col0, ncols, ms, tn, name):
    m, k = a8.shape
    assert m % ms == 0 and k % ms == 0 and ncols % tn == 0 and col0 % tn == 0
    nt, tm, ck = ncols // tn, m // ms, k // ms
    assert tm % 32 == 0 and ck % 32 == 0
    lanes = a_scale.shape[1]
    row = lambda j, i: jnp.where(j < 2, 0, i)
    col = lambda j: jnp.maximum(j - 2, 0)
    scan = lambda j, i: (l, jnp.where(j >= nt, ms - 1, i), col0 // tn + jnp.minimum(j, nt - 1))
    cast = lambda j, i: (l, jnp.where(j == 0, 0, jnp.where(j > nt, ms - 1, i)), col0 // tn + jnp.clip(j - 1, 0, nt - 1))
    vmem = 2 * (tm * k + 2 * ck * tn * 4 + tm * tn * 2 + tm * lanes * 4) + 2 * k * tn + 2 * tm * tn * 4
    return pl.pallas_call(
        functools.partial(_mm8_kernel, nt=nt, ck=ck),
        grid=(nt + 2, ms),
        in_specs=[pl.BlockSpec((tm, k), lambda j, i: (row(j, i), 0)),
                  pl.BlockSpec((tm, lanes), lambda j, i: (row(j, i), 0)),
                  pl.BlockSpec((None, ck, tn), scan),
                  pl.BlockSpec((None, ck, tn), cast)],
        out_specs=pl.BlockSpec((tm, tn), lambda j, i: (row(j, i), col(j))),
        out_shape=jax.ShapeDtypeStruct((m, ncols), BF16),
        scratch_shapes=[pltpu.VMEM((2, k, tn), FP8), pltpu.VMEM((3, 1, tn), F32)],
        compiler_params=_params(("arbitrary", "arbitrary"), vmem),
        name=name,
    )(a8, a_scale, w, w)


def _mm_res_kernel(*refs, t, n_lat, ks):
    a_refs = refs[:len(ks)]
    w_ref, x_ref, gx_ref, gc_ref, o_ref, wb_ref = refs[len(ks):]
    j, i = pl.program_id(0), pl.program_id(1)
    _stage_weight(t, (w_ref,), (wb_ref,))

    @pl.when(j > 0)
    def _():
        slot = (j + 1) % 2
        y, k0 = None, 0
        for a_ref, k in zip(a_refs, ks):
            part = jnp.dot(a_ref[...], wb_ref[slot, k0:k0 + k, :], preferred_element_type=F32)
            y = part if y is None else y + part
            k0 += k
        row = i * t.tm + lax.broadcasted_iota(jnp.int32, (t.tm, 1), 0)
        gate = jnp.where(row < n_lat, gx_ref[...], gc_ref[...])
        o_ref[...] = x_ref[...] + gate * y


def _mm_res_call(a_parts, w, l, x, mods, lm, which, n_lat, ms, tn):
    m = a_parts[0].shape[0]
    ks = tuple(a.shape[1] for a in a_parts)
    k = sum(ks)
    d = w.shape[2]
    assert w.shape[1] == k
    t = _Tiling(m, k, d, ms, tn)
    gate_spec = lambda kind: pl.BlockSpec((None, None, None, 1, tn),
                                          lambda j, i: (lm, kind, which, 0, t.col(j)))
    xo_spec = pl.BlockSpec((t.tm, tn), lambda j, i: (t.row(j, i), t.col(j)))
    vmem = 2 * (t.tm * k * 2 + t.ck * tn * 4 + 2 * t.tm * tn * 4) + 2 * k * tn * 2 + 2 * t.tm * tn * 4
    return pl.pallas_call(
        functools.partial(_mm_res_kernel, t=t, n_lat=n_lat, ks=ks),
        grid=t.grid,
        in_specs=[pl.BlockSpec((t.tm, kp), lambda j, i: (t.row(j, i), 0)) for kp in ks]
                 + [pl.BlockSpec((None, t.ck, tn), lambda j, i: (l, t.wrow(j, i), t.wcol(j))),
                    xo_spec, gate_spec(0), gate_spec(1)],
        out_specs=xo_spec,
        out_shape=jax.ShapeDtypeStruct((m, d), F32),
        scratch_shapes=[pltpu.VMEM((2, k, tn), BF16)],
        compiler_params=_params(("arbitrary", "arbitrary"), vmem),
        name="proj_residual",
    )(*a_parts, w, x, mods, mods)


def _merge_kernel(ya_ref, yb_ref, yc_ref, ga_ref, gb_ref, gc_ref, w_ref, o_ref, wb_ref, *, t, bw):
    j = pl.program_id(0)
    _stage_weight(t, (w_ref,), (wb_ref,))

    @pl.when(j > 0)
    def _():
        slot = (j + 1) % 2
        acc = None
        for br, (y_ref, g_ref) in enumerate(((ya_ref, ga_ref), (yb_ref, gb_ref), (yc_ref, gc_ref))):
            part = jnp.dot(y_ref[...], wb_ref[slot, br * bw:(br + 1) * bw, :], preferred_element_type=F32)
            part = jax.nn.sigmoid(g_ref[...].astype(F32)) * part
            acc = part if acc is None else acc + part
        o_ref[...] = acc.astype(o_ref.dtype)


def _merge_call(ya, yb, yc, px, w_branch, l, off_gate, ms, tn):
    m, bw = ya.shape
    d = w_branch.shape[3]
    k = N_BRANCHES * bw
    t = _Tiling(m, k, d, ms, tn)
    y_spec = pl.BlockSpec((t.tm, bw), lambda j, i: (t.row(j, i), 0))
    g_spec = lambda br: pl.BlockSpec((t.tm, tn), lambda j, i: (t.row(j, i), (off_gate + br * d) // tn + t.col(j)))
    vmem = (2 * (3 * t.tm * bw * 2 + 3 * t.tm * tn * 2 + t.ck * tn * 4 + t.tm * tn * 2) + 2 * k * tn * 2
            + 2 * t.tm * tn * 4)
    return pl.pallas_call(
        functools.partial(_merge_kernel, t=t, bw=bw),
        grid=t.grid,
        in_specs=[y_spec, y_spec, y_spec, g_spec(0), g_spec(1), g_spec(2),
                  pl.BlockSpec((None, t.ck, tn), lambda j, i: (l, t.wrow(j, i), t.wcol(j)))],
        out_specs=pl.BlockSpec((t.tm, tn), lambda j, i: (t.row(j, i), t.col(j))),
        out_shape=jax.ShapeDtypeStruct((m, d), BF16),
        scratch_shapes=[pltpu.VMEM((2, k, tn), BF16)],
        compiler_params=_params(("arbitrary", "arbitrary"), vmem),
        name="branch_merge",
    )(ya, yb, yc, px, px, px, w_branch.reshape(w_branch.shape[0], k, d))


def _ffn_up_kernel(a_ref, wg_ref, wu_ref, o_ref, wgb_ref, wub_ref, *, t):
    j = pl.program_id(0)
    _stage_weight(t, (wg_ref, wu_ref), (wgb_ref, wub_ref))

    @pl.when(j > 0)
    def _():
        slot = (j + 1) % 2
        a = a_ref[...]
        hg = jnp.dot(a, wgb_ref[slot], preferred_element_type=F32)
        hu = jnp.dot(a, wub_ref[slot], preferred_element_type=F32)
        o_ref[...] = _swiglu(hg, hu).astype(o_ref.dtype)


def _moe_up_kernel(a_ref, wg_ref, wu_ref, gate_ref, o_ref, wgb_ref, wub_ref, *, t, fe, tile0):
    j = pl.program_id(0)
    _stage_weight(t, (wg_ref, wu_ref), (wgb_ref, wub_ref))

    @pl.when(j > 0)
    def _():
        slot = (j + 1) % 2
        a = a_ref[...]
        hg = jnp.dot(a, wgb_ref[slot], preferred_element_type=F32)
        hu = jnp.dot(a, wub_ref[slot], preferred_element_type=F32)
        per_tile = t.tn // fe
        ge = _expert_weights(gate_ref[...], (tile0 + j - 1) * per_tile, per_tile, fe)
        o_ref[...] = (_swiglu(hg, hu) * ge).astype(o_ref.dtype)


def _ffn_up_call(a, wg, wu, i_layer, ms, tn, tile0):
    m, k = a.shape
    f = wg.shape[2] - tile0 * tn
    t = _Tiling(m, k, f, ms, tn)
    w_spec = pl.BlockSpec((None, t.ck, tn), lambda j, i: (i_layer, t.wrow(j, i), tile0 + t.wcol(j)))
    vmem = 2 * (t.tm * k * 2 + 2 * t.ck * tn * 4 + t.tm * tn * 2) + 4 * k * tn * 2 + 3 * t.tm * tn * 4
    return pl.pallas_call(
        functools.partial(_ffn_up_kernel, t=t),
        grid=t.grid,
        in_specs=[pl.BlockSpec((t.tm, k), lambda j, i: (t.row(j, i), 0)), w_spec, w_spec],
        out_specs=pl.BlockSpec((t.tm, tn), lambda j, i: (t.row(j, i), t.col(j))),
        out_shape=jax.ShapeDtypeStruct((m, f), BF16),
        scratch_shapes=[pltpu.VMEM((2, k, tn), BF16), pltpu.VMEM((2, k, tn), BF16)],
        compiler_params=_params(("arbitrary", "arbitrary"), vmem),
        name="ffn_up",
    )(a, wg, wu)


def _moe_up_call(a, wg, wu, gates, i_layer, ms, per_tile, tile0):
    m, k = a.shape
    n_exp, fe = wg.shape[1], wg.shape[3]
    tn = per_tile * fe
    t = _Tiling(m, k, n_exp * fe - tile0 * tn, ms, tn)
    w_spec = pl.BlockSpec((None, per_tile, t.ck, fe),
                          lambda j, i: (i_layer, tile0 + t.wcol(j), t.wrow(j, i), 0))
    vmem = (2 * (t.tm * k * 2 + 2 * t.ck * tn * 4 + t.tm * tn * 2 + t.tm * ROUTER_LANES * 4) + 4 * k * tn * 2
            + 3 * t.tm * tn * 4)
    return pl.pallas_call(
        functools.partial(_moe_up_kernel, t=t, fe=fe, tile0=tile0),
        grid=t.grid,
        in_specs=[pl.BlockSpec((t.tm, k), lambda j, i: (t.row(j, i), 0)), w_spec, w_spec,
                  pl.BlockSpec((t.tm, ROUTER_LANES), lambda j, i: (t.row(j, i), 0))],
        out_specs=pl.BlockSpec((t.tm, tn), lambda j, i: (t.row(j, i), t.col(j))),
        out_shape=jax.ShapeDtypeStruct((m, t.nt * tn), BF16),
        scratch_shapes=[pltpu.VMEM((2, k, tn), BF16), pltpu.VMEM((2, k, tn), BF16)],
        compiler_params=_params(("arbitrary", "arbitrary"), vmem),
        name="moe_up",
    )(a, wg, wu, gates)


def _gmlp_kernel(u_ref, v_ref, lng_ref, lnb_ref, ws_ref, bs_ref, o_ref, *, chunks):
    u = jax.nn.gelu(u_ref[...].astype(F32))
    v = jax.nn.gelu(v_ref[...].astype(F32))
    mu = jnp.mean(v, axis=-1, keepdims=True)
    var = jnp.mean(jnp.square(v - mu), axis=-1, keepdims=True)
    vn = ((v - mu) * lax.rsqrt(var + EPS) * lng_ref[...] + lnb_ref[...]).astype(BF16)
    for g in range(GMLP_GROUPS):
        cols = slice(g * CHUNK, (g + 1) * CHUNK)
        wsg = ws_ref[g].astype(BF16)
        for c in range(chunks):
            rows = slice(c * CHUNK, (c + 1) * CHUNK)
            s = jnp.dot(wsg, vn[rows, cols], preferred_element_type=F32) + bs_ref[:, cols]
            o_ref[rows, cols] = (u[rows, cols] * s).astype(o_ref.dtype)


def _gmlp_call(px, lng, lnb, ws, bs_full, l, chunks):
    m = px.shape[0]
    t = chunks * CHUNK
    return pl.pallas_call(
        functools.partial(_gmlp_kernel, chunks=chunks),
        grid=(m // t,),
        in_specs=[pl.BlockSpec((t, BRANCH_W), lambda i: (i, 0)),
                  pl.BlockSpec((t, BRANCH_W), lambda i: (i, 1)),
                  pl.BlockSpec((None, 1, BRANCH_W), lambda i: (l, 0, 0)),
                  pl.BlockSpec((None, 1, BRANCH_W), lambda i: (l, 0, 0)),
                  pl.BlockSpec((None, GMLP_GROUPS, CHUNK, CHUNK), lambda i: (l, 0, 0, 0)),
                  pl.BlockSpec((None, CHUNK, BRANCH_W), lambda i: (l, 0, 0))],
        out_specs=pl.BlockSpec((t, BRANCH_W), lambda i: (i, 0)),
        out_shape=jax.ShapeDtypeStruct((m, BRANCH_W), BF16),
        compiler_params=_params(("arbitrary",), 6 * t * BRANCH_W * 2 + 6 * t * BRANCH_W * 4 + 2 * MIB),
        name="gmlp",
    )(px, px, lng.reshape(-1, 1, BRANCH_W), lnb.reshape(-1, 1, BRANCH_W), ws, bs_full)


def _dot_nt(a, b):
    return lax.dot_general(a, b, (((1,), (1,)), ((), ())), preferred_element_type=F32)


def _tree(op, xs):
    while len(xs) > 1:
        xs = [op(xs[i], xs[i + 1]) if i + 1 < len(xs) else xs[i] for i in range(0, len(xs), 2)]
    return xs[0]


def _with_ones(v):
    return jnp.concatenate([v, jnp.ones_like(v)], axis=1)


def _softmax_pv(s, v1_tiles, extra_logit=None):
    width = min(t.shape[1] for t in s)
    parts = [t[:, i:i + width] for t in s for i in range(0, t.shape[1], width)]
    m = jnp.max(_tree(jnp.maximum, parts), axis=-1, keepdims=True)
    if extra_logit is not None:
        m = jnp.maximum(m, extra_logit)
    o = _tree(jnp.add, [jnp.dot(jnp.exp2(t - m).astype(BF16), v1, preferred_element_type=F32)
                        for t, v1 in zip(s, v1_tiles)])
    hd = o.shape[1] // 2
    denom = o[:, hd:]
    if extra_logit is not None:
        denom = denom + jnp.exp2(extra_logit - m)
    return o[:, :hd] / denom


def _na_kernel(q_ref, k0_ref, k1_ref, k2_ref, v0_ref, v1_ref, v2_ref, kc_ref, vc_ref, pair_ref, mask_ref,
               o_ref, bias_ref, *, nb, roff):
    b = pl.program_id(0)
    left = lax.broadcasted_iota(jnp.int32, (GRID_W, 2 * GRID_W), 1) < GRID_W

    def build(kind):
        for h in range(NA_HEADS):
            for qr in range(NA_QROWS):
                rows = slice(qr * GRID_W, (qr + 1) * GRID_W)
                for p in range(NA_KROWS // 2):
                    cols = slice(p * 2 * GRID_W, (p + 1) * 2 * GRID_W)
                    pair = jnp.where(left, pair_ref[h, roff[kind][qr][2 * p]], pair_ref[h, roff[kind][qr][2 * p + 1]])
                    bias_ref[h, rows, cols] = pair * LOG2E + mask_ref[rows, cols]

    for kind, first_block in enumerate((0, 1, nb - 1)):
        pl.when(b == first_block)(functools.partial(build, kind))

    @pl.when(b == nb)
    def _():
        for h in range(NA_HEADS):
            bias_ref[h] = mask_ref[...]

    k_refs = (k0_ref, k1_ref, k2_ref)
    v_refs = (v0_ref, v1_ref, v2_ref)
    for h in range(NA_HEADS):
        cols = slice(h * HEAD_DIM, (h + 1) * HEAD_DIM)
        q = (q_ref[:, cols].astype(F32) * (HEAD_DIM ** -0.5 * LOG2E)).astype(BF16)
        s = [_dot_nt(q, k_refs[j][:, cols]) + bias_ref[h, :, j * NA_QB:(j + 1) * NA_QB] for j in range(NA_PIECES)]
        s.append(_dot_nt(q, kc_ref[:, cols]))
        v1 = [_with_ones(v_refs[j][:, cols]) for j in range(NA_PIECES)] + [_with_ones(vc_ref[:, cols])]
        o_ref[:, cols] = _softmax_pv(s, v1).astype(o_ref.dtype)


def _na_call(px, pv, pairs, l, n_lat, off_q, off_k, off_v):
    m = px.shape[0]
    nb = n_lat // NA_QB
    assert nb >= 3
    w = BRANCH_W
    roff, mask = _na_geometry(n_lat // GRID_W)
    kstart = lambda b: jnp.clip(b - 1, 0, nb - NA_PIECES)
    k_spec = lambda off, j: pl.BlockSpec((NA_QB, w), lambda b: (kstart(b) + j, off // w))
    kind = lambda b: jnp.where(b == 0, 0, jnp.where(b == nb - 1, 2, jnp.where(b == nb, 3, 1)))
    n_rel = 2 * NA_WIN_R - 1
    vmem = (2 * (9 * NA_QB * w * 2 + NA_HEADS * n_rel * GRID_W * 2 * GRID_W * 4 + NA_QB * NA_KB * 4 + NA_QB * w * 2)
            + NA_HEADS * NA_QB * NA_KB * 4 + 16 * NA_QB * NA_QB * 4)
    return pl.pallas_call(
        functools.partial(_na_kernel, nb=nb, roff=roff),
        grid=(m // NA_QB,),
        in_specs=[pl.BlockSpec((NA_QB, w), lambda b: (b, off_q // w)),
                  k_spec(off_k, 0), k_spec(off_k, 1), k_spec(off_k, 2),
                  k_spec(off_v, 0), k_spec(off_v, 1), k_spec(off_v, 2),
                  pl.BlockSpec((NA_QB, w), lambda b: (nb, off_k // w)),
                  pl.BlockSpec((NA_QB, w), lambda b: (nb, off_v // w)),
                  pl.BlockSpec((None, NA_HEADS, n_rel, GRID_W, 2 * GRID_W), lambda b: (l, 0, 0, 0, 0)),
                  pl.BlockSpec((None, NA_QB, NA_KB), lambda b: (kind(b), 0, 0))],
        out_specs=pl.BlockSpec((NA_QB, w), lambda b: (b, 0)),
        out_shape=jax.ShapeDtypeStruct((m, w), BF16),
        scratch_shapes=[pltpu.VMEM((NA_HEADS, NA_QB, NA_KB), F32)],
        compiler_params=_params(("arbitrary",), vmem),
        name="neighbourhood_attention",
    )(px, px, px, px, pv, pv, pv, px, pv, pairs, mask)


def _na_geometry(rows):
    col = np.arange(GRID_W)
    c0 = np.clip(col - NA_WIN_C // 2, 0, GRID_W - NA_WIN_C)
    in_win = (col[None, :] >= c0[:, None]) & (col[None, :] < c0[:, None] + NA_WIN_C)
    roffs, masks = [], []
    for r0, s0 in ((0, 0), (NA_QROWS, 0), (rows - NA_QROWS, rows - NA_KROWS)):
        r = r0 + np.arange(NA_QROWS)
        start = np.clip(r - NA_WIN_R // 2, 0, rows - NA_WIN_R)
        key_row = s0 + np.arange(NA_KROWS)
        valid_r = (key_row[None, :] >= start[:, None]) & (key_row[None, :] < start[:, None] + NA_WIN_R)
        roff = np.clip(key_row[None, :] - r[:, None] + (NA_WIN_R - 1), 0, 2 * NA_WIN_R - 2)
        valid = valid_r[:, None, :, None] & in_win[None, :, None, :]
        roffs.append(tuple(tuple(int(v) for v in row) for row in roff))
        masks.append(np.where(valid, 0.0, NEG_INF).reshape(NA_QB, NA_KB))
    masks.append(np.full((NA_QB, NA_KB), NEG_INF))
    return tuple(roffs), jnp.asarray(np.stack(masks), F32)


def _na_bias_pairs(rpb):
    qcol = np.arange(GRID_W)
    kcol = np.arange(2 * GRID_W) % GRID_W
    coff = np.clip(kcol[None, :] - qcol[:, None] + (NA_WIN_C - 1), 0, 2 * NA_WIN_C - 2)
    pick = (coff[:, :, None] == np.arange(2 * NA_WIN_C - 1)).astype(np.float32)
    return jnp.einsum('lhij,cdj->lhicd', rpb.astype(F32), pick, precision=lax.Precision.HIGHEST)


def _swa_kernel(q_ref, csq_ref, kv0_ref, kv1_ref, kv2_ref, kv3_ref, cs0_ref, cs1_ref, cs2_ref, cs3_ref,
                kvc_ref, mask_ref, sink_ref, rot_ref, o_ref):
    rot = rot_ref[...]
    hd = HEAD_DIM

    def rope(x, cs):
        swapped = jnp.dot(x, rot, preferred_element_type=F32)
        return x.astype(F32) * cs[:, :hd] + swapped * cs[:, hd:]

    csq = csq_ref[...]
    kv_refs = (kv0_ref, kv1_ref, kv2_ref, kv3_ref)
    cs_refs = (cs0_ref, cs1_ref, cs2_ref, cs3_ref)
    for kv in range(SWA_KV_HEADS):
        kcols = slice(kv * hd, (kv + 1) * hd)
        vcols = slice(SWA_KV_W + kv * hd, SWA_KV_W + (kv + 1) * hd)
        k_tiles = [rope(kv_refs[j][:, kcols], cs_refs[j][...]).astype(BF16) for j in range(SWA_PIECES)]
        k_tiles.append(kvc_ref[:, kcols])
        v1 = [_with_ones(kv_refs[j][:, vcols]) for j in range(SWA_PIECES)] + [_with_ones(kvc_ref[:, vcols])]
        for h in range(kv * SWA_GROUP, (kv + 1) * SWA_GROUP):
            cols = slice(h * hd, (h + 1) * hd)
            q = (rope(q_ref[:, cols], csq) * (hd ** -0.5 * LOG2E)).astype(BF16)
            s = [_dot_nt(q, k_tiles[j]) + mask_ref[:, j * SWA_BLOCK:(j + 1) * SWA_BLOCK] for j in range(SWA_PIECES)]
            s.append(_dot_nt(q, k_tiles[SWA_PIECES]))
            o_ref[:, cols] = _softmax_pv(s, v1, extra_logit=sink_ref[h] * LOG2E).astype(o_ref.dtype)


def _swa_call(px, pkv, cs_t, mask, sink_col, rot, n_lat, n_ctx, off_q, off_kv):
    kvb = off_kv // (2 * SWA_KV_W)
    assert off_kv % (2 * SWA_KV_W) == 0
    m = px.shape[0]
    blk, w = SWA_BLOCK, BRANCH_W
    nbl = n_lat // blk
    nbq = n_lat // SWA_QB
    assert n_ctx == SWA_QB and nbq >= 2
    piece = lambda b, j: jnp.clip(2 * b - 1 + j, 0, nbl - 1)
    kind = lambda b: jnp.where(b == 0, 0, jnp.where(b == nbq - 1, 2, jnp.where(b == nbq, 3, 1)))
    kv_spec = lambda j: pl.BlockSpec((blk, 2 * SWA_KV_W), lambda b: (piece(b, j), kvb))
    cs_spec = lambda j: pl.BlockSpec((blk, 2 * HEAD_DIM), lambda b: (piece(b, j), 0))
    vmem = (2 * (2 * SWA_QB * w * 2 + SWA_QB * SWA_PIECES * blk * 4 + SWA_HEADS * SWA_QB * HEAD_DIM * 4)
            + 16 * SWA_QB * (SWA_PIECES * blk + n_ctx) * 4)
    return pl.pallas_call(
        _swa_kernel,
        grid=(m // SWA_QB,),
        in_specs=[pl.BlockSpec((SWA_QB, w), lambda b: (b, off_q // w)),
                  pl.BlockSpec((SWA_QB, 2 * HEAD_DIM), lambda b: (b, 0)),
                  kv_spec(0), kv_spec(1), kv_spec(2), kv_spec(3),
                  cs_spec(0), cs_spec(1), cs_spec(2), cs_spec(3),
                  pl.BlockSpec((n_ctx, 2 * SWA_KV_W), lambda b: (n_lat // n_ctx, kvb)),
                  pl.BlockSpec((None, SWA_QB, SWA_PIECES * blk), lambda b: (kind(b), 0, 0)),
                  pl.BlockSpec((SWA_HEADS, SWA_QB, 1), lambda b: (0, 0, 0)),
                  pl.BlockSpec((HEAD_DIM, HEAD_DIM), lambda b: (0, 0))],
        out_specs=pl.BlockSpec((SWA_QB, w), lambda b: (b, 0)),
        out_shape=jax.ShapeDtypeStruct((m, w), BF16),
        compiler_params=_params(("arbitrary",), vmem),
        name="window_gqa",
    )(px, cs_t, pkv, pkv, pkv, pkv, cs_t, cs_t, cs_t, cs_t, pkv, mask, sink_col, rot)


def _rope_tables(n_lat, n_ctx):
    t = jnp.arange(n_lat, dtype=jnp.int32)
    half = HEAD_DIM // 2
    inv = ROPE_BASE ** (-jnp.arange(0, half, 2, dtype=F32) / half)
    parts_c, parts_s = [], []
    for pos in (t // GRID_W, t % GRID_W):
        ang = pos.astype(F32)[:, None] * inv[None, :]
        parts_c += [jnp.cos(ang), jnp.cos(ang)]
        parts_s += [jnp.sin(ang), jnp.sin(ang)]
    cos_t = jnp.concatenate(parts_c, axis=-1)
    sin_t = jnp.concatenate(parts_s, axis=-1)
    cos_t = jnp.concatenate([cos_t, jnp.ones((n_ctx, HEAD_DIM), F32)], axis=0)
    sin_t = jnp.concatenate([sin_t, jnp.zeros((n_ctx, HEAD_DIM), F32)], axis=0)
    quarter = half // 2
    rot = np.zeros((HEAD_DIM, HEAD_DIM), np.float32)
    for lane in range(HEAD_DIM):
        if lane % half < quarter:
            rot[lane + quarter, lane] = -1.0
        else:
            rot[lane - quarter, lane] = 1.0
    return jnp.concatenate([cos_t, sin_t], axis=1), jnp.asarray(rot, BF16)


def _swa_mask():
    a = np.arange(SWA_QB)
    j = np.arange(SWA_PIECES * SWA_BLOCK)
    in_band = np.abs((j[None, :] - SWA_BLOCK) - a[:, None]) <= SWA_WINDOW
    last = (SWA_PIECES - 1) * SWA_BLOCK
    piece_ok = {0: j >= SWA_BLOCK, 1: j >= 0, 2: j < last, 3: j < 0}
    tabs = [np.where(in_band & piece_ok[kind][None, :], 0.0, NEG_INF) for kind in range(4)]
    return jnp.asarray(np.stack(tabs), F32)


def kernel(x, c, ctx, c_ctx, w_ada, b_ada, g_mix, w_in, gmlp_ln_g, gmlp_ln_b, gmlp_ws, gmlp_bs, na_rpb,
           swa_sink, w_branch, w_out, g_ffn, w_ffn_gate, w_ffn_up, w_ffn_down, w_router, w_exp_gate,
           w_exp_up, w_exp_down, g_final):
    batch, n_lat, d = x.shape
    n_ctx = ctx.shape[1]
    depth = w_ada.shape[0]
    assert batch == 1 and c.shape[0] == 1
    assert n_lat % NA_QB == 0 and n_ctx == NA_QB and n_lat // GRID_W >= NA_KROWS
    m = n_lat + n_ctx

    off_au, off_av = 0, BRANCH_W
    off_bq, off_cq = 2 * BRANCH_W, 3 * BRANCH_W
    off_gate = 4 * BRANCH_W
    off_bk = off_gate + N_BRANCHES * d
    off_bv = off_bk + BRANCH_W
    off_ck = off_bv + BRANCH_W
    off_cv = off_ck + SWA_KV_W
    assert w_in.shape[2] == off_cv + SWA_KV_W and off_au == 0 and off_av == BRANCH_W

    ms_big, ms_small = 8, 16
    gmlp_chunks = 6 if m % (6 * CHUNK) == 0 else 2

    xs = jnp.concatenate([x[0], ctx[0]], axis=0)
    c8 = jnp.concatenate([c, c_ctx[None, :], jnp.zeros((6, d), F32)], axis=0)
    mods = _ada_mods(c8, w_ada, b_ada)[:, :2].reshape(depth, 2, 6, 1, d)

    cs_t, rot = _rope_tables(n_lat, n_ctx)
    swa_mask = _swa_mask()
    na_pairs = _na_bias_pairs(na_rpb)
    bs_full = jnp.repeat(jnp.swapaxes(gmlp_bs, 1, 2), CHUNK, axis=2)
    sink_col = jnp.broadcast_to(swa_sink[:, :, None, None], (depth, SWA_HEADS, SWA_QB, 1))
    w_router_p = jnp.pad(w_router, ((0, 0), (0, 0), (0, ROUTER_LANES - N_EXPERTS)))
    w_router_hi = w_router_p.astype(BF16)
    w_router_lo = (w_router_p - w_router_hi.astype(F32)).astype(BF16)
    w_router_t = jnp.stack([w_router_hi, w_router_lo], axis=1)
    n_moe, n_exp, _, fe = w_exp_gate.shape
    w_exp_down2 = w_exp_down.reshape(n_moe, n_exp * fe, d)

    for l in range(depth):
        i_layer = l // 2
        kvw = 2 * SWA_KV_W
        tail = [w_in] * ((w_in.shape[2] - off_bv) // kvw)
        h, pkv, h8, h8_scale = _norm_first_call(xs, g_mix, mods, l, 0, n_lat, tail, lambda ck: (None, ck, kvw),
                                                lambda n, c: (l, c, off_bv // kvw + n), kvw, "plain8",
                                                "norm_in_proj_tail")
        tn_in = 1024
        px = _mm_call(h, w_in, l, lambda c: jnp.where(c < off_gate // tn_in, c, off_bk // tn_in),
                      off_gate // tn_in + 1, ms_big, tn_in, "in_proj")
        pg = _mm8_call(h8, h8_scale, w_in, l, off_gate, N_BRANCHES * d, ms_big, tn_in, "in_proj_gates")
        y_a = _gmlp_call(px, gmlp_ln_g, gmlp_ln_b, gmlp_ws, bs_full, l, gmlp_chunks)
        y_b = _na_call(px, pkv, na_pairs, l, n_lat, off_bq, (off_gate // tn_in) * tn_in, 0)
        y_c = _swa_call(px, pkv, cs_t, swa_mask, sink_col[l], rot, n_lat, n_ctx, off_cq, off_ck - off_bv)
        acc = _merge_call(y_a, y_b, y_c, pg, w_branch, l, 0, ms_small, 1024)
        xs = _mm_res_call([acc], w_out, l, xs, mods, l, 2, n_lat, ms_small, 1024)
        if l % 2 == 0:
            tn_up = 512
            h, hid0 = _norm_first_call(xs, g_ffn, mods, l, 3, n_lat, [w_ffn_gate, w_ffn_up],
                                       lambda ck: (None, ck, tn_up), lambda n, c: (i_layer, c, 0), tn_up,
                                       "swiglu", "norm_ffn_up")
            hid1 = _ffn_up_call(h, w_ffn_gate, w_ffn_up, i_layer, ms_big, tn_up, 1)
            xs = _mm_res_call([hid0, hid1], w_ffn_down, i_layer, xs, mods, l, 5, n_lat, ms_big, 1024)
        else:
            pair = 2
            h, hid0, gates = _norm_first_call(xs, g_ffn, mods, l, 3, n_lat, [w_exp_gate, w_exp_up],
                                              lambda ck: (None, pair, ck, fe), lambda n, c: (i_layer, 0, c, 0),
                                              pair * fe, "moe", "norm_moe_up", w_router=w_router_t[i_layer], fe=fe)
            hid1 = _moe_up_call(h, w_exp_gate, w_exp_up, gates, i_layer, ms_small, pair, 1)
            xs = _mm_res_call([hid0, hid1], w_exp_down2, i_layer, xs, mods, l, 5, n_lat, ms_big, 1024)
    return _final_norm_call(xs, g_final, n_lat)[None]
```

```python
import functools

import numpy as np
import jax
import jax.numpy as jnp
from jax import lax
from jax.experimental import pallas as pl
from jax.experimental.pallas import tpu as pltpu

F32 = jnp.float32
BF16 = jnp.bfloat16
FP8 = jnp.float8_e4m3fn
FP8_MAX = 448.0

GRID_W = 64
HEAD_DIM = 128
BRANCH_W = 1024
N_BRANCHES = 3
CHUNK = 128
GMLP_GROUPS = 8
NA_HEADS = BRANCH_W // HEAD_DIM
NA_WIN_R = 8
NA_WIN_C = 16
SWA_HEADS = BRANCH_W // HEAD_DIM
SWA_KV_HEADS = 2
SWA_GROUP = SWA_HEADS // SWA_KV_HEADS
SWA_KV_W = SWA_KV_HEADS * HEAD_DIM
SWA_WINDOW = 128
SWA_BLOCK = 128
ROPE_BASE = 10000.0
N_EXPERTS = 8
NEG_INF = -1e30
EPS = 1e-6

NA_QROWS = 4
NA_KROWS = 12
NA_QB = NA_QROWS * GRID_W
NA_KB = NA_KROWS * GRID_W
NA_PIECES = NA_KB // NA_QB

SWA_QB = 2 * SWA_BLOCK
SWA_PIECES = 4

LOG2E = 1.4426950408889634

LANE_TILE = 128
ROUTER_LANES = LANE_TILE

MIB = 1024 * 1024
V7X_VMEM_BYTES = 64 * MIB
COMPILER_SCRATCH_BYTES = 6 * MIB


def _params(sem, vmem_bytes):
    limit = vmem_bytes + COMPILER_SCRATCH_BYTES
    assert limit <= V7X_VMEM_BYTES, limit
    return pltpu.CompilerParams(dimension_semantics=sem, vmem_limit_bytes=limit)


def _ada_kernel(c_ref, w_ref, b_ref, o_ref):
    cs = c_ref[...]
    cs = (cs * jax.nn.sigmoid(cs)).astype(BF16)
    w = w_ref[...].astype(BF16)
    o_ref[...] = jnp.dot(cs, w, preferred_element_type=F32) + b_ref[...]


def _ada_mods(c8, w_ada, b_ada, tn=512):
    depth, d, n6 = w_ada.shape
    return pl.pallas_call(
        _ada_kernel,
        grid=(depth, n6 // tn),
        in_specs=[
            pl.BlockSpec((8, d), lambda l, j: (0, 0)),
            pl.BlockSpec((None, d, tn), lambda l, j: (l, 0, j)),
            pl.BlockSpec((None, 1, tn), lambda l, j: (l, 0, j)),
        ],
        out_specs=pl.BlockSpec((None, 8, tn), lambda l, j: (l, 0, j)),
        out_shape=jax.ShapeDtypeStruct((depth, 8, n6), F32),
        compiler_params=_params(("arbitrary", "arbitrary"), 2 * d * tn * 4 + d * tn * 2 + MIB),
        name="ada_mods",
    )(c8, w_ada, b_ada.reshape(depth, 1, n6))


def _norm_mod(x, g_ref, sh_ref, sc_ref):
    ms = jnp.mean(x * x, axis=-1, keepdims=True)
    y = x * lax.rsqrt(ms + EPS) * g_ref[...]
    return y * (1.0 + sc_ref[...]) + sh_ref[...]


def _route_top2(h, wr_ref):
    h_hi = h.astype(BF16)
    h_lo = (h - h_hi.astype(F32)).astype(BF16)
    logits = (jnp.dot(h_hi, wr_ref[0], preferred_element_type=F32)
              + jnp.dot(h_lo, wr_ref[0], preferred_element_type=F32)
              + jnp.dot(h_hi, wr_ref[1], preferred_element_type=F32))
    lane = lax.broadcasted_iota(jnp.int32, logits.shape, 1)
    logits = jnp.where(lane < N_EXPERTS, logits, -jnp.inf)
    m1 = jnp.max(logits, axis=-1, keepdims=True)
    i1 = jnp.min(jnp.where(logits == m1, lane, ROUTER_LANES), axis=-1, keepdims=True)
    first = lane == i1
    rest = jnp.where(first, -jnp.inf, logits)
    m2 = jnp.max(rest, axis=-1, keepdims=True)
    i2 = jnp.min(jnp.where(rest == m2, lane, ROUTER_LANES), axis=-1, keepdims=True)
    second = lane == i2
    e2 = jnp.exp(m2 - m1)
    w1 = 1.0 / (1.0 + e2)
    return jnp.where(first, w1, 0.0) + jnp.where(second, e2 * w1, 0.0)


def _sigmoid(x):
    return 0.5 * jnp.tanh(0.5 * x) + 0.5


def _swiglu(hg, hu):
    return hg * _sigmoid(hg) * hu


def _expert_weights(gates, first_expert, per_tile, fe):
    lane = lax.broadcasted_iota(jnp.int32, gates.shape, 1)
    out_lane = lax.broadcasted_iota(jnp.int32, (1, per_tile * fe), 1)
    ge = None
    for e in range(per_tile):
        w_e = jnp.sum(jnp.where(lane == first_expert + e, gates, 0.0), axis=-1, keepdims=True)
        ge = w_e if ge is None else jnp.where(out_lane >= e * fe, w_e, ge)
    return ge


def _cast_chunks(w_refs, wb_refs, slot, rows):
    for w_ref, wb_ref in zip(w_refs, wb_refs):
        if len(w_ref.shape) == 3:
            fe = w_ref.shape[2]
            for e in range(w_ref.shape[0]):
                wb_ref[slot, rows, e * fe:(e + 1) * fe] = w_ref[e].astype(BF16)
        else:
            wb_ref[slot, rows, :] = w_ref[...].astype(BF16)


def _norm_first_kernel(*refs, p_steps, ck, n_w, kind, fe):
    x_ref, g_ref, sh_ref, sc_ref = refs[:4]
    w_refs = refs[4:4 + n_w]
    pos = 4 + n_w
    moe = kind == "moe"
    wr_ref = refs[pos] if moe else None
    pos += int(moe)
    h_ref, o_ref = refs[pos:pos + 2]
    pos += 2
    gate_ref = refs[pos] if moe else None
    pos += int(moe)
    quant = kind == "plain8"
    h8_ref, hs_ref = refs[pos:pos + 2] if quant else (None, None)
    wb_refs = refs[pos + 2 * int(quant):]
    kind = "plain" if quant else kind
    s = pl.program_id(0)

    @pl.when(s < p_steps)
    def _():
        _cast_chunks(w_refs, wb_refs, 0, pl.ds(pl.multiple_of(s * ck, ck), ck))

    @pl.when(s >= p_steps)
    def _():
        hf = _norm_mod(x_ref[...], g_ref, sh_ref, sc_ref)
        h = hf.astype(BF16)
        h_ref[...] = h
        if quant:
            amax = jnp.max(jnp.abs(hf), axis=-1, keepdims=True)
            nonzero = amax > 0.0
            h8_ref[...] = (hf * jnp.where(nonzero, FP8_MAX / amax, 1.0)).astype(FP8)
            hs_ref[...] = jnp.broadcast_to(jnp.where(nonzero, amax * (1.0 / FP8_MAX), 1.0), hs_ref.shape)
        prods = [jnp.dot(h, wb_ref[0], preferred_element_type=F32) for wb_ref in wb_refs]
        if kind == "plain":
            out = prods[0] if len(prods) == 1 else jnp.concatenate(prods, axis=1)
        else:
            out = _swiglu(prods[0], prods[1])
        if moe:
            gates = _route_top2(hf, wr_ref)
            gate_ref[...] = gates
            out = out * _expert_weights(gates, 0, o_ref.shape[1] // fe, fe)
        o_ref[...] = out.astype(o_ref.dtype)


def _final_norm_kernel(x_ref, g_ref, o_ref):
    x = x_ref[...]
    ms = jnp.mean(x * x, axis=-1, keepdims=True)
    o_ref[...] = x * lax.rsqrt(ms + EPS) * g_ref[...]


NORM_ROWS = 256
NORM_STAGE_STEPS = 8


def _norm_first_call(x, g, mods, l, which, n_lat, weights, w_block, w_index, tn, kind, name, w_router=None, fe=0):
    out_cols = tn * len(weights) if kind in ("plain", "plain8") else tn
    m, d = x.shape
    tr, p_steps = NORM_ROWS, NORM_STAGE_STEPS
    ck = d // p_steps
    assert m % tr == 0 and n_lat % tr == 0 and d % p_steps == 0
    lat_tiles = n_lat // tr
    tile = lambda s: jnp.maximum(s - p_steps, 0)
    chunk = lambda s: jnp.minimum(s, p_steps - 1)
    row_kind = lambda s: jnp.where(tile(s) >= lat_tiles, 1, 0)
    mod_spec = lambda k: pl.BlockSpec((None, None, None, 1, d), lambda s: (l, row_kind(s), k, 0, 0))
    in_specs = [pl.BlockSpec((tr, d), lambda s: (tile(s), 0)),
                pl.BlockSpec((None, 1, d), lambda s: (l, 0, 0)),
                mod_spec(which), mod_spec(which + 1)]
    in_specs += [pl.BlockSpec(w_block(ck), functools.partial(lambda n, s: w_index(n, chunk(s)), n))
                 for n in range(len(weights))]
    args = [x, g.reshape(g.shape[0], 1, d), mods, mods, *weights]
    out_specs = [pl.BlockSpec((tr, d), lambda s: (tile(s), 0)),
                 pl.BlockSpec((tr, out_cols), lambda s: (tile(s), 0))]
    out_shape = [jax.ShapeDtypeStruct((m, d), BF16), jax.ShapeDtypeStruct((m, out_cols), BF16)]
    vmem = (2 * (tr * d * 4 + tr * d * 2 + tr * out_cols * 2 + len(weights) * ck * tn * 4)
            + len(weights) * d * tn * 2 + 3 * tr * d * 4 + 3 * tr * out_cols * 4)
    if kind == "moe":
        in_specs.append(pl.BlockSpec((2, d, ROUTER_LANES), lambda s: (0, 0, 0)))
        args.append(w_router)
        out_specs.append(pl.BlockSpec((tr, ROUTER_LANES), lambda s: (tile(s), 0)))
        out_shape.append(jax.ShapeDtypeStruct((m, ROUTER_LANES), F32))
        vmem += 2 * 2 * d * ROUTER_LANES * 2 + 2 * tr * d * 4
    if kind == "plain8":
        out_specs += [pl.BlockSpec((tr, d), lambda s: (tile(s), 0)),
                      pl.BlockSpec((tr, LANE_TILE), lambda s: (tile(s), 0))]
        out_shape += [jax.ShapeDtypeStruct((m, d), FP8), jax.ShapeDtypeStruct((m, LANE_TILE), F32)]
        vmem += 2 * (tr * d + tr * LANE_TILE * 4)
    return pl.pallas_call(
        functools.partial(_norm_first_kernel, p_steps=p_steps, ck=ck, n_w=len(weights), kind=kind, fe=fe),
        grid=(p_steps + m // tr,),
        in_specs=in_specs,
        out_specs=out_specs,
        out_shape=out_shape,
        scratch_shapes=[pltpu.VMEM((1, d, tn), BF16) for _ in weights],
        compiler_params=_params(("arbitrary",), vmem),
        name=name,
    )(*args)


def _final_norm_call(x, g, n_lat, tr=256):
    d = x.shape[1]
    return pl.pallas_call(
        _final_norm_kernel,
        grid=(n_lat // tr,),
        in_specs=[pl.BlockSpec((tr, d), lambda i: (i, 0)),
                  pl.BlockSpec((1, d), lambda i: (0, 0))],
        out_specs=pl.BlockSpec((tr, d), lambda i: (i, 0)),
        out_shape=jax.ShapeDtypeStruct((n_lat, d), F32),
        compiler_params=_params(("arbitrary",), 6 * tr * d * 4),
        name="final_norm",
    )(x, g.reshape(1, d))


class _Tiling:
    def __init__(self, m, k, n, ms, tn):
        assert m % ms == 0 and k % ms == 0 and n % tn == 0
        self.ms, self.tn, self.nt = ms, tn, n // tn
        self.tm, self.ck = m // ms, k // ms
        assert self.tm % 16 == 0 and self.ck % 16 == 0
        self.grid = (self.nt + 1, ms)

    def row(self, j, i):
        return jnp.where(j == 0, 0, i)

    def col(self, j):
        return jnp.maximum(j - 1, 0)

    def wrow(self, j, i):
        return jnp.where(j == self.nt, self.ms - 1, i)

    def wcol(self, j):
        return jnp.minimum(j, self.nt - 1)


def _stage_weight(t, w_refs, wb_refs):
    j, i = pl.program_id(0), pl.program_id(1)

    @pl.when(j < t.nt)
    def _():
        _cast_chunks(w_refs, wb_refs, j % 2, pl.ds(pl.multiple_of(i * t.ck, t.ck), t.ck))


def _mm_kernel(a_ref, w_ref, o_ref, wb_ref, *, t):
    j = pl.program_id(0)
    _stage_weight(t, (w_ref,), (wb_ref,))

    @pl.when(j > 0)
    def _():
        o_ref[...] = jnp.dot(a_ref[...], wb_ref[(j + 1) % 2], preferred_element_type=F32).astype(o_ref.dtype)


def _mm_call(a, w, l, w_tile, n_tiles, ms, tn, name):
    m, k = a.shape
    t = _Tiling(m, k, n_tiles * tn, ms, tn)
    vmem = 2 * (t.tm * k * 2 + t.ck * tn * 4 + t.tm * tn * 2) + 2 * k * tn * 2 + t.tm * tn * 4
    return pl.pallas_call(
        functools.partial(_mm_kernel, t=t),
        grid=t.grid,
        in_specs=[pl.BlockSpec((t.tm, k), lambda j, i: (t.row(j, i), 0)),
                  pl.BlockSpec((None, t.ck, tn), lambda j, i: (l, t.wrow(j, i), w_tile(t.wcol(j))))],
        out_specs=pl.BlockSpec((t.tm, tn), lambda j, i: (t.row(j, i), t.col(j))),
        out_shape=jax.ShapeDtypeStruct((m, n_tiles * tn), BF16),
        scratch_shapes=[pltpu.VMEM((2, k, tn), BF16)],
        compiler_params=_params(("arbitrary", "arbitrary"), vmem),
        name=name,
    )(a, w)


def _mm8_kernel(a_ref, hs_ref, wa_ref, wc_ref, o_ref, wb_ref, amax_ref, *, nt, ck):
    j, i = pl.program_id(0), pl.program_id(1)

    def scan():
        slot = j % 3
        so_far = jnp.where(i == 0, 0.0, amax_ref[slot])
        amax_ref[slot] = jnp.maximum(so_far, jnp.max(jnp.abs(wa_ref[...]), axis=0, keepdims=True))

    def cast():
        amax = amax_ref[(j + 2) % 3]
        inv = jnp.where(amax > 0.0, FP8_MAX / amax, 1.0)
        rows = pl.ds(pl.multiple_of(i * ck, ck), ck)
        wb_ref[(j + 1) % 2, rows, :] = (wc_ref[...] * inv).astype(FP8)

    def multiply():
        amax = amax_ref[(j + 1) % 3]
        scale = jnp.where(amax > 0.0, amax * (1.0 / FP8_MAX), 1.0)
        y = jnp.dot(a_ref[...], wb_ref[j % 2], preferred_element_type=F32)
        o_ref[...] = (y * hs_ref[:, :1] * scale).astype(o_ref.dtype)

    @pl.when((j == 0) & (i == 0))
    def _():
        amax_ref[...] = jnp.zeros(amax_ref.shape, F32)

    @pl.when(j < 2)
    def _():
        scan()
        pl.when(j == 1)(cast)

    @pl.when(j >= 2)
    def _():
        multiply()
        scan()
        cast()


def _mm8_call(a8, a_scale, w, l, col0, ncols, ms, tn, name):
    m, k = a8.shape
    assert m % ms == 0 and k % ms == 0 and ncols % tn == 0 and col0 % tn == 0
    nt, tm, ck = ncols // tn, m // ms, k // ms
    assert tm % 32 == 0 and ck % 32 == 0 and nt >= 2
    lanes = a_scale.shape[1]
    row = lambda j, i: jnp.where(j < 2, 0, i)
    col = lambda j: jnp.maximum(j - 2, 0)
    scan = lambda j, i: (l, jnp.where(j >= nt, ms - 1, i), col0 // tn + jnp.minimum(j, nt - 1))
    cast = lambda j, i: (l, jnp.where(j == 0, 0, jnp.where(j > nt, ms - 1, i)), col0 // tn + jnp.clip(j - 1, 0, nt - 1))
    vmem = 2 * (tm * k + 2 * ck * tn * 4 + tm * tn * 2 + tm * lanes * 4) + 2 * k * tn + 2 * tm * tn * 4
    return pl.pallas_call(
        functools.partial(_mm8_kernel, nt=nt, ck=ck),
        grid=(nt + 2, ms),
        in_specs=[pl.BlockSpec((tm, k), lambda j, i: (row(j, i), 0)),
                  pl.BlockSpec((tm, lanes), lambda j, i: (row(j, i), 0)),
                  pl.BlockSpec((None, ck, tn), scan),
                  pl.BlockSpec((None, ck, tn), cast)],
        out_specs=pl.BlockSpec((tm, tn), lambda j, i: (row(j, i), col(j))),
        out_shape=jax.ShapeDtypeStruct((m, ncols), BF16),
        scratch_shapes=[pltpu.VMEM((2, k, tn), FP8), pltpu.VMEM((3, 1, tn), F32)],
        compiler_params=_params(("arbitrary", "arbitrary"), vmem),
        name=name,
    )(a8, a_scale, w, w)


def _mm_res_kernel(*refs, t, n_lat, ks):
    a_refs = refs[:len(ks)]
    w_ref, x_ref, gx_ref, gc_ref, o_ref, wb_ref = refs[len(ks):]
    j, i = pl.program_id(0), pl.program_id(1)
    _stage_weight(t, (w_ref,), (wb_ref,))

    @pl.when(j > 0)
    def _():
        slot = (j + 1) % 2
        y, k0 = None, 0
        for a_ref, k in zip(a_refs, ks):
            part = jnp.dot(a_ref[...], wb_ref[slot, k0:k0 + k, :], preferred_element_type=F32)
            y = part if y is None else y + part
            k0 += k
        row = i * t.tm + lax.broadcasted_iota(jnp.int32, (t.tm, 1), 0)
        gate = jnp.where(row < n_lat, gx_ref[...], gc_ref[...])
        o_ref[...] = x_ref[...] + gate * y


def _mm_res_call(a_parts, w, l, x, mods, lm, which, n_lat, ms, tn):
    m = a_parts[0].shape[0]
    ks = tuple(a.shape[1] for a in a_parts)
    k = sum(ks)
    d = w.shape[2]
    assert w.shape[1] == k
    t = _Tiling(m, k, d, ms, tn)
    gate_spec = lambda kind: pl.BlockSpec((None, None, None, 1, tn),
                                          lambda j, i: (lm, kind, which, 0, t.col(j)))
    xo_spec = pl.BlockSpec((t.tm, tn), lambda j, i: (t.row(j, i), t.col(j)))
    vmem = 2 * (t.tm * k * 2 + t.ck * tn * 4 + 2 * t.tm * tn * 4) + 2 * k * tn * 2 + 2 * t.tm * tn * 4
    return pl.pallas_call(
        functools.partial(_mm_res_kernel, t=t, n_lat=n_lat, ks=ks),
        grid=t.grid,
        in_specs=[pl.BlockSpec((t.tm, kp), lambda j, i: (t.row(j, i), 0)) for kp in ks]
                 + [pl.BlockSpec((None, t.ck, tn), lambda j, i: (l, t.wrow(j, i), t.wcol(j))),
                    xo_spec, gate_spec(0), gate_spec(1)],
        out_specs=xo_spec,
        out_shape=jax.ShapeDtypeStruct((m, d), F32),
        scratch_shapes=[pltpu.VMEM((2, k, tn), BF16)],
        compiler_params=_params(("arbitrary", "arbitrary"), vmem),
        name="proj_residual",
    )(*a_parts, w, x, mods, mods)


def _merge_kernel(ya_ref, yb_ref, yc_ref, ga_ref, gb_ref, gc_ref, w_ref, o_ref, wb_ref, *, t, bw):
    j = pl.program_id(0)
    _stage_weight(t, (w_ref,), (wb_ref,))

    @pl.when(j > 0)
    def _():
        slot = (j + 1) % 2
        acc = None
        for br, (y_ref, g_ref) in enumerate(((ya_ref, ga_ref), (yb_ref, gb_ref), (yc_ref, gc_ref))):
            part = jnp.dot(y_ref[...], wb_ref[slot, br * bw:(br + 1) * bw, :], preferred_element_type=F32)
            part = _sigmoid(g_ref[...].astype(F32)) * part
            acc = part if acc is None else acc + part
        o_ref[...] = acc.astype(o_ref.dtype)


def _merge_call(ya, yb, yc, px, w_branch, l, off_gate, ms, tn):
    m, bw = ya.shape
    d = w_branch.shape[3]
    k = N_BRANCHES * bw
    t = _Tiling(m, k, d, ms, tn)
    y_spec = pl.BlockSpec((t.tm, bw), lambda j, i: (t.row(j, i), 0))
    g_spec = lambda br: pl.BlockSpec((t.tm, tn), lambda j, i: (t.row(j, i), (off_gate + br * d) // tn + t.col(j)))
    vmem = (2 * (3 * t.tm * bw * 2 + 3 * t.tm * tn * 2 + t.ck * tn * 4 + t.tm * tn * 2) + 2 * k * tn * 2
            + 2 * t.tm * tn * 4)
    return pl.pallas_call(
        functools.partial(_merge_kernel, t=t, bw=bw),
        grid=t.grid,
        in_specs=[y_spec, y_spec, y_spec, g_spec(0), g_spec(1), g_spec(2),
                  pl.BlockSpec((None, t.ck, tn), lambda j, i: (l, t.wrow(j, i), t.wcol(j)))],
        out_specs=pl.BlockSpec((t.tm, tn), lambda j, i: (t.row(j, i), t.col(j))),
        out_shape=jax.ShapeDtypeStruct((m, d), BF16),
        scratch_shapes=[pltpu.VMEM((2, k, tn), BF16)],
        compiler_params=_params(("arbitrary", "arbitrary"), vmem),
        name="branch_merge",
    )(ya, yb, yc, px, px, px, w_branch.reshape(w_branch.shape[0], k, d))


def _ffn_up_kernel(a_ref, wg_ref, wu_ref, o_ref, wgb_ref, wub_ref, *, t):
    j = pl.program_id(0)
    _stage_weight(t, (wg_ref, wu_ref), (wgb_ref, wub_ref))

    @pl.when(j > 0)
    def _():
        slot = (j + 1) % 2
        a = a_ref[...]
        hg = jnp.dot(a, wgb_ref[slot], preferred_element_type=F32)
        hu = jnp.dot(a, wub_ref[slot], preferred_element_type=F32)
        o_ref[...] = _swiglu(hg, hu).astype(o_ref.dtype)


def _moe_up_kernel(a_ref, wg_ref, wu_ref, gate_ref, o_ref, wgb_ref, wub_ref, *, t, fe, tile0):
    j = pl.program_id(0)
    _stage_weight(t, (wg_ref, wu_ref), (wgb_ref, wub_ref))

    @pl.when(j > 0)
    def _():
        slot = (j + 1) % 2
        a = a_ref[...]
        hg = jnp.dot(a, wgb_ref[slot], preferred_element_type=F32)
        hu = jnp.dot(a, wub_ref[slot], preferred_element_type=F32)
        per_tile = t.tn // fe
        ge = _expert_weights(gate_ref[...], (tile0 + j - 1) * per_tile, per_tile, fe)
        o_ref[...] = (_swiglu(hg, hu) * ge).astype(o_ref.dtype)


def _ffn_up_call(a, wg, wu, i_layer, ms, tn, tile0):
    m, k = a.shape
    f = wg.shape[2] - tile0 * tn
    t = _Tiling(m, k, f, ms, tn)
    w_spec = pl.BlockSpec((None, t.ck, tn), lambda j, i: (i_layer, t.wrow(j, i), tile0 + t.wcol(j)))
    vmem = 2 * (t.tm * k * 2 + 2 * t.ck * tn * 4 + t.tm * tn * 2) + 4 * k * tn * 2 + 3 * t.tm * tn * 4
    return pl.pallas_call(
        functools.partial(_ffn_up_kernel, t=t),
        grid=t.grid,
        in_specs=[pl.BlockSpec((t.tm, k), lambda j, i: (t.row(j, i), 0)), w_spec, w_spec],
        out_specs=pl.BlockSpec((t.tm, tn), lambda j, i: (t.row(j, i), t.col(j))),
        out_shape=jax.ShapeDtypeStruct((m, f), BF16),
        scratch_shapes=[pltpu.VMEM((2, k, tn), BF16), pltpu.VMEM((2, k, tn), BF16)],
        compiler_params=_params(("arbitrary", "arbitrary"), vmem),
        name="ffn_up",
    )(a, wg, wu)


def _moe_up_call(a, wg, wu, gates, i_layer, ms, per_tile, tile0):
    m, k = a.shape
    n_exp, fe = wg.shape[1], wg.shape[3]
    tn = per_tile * fe
    t = _Tiling(m, k, n_exp * fe - tile0 * tn, ms, tn)
    w_spec = pl.BlockSpec((None, per_tile, t.ck, fe),
                          lambda j, i: (i_layer, tile0 + t.wcol(j), t.wrow(j, i), 0))
    vmem = (2 * (t.tm * k * 2 + 2 * t.ck * tn * 4 + t.tm * tn * 2 + t.tm * ROUTER_LANES * 4) + 4 * k * tn * 2
            + 3 * t.tm * tn * 4)
    return pl.pallas_call(
        functools.partial(_moe_up_kernel, t=t, fe=fe, tile0=tile0),
        grid=t.grid,
        in_specs=[pl.BlockSpec((t.tm, k), lambda j, i: (t.row(j, i), 0)), w_spec, w_spec,
                  pl.BlockSpec((t.tm, ROUTER_LANES), lambda j, i: (t.row(j, i), 0))],
        out_specs=pl.BlockSpec((t.tm, tn), lambda j, i: (t.row(j, i), t.col(j))),
        out_shape=jax.ShapeDtypeStruct((m, t.nt * tn), BF16),
        scratch_shapes=[pltpu.VMEM((2, k, tn), BF16), pltpu.VMEM((2, k, tn), BF16)],
        compiler_params=_params(("arbitrary", "arbitrary"), vmem),
        name="moe_up",
    )(a, wg, wu, gates)


def _gmlp_kernel(u_ref, v_ref, lng_ref, lnb_ref, ws_ref, bs_ref, o_ref, *, chunks):
    u = jax.nn.gelu(u_ref[...].astype(F32))
    v = jax.nn.gelu(v_ref[...].astype(F32))
    mu = jnp.mean(v, axis=-1, keepdims=True)
    var = jnp.mean(jnp.square(v - mu), axis=-1, keepdims=True)
    vn = ((v - mu) * lax.rsqrt(var + EPS) * lng_ref[...] + lnb_ref[...]).astype(BF16)
    for g in range(GMLP_GROUPS):
        cols = slice(g * CHUNK, (g + 1) * CHUNK)
        wsg = ws_ref[g].astype(BF16)
        for c in range(chunks):
            rows = slice(c * CHUNK, (c + 1) * CHUNK)
            s = jnp.dot(wsg, vn[rows, cols], preferred_element_type=F32) + bs_ref[:, cols]
            o_ref[rows, cols] = (u[rows, cols] * s).astype(o_ref.dtype)


def _gmlp_call(px, lng, lnb, ws, bs_full, l, chunks):
    m = px.shape[0]
    t = chunks * CHUNK
    return pl.pallas_call(
        functools.partial(_gmlp_kernel, chunks=chunks),
        grid=(m // t,),
        in_specs=[pl.BlockSpec((t, BRANCH_W), lambda i: (i, 0)),
                  pl.BlockSpec((t, BRANCH_W), lambda i: (i, 1)),
                  pl.BlockSpec((None, 1, BRANCH_W), lambda i: (l, 0, 0)),
                  pl.BlockSpec((None, 1, BRANCH_W), lambda i: (l, 0, 0)),
                  pl.BlockSpec((None, GMLP_GROUPS, CHUNK, CHUNK), lambda i: (l, 0, 0, 0)),
                  pl.BlockSpec((None, CHUNK, BRANCH_W), lambda i: (l, 0, 0))],
        out_specs=pl.BlockSpec((t, BRANCH_W), lambda i: (i, 0)),
        out_shape=jax.ShapeDtypeStruct((m, BRANCH_W), BF16),
        compiler_params=_params(("arbitrary",), 6 * t * BRANCH_W * 2 + 6 * t * BRANCH_W * 4 + 2 * MIB),
        name="gmlp",
    )(px, px, lng.reshape(-1, 1, BRANCH_W), lnb.reshape(-1, 1, BRANCH_W), ws, bs_full)


def _dot_nt(a, b):
    return lax.dot_general(a, b, (((1,), (1,)), ((), ())), preferred_element_type=F32)


def _tree(op, xs):
    while len(xs) > 1:
        xs = [op(xs[i], xs[i + 1]) if i + 1 < len(xs) else xs[i] for i in range(0, len(xs), 2)]
    return xs[0]


def _with_ones(v):
    return jnp.concatenate([v, jnp.ones_like(v)], axis=1)


def _softmax_pv(s, v1_tiles, extra_logit=None):
    width = min(t.shape[1] for t in s)
    parts = [t[:, i:i + width] for t in s for i in range(0, t.shape[1], width)]
    m = jnp.max(_tree(jnp.maximum, parts), axis=-1, keepdims=True)
    if extra_logit is not None:
        m = jnp.maximum(m, extra_logit)
    o = _tree(jnp.add, [jnp.dot(jnp.exp2(t - m).astype(BF16), v1, preferred_element_type=F32)
                        for t, v1 in zip(s, v1_tiles)])
    hd = o.shape[1] // 2
    denom = o[:, hd:]
    if extra_logit is not None:
        denom = denom + jnp.exp2(extra_logit - m)
    return o[:, :hd] / denom


def _na_kernel(q_ref, k0_ref, k1_ref, k2_ref, v0_ref, v1_ref, v2_ref, kc_ref, vc_ref, pair_ref, mask_ref,
               o_ref, bias_ref, *, nb, roff):
    b = pl.program_id(0)
    left = lax.broadcasted_iota(jnp.int32, (GRID_W, 2 * GRID_W), 1) < GRID_W

    def build(kind):
        for h in range(NA_HEADS):
            for qr in range(NA_QROWS):
                rows = slice(qr * GRID_W, (qr + 1) * GRID_W)
                for p in range(NA_KROWS // 2):
                    cols = slice(p * 2 * GRID_W, (p + 1) * 2 * GRID_W)
                    pair = jnp.where(left, pair_ref[h, roff[kind][qr][2 * p]], pair_ref[h, roff[kind][qr][2 * p + 1]])
                    bias_ref[h, rows, cols] = pair * LOG2E + mask_ref[rows, cols]

    for kind, first_block in enumerate((0, 1, nb - 1)):
        pl.when(b == first_block)(functools.partial(build, kind))

    @pl.when(b == nb)
    def _():
        for h in range(NA_HEADS):
            bias_ref[h] = mask_ref[...]

    k_refs = (k0_ref, k1_ref, k2_ref)
    v_refs = (v0_ref, v1_ref, v2_ref)
    for h in range(NA_HEADS):
        cols = slice(h * HEAD_DIM, (h + 1) * HEAD_DIM)
        q = (q_ref[:, cols].astype(F32) * (HEAD_DIM ** -0.5 * LOG2E)).astype(BF16)
        s = [_dot_nt(q, k_refs[j][:, cols]) + bias_ref[h, :, j * NA_QB:(j + 1) * NA_QB] for j in range(NA_PIECES)]
        s.append(_dot_nt(q, kc_ref[:, cols]))
        v1 = [_with_ones(v_refs[j][:, cols]) for j in range(NA_PIECES)] + [_with_ones(vc_ref[:, cols])]
        o_ref[:, cols] = _softmax_pv(s, v1).astype(o_ref.dtype)


def _na_call(px, pv, pairs, l, n_lat, off_q, off_k, off_v):
    m = px.shape[0]
    nb = n_lat // NA_QB
    assert nb >= 3
    w = BRANCH_W
    roff, mask = _na_geometry(n_lat // GRID_W)
    kstart = lambda b: jnp.clip(b - 1, 0, nb - NA_PIECES)
    k_spec = lambda off, j: pl.BlockSpec((NA_QB, w), lambda b: (kstart(b) + j, off // w))
    kind = lambda b: jnp.where(b == 0, 0, jnp.where(b == nb - 1, 2, jnp.where(b == nb, 3, 1)))
    n_rel = 2 * NA_WIN_R - 1
    vmem = (2 * (9 * NA_QB * w * 2 + NA_HEADS * n_rel * GRID_W * 2 * GRID_W * 4 + NA_QB * NA_KB * 4 + NA_QB * w * 2)
            + NA_HEADS * NA_QB * NA_KB * 4 + 16 * NA_QB * NA_QB * 4)
    return pl.pallas_call(
        functools.partial(_na_kernel, nb=nb, roff=roff),
        grid=(m // NA_QB,),
        in_specs=[pl.BlockSpec((NA_QB, w), lambda b: (b, off_q // w)),
                  k_spec(off_k, 0), k_spec(off_k, 1), k_spec(off_k, 2),
                  k_spec(off_v, 0), k_spec(off_v, 1), k_spec(off_v, 2),
                  pl.BlockSpec((NA_QB, w), lambda b: (nb, off_k // w)),
                  pl.BlockSpec((NA_QB, w), lambda b: (nb, off_v // w)),
                  pl.BlockSpec((None, NA_HEADS, n_rel, GRID_W, 2 * GRID_W), lambda b: (l, 0, 0, 0, 0)),
                  pl.BlockSpec((None, NA_QB, NA_KB), lambda b: (kind(b), 0, 0))],
        out_specs=pl.BlockSpec((NA_QB, w), lambda b: (b, 0)),
        out_shape=jax.ShapeDtypeStruct((m, w), BF16),
        scratch_shapes=[pltpu.VMEM((NA_HEADS, NA_QB, NA_KB), F32)],
        compiler_params=_params(("arbitrary",), vmem),
        name="neighbourhood_attention",
    )(px, px, px, px, pv, pv, pv, px, pv, pairs, mask)


def _na_geometry(rows):
    col = np.arange(GRID_W)
    c0 = np.clip(col - NA_WIN_C // 2, 0, GRID_W - NA_WIN_C)
    in_win = (col[None, :] >= c0[:, None]) & (col[None, :] < c0[:, None] + NA_WIN_C)
    roffs, masks = [], []
    for r0, s0 in ((0, 0), (NA_QROWS, 0), (rows - NA_QROWS, rows - NA_KROWS)):
        r = r0 + np.arange(NA_QROWS)
        start = np.clip(r - NA_WIN_R // 2, 0, rows - NA_WIN_R)
        key_row = s0 + np.arange(NA_KROWS)
        valid_r = (key_row[None, :] >= start[:, None]) & (key_row[None, :] < start[:, None] + NA_WIN_R)
        roff = np.clip(key_row[None, :] - r[:, None] + (NA_WIN_R - 1), 0, 2 * NA_WIN_R - 2)
        valid = valid_r[:, None, :, None] & in_win[None, :, None, :]
        roffs.append(tuple(tuple(int(v) for v in row) for row in roff))
        masks.append(np.where(valid, 0.0, NEG_INF).reshape(NA_QB, NA_KB))
    masks.append(np.full((NA_QB, NA_KB), NEG_INF))
    return tuple(roffs), jnp.asarray(np.stack(masks), F32)


def _na_bias_pairs(rpb):
    qcol = np.arange(GRID_W)
    kcol = np.arange(2 * GRID_W) % GRID_W
    coff = np.clip(kcol[None, :] - qcol[:, None] + (NA_WIN_C - 1), 0, 2 * NA_WIN_C - 2)
    pick = (coff[:, :, None] == np.arange(2 * NA_WIN_C - 1)).astype(np.float32)
    return jnp.einsum('lhij,cdj->lhicd', rpb.astype(F32), pick, precision=lax.Precision.HIGHEST)


def _swa_kernel(q_ref, csq_ref, kv0_ref, kv1_ref, kv2_ref, kv3_ref, cs0_ref, cs1_ref, cs2_ref, cs3_ref,
                kvc_ref, mask_ref, sink_ref, rot_ref, o_ref):
    rot = rot_ref[...]
    hd = HEAD_DIM

    def rope(x, cs):
        swapped = jnp.dot(x, rot, preferred_element_type=F32)
        return x.astype(F32) * cs[:, :hd] + swapped * cs[:, hd:]

    csq = csq_ref[...]
    kv_refs = (kv0_ref, kv1_ref, kv2_ref, kv3_ref)
    cs_refs = (cs0_ref, cs1_ref, cs2_ref, cs3_ref)
    for kv in range(SWA_KV_HEADS):
        kcols = slice(kv * hd, (kv + 1) * hd)
        vcols = slice(SWA_KV_W + kv * hd, SWA_KV_W + (kv + 1) * hd)
        k_tiles = [rope(kv_refs[j][:, kcols], cs_refs[j][...]).astype(BF16) for j in range(SWA_PIECES)]
        k_tiles.append(kvc_ref[:, kcols])
        v1 = [_with_ones(kv_refs[j][:, vcols]) for j in range(SWA_PIECES)] + [_with_ones(kvc_ref[:, vcols])]
        for h in range(kv * SWA_GROUP, (kv + 1) * SWA_GROUP):
            cols = slice(h * hd, (h + 1) * hd)
            q = (rope(q_ref[:, cols], csq) * (hd ** -0.5 * LOG2E)).astype(BF16)
            s = [_dot_nt(q, k_tiles[j]) + mask_ref[:, j * SWA_BLOCK:(j + 1) * SWA_BLOCK] for j in range(SWA_PIECES)]
            s.append(_dot_nt(q, k_tiles[SWA_PIECES]))
            o_ref[:, cols] = _softmax_pv(s, v1, extra_logit=sink_ref[h] * LOG2E).astype(o_ref.dtype)


def _swa_call(px, pkv, cs_t, mask, sink_col, rot, n_lat, n_ctx, off_q, off_kv):
    kvb = off_kv // (2 * SWA_KV_W)
    assert off_kv % (2 * SWA_KV_W) == 0
    m = px.shape[0]
    blk, w = SWA_BLOCK, BRANCH_W
    nbl = n_lat // blk
    nbq = n_lat // SWA_QB
    assert n_ctx == SWA_QB and nbq >= 2
    piece = lambda b, j: jnp.clip(2 * b - 1 + j, 0, nbl - 1)
    kind = lambda b: jnp.where(b == 0, 0, jnp.where(b == nbq - 1, 2, jnp.where(b == nbq, 3, 1)))
    kv_spec = lambda j: pl.BlockSpec((blk, 2 * SWA_KV_W), lambda b: (piece(b, j), kvb))
    cs_spec = lambda j: pl.BlockSpec((blk, 2 * HEAD_DIM), lambda b: (piece(b, j), 0))
    vmem = (2 * (2 * SWA_QB * w * 2 + SWA_QB * SWA_PIECES * blk * 4 + SWA_HEADS * SWA_QB * HEAD_DIM * 4)
            + 16 * SWA_QB * (SWA_PIECES * blk + n_ctx) * 4)
    return pl.pallas_call(
        _swa_kernel,
        grid=(m // SWA_QB,),
        in_specs=[pl.BlockSpec((SWA_QB, w), lambda b: (b, off_q // w)),
                  pl.BlockSpec((SWA_QB, 2 * HEAD_DIM), lambda b: (b, 0)),
                  kv_spec(0), kv_spec(1), kv_spec(2), kv_spec(3),
                  cs_spec(0), cs_spec(1), cs_spec(2), cs_spec(3),
                  pl.BlockSpec((n_ctx, 2 * SWA_KV_W), lambda b: (n_lat // n_ctx, kvb)),
                  pl.BlockSpec((None, SWA_QB, SWA_PIECES * blk), lambda b: (kind(b), 0, 0)),
                  pl.BlockSpec((SWA_HEADS, SWA_QB, 1), lambda b: (0, 0, 0)),
                  pl.BlockSpec((HEAD_DIM, HEAD_DIM), lambda b: (0, 0))],
        out_specs=pl.BlockSpec((SWA_QB, w), lambda b: (b, 0)),
        out_shape=jax.ShapeDtypeStruct((m, w), BF16),
        compiler_params=_params(("arbitrary",), vmem),
        name="window_gqa",
    )(px, cs_t, pkv, pkv, pkv, pkv, cs_t, cs_t, cs_t, cs_t, pkv, mask, sink_col, rot)


def _rope_tables(n_lat, n_ctx):
    t = jnp.arange(n_lat, dtype=jnp.int32)
    half = HEAD_DIM // 2
    inv = ROPE_BASE ** (-jnp.arange(0, half, 2, dtype=F32) / half)
    parts_c, parts_s = [], []
    for pos in (t // GRID_W, t % GRID_W):
        ang = pos.astype(F32)[:, None] * inv[None, :]
        parts_c += [jnp.cos(ang), jnp.cos(ang)]
        parts_s += [jnp.sin(ang), jnp.sin(ang)]
    cos_t = jnp.concatenate(parts_c, axis=-1)
    sin_t = jnp.concatenate(parts_s, axis=-1)
    cos_t = jnp.concatenate([cos_t, jnp.ones((n_ctx, HEAD_DIM), F32)], axis=0)
    sin_t = jnp.concatenate([sin_t, jnp.zeros((n_ctx, HEAD_DIM), F32)], axis=0)
    quarter = half // 2
    rot = np.zeros((HEAD_DIM, HEAD_DIM), np.float32)
    for lane in range(HEAD_DIM):
        if lane % half < quarter:
            rot[lane + quarter, lane] = -1.0
        else:
            rot[lane - quarter, lane] = 1.0
    return jnp.concatenate([cos_t, sin_t], axis=1), jnp.asarray(rot, BF16)


def _swa_mask():
    a = np.arange(SWA_QB)
    j = np.arange(SWA_PIECES * SWA_BLOCK)
    in_band = np.abs((j[None, :] - SWA_BLOCK) - a[:, None]) <= SWA_WINDOW
    last = (SWA_PIECES - 1) * SWA_BLOCK
    piece_ok = {0: j >= SWA_BLOCK, 1: j >= 0, 2: j < last, 3: j < 0}
    tabs = [np.where(in_band & piece_ok[kind][None, :], 0.0, NEG_INF) for kind in range(4)]
    return jnp.asarray(np.stack(tabs), F32)


def kernel(x, c, ctx, c_ctx, w_ada, b_ada, g_mix, w_in, gmlp_ln_g, gmlp_ln_b, gmlp_ws, gmlp_bs, na_rpb,
           swa_sink, w_branch, w_out, g_ffn, w_ffn_gate, w_ffn_up, w_ffn_down, w_router, w_exp_gate,
           w_exp_up, w_exp_down, g_final):
    batch, n_lat, d = x.shape
    n_ctx = ctx.shape[1]
    depth = w_ada.shape[0]
    assert batch == 1 and c.shape[0] == 1
    assert n_lat % NA_QB == 0 and n_ctx == NA_QB and n_lat // GRID_W >= NA_KROWS
    m = n_lat + n_ctx

    off_au, off_av = 0, BRANCH_W
    off_bq, off_cq = 2 * BRANCH_W, 3 * BRANCH_W
    off_gate = 4 * BRANCH_W
    off_bk = off_gate + N_BRANCHES * d
    off_bv = off_bk + BRANCH_W
    off_ck = off_bv + BRANCH_W
    off_cv = off_ck + SWA_KV_W
    assert w_in.shape[2] == off_cv + SWA_KV_W and off_au == 0 and off_av == BRANCH_W

    ms_big, ms_small = 8, 16
    gmlp_chunks = 6 if m % (6 * CHUNK) == 0 else 2

    xs = jnp.concatenate([x[0], ctx[0]], axis=0)
    c8 = jnp.concatenate([c, c_ctx[None, :], jnp.zeros((6, d), F32)], axis=0)
    mods = _ada_mods(c8, w_ada, b_ada)[:, :2].reshape(depth, 2, 6, 1, d)

    cs_t, rot = _rope_tables(n_lat, n_ctx)
    swa_mask = _swa_mask()
    na_pairs = _na_bias_pairs(na_rpb)
    bs_full = jnp.repeat(jnp.swapaxes(gmlp_bs, 1, 2), CHUNK, axis=2)
    sink_col = jnp.broadcast_to(swa_sink[:, :, None, None], (depth, SWA_HEADS, SWA_QB, 1))
    w_router_p = jnp.pad(w_router, ((0, 0), (0, 0), (0, ROUTER_LANES - N_EXPERTS)))
    w_router_hi = w_router_p.astype(BF16)
    w_router_lo = (w_router_p - w_router_hi.astype(F32)).astype(BF16)
    w_router_t = jnp.stack([w_router_hi, w_router_lo], axis=1)
    n_moe, n_exp, _, fe = w_exp_gate.shape
    w_exp_down2 = w_exp_down.reshape(n_moe, n_exp * fe, d)

    for l in range(depth):
        i_layer = l // 2
        kvw = 2 * SWA_KV_W
        tail = [w_in] * ((w_in.shape[2] - off_bv) // kvw)
        h, pkv, h8, h8_scale = _norm_first_call(xs, g_mix, mods, l, 0, n_lat, tail, lambda ck: (None, ck, kvw),
                                                lambda n, c: (l, c, off_bv // kvw + n), kvw, "plain8",
                                                "norm_in_proj_tail")
        tn_in = 1024
        puv = _mm_call(h, w_in, l, lambda c: c, off_bq // tn_in, ms_big, tn_in, "in_proj")
        p8 = _mm8_call(h8, h8_scale, w_in, l, off_bq, off_bv - off_bq, ms_big, tn_in, "in_proj_fp8")
        y_a = _gmlp_call(puv, gmlp_ln_g, gmlp_ln_b, gmlp_ws, bs_full, l, gmlp_chunks)
        y_b = _na_call(p8, pkv, na_pairs, l, n_lat, 0, off_bk - off_bq, 0)
        y_c = _swa_call(p8, pkv, cs_t, swa_mask, sink_col[l], rot, n_lat, n_ctx, off_cq - off_bq, off_ck - off_bv)
        acc = _merge_call(y_a, y_b, y_c, p8, w_branch, l, off_gate - off_bq, ms_small, 1024)
        xs = _mm_res_call([acc], w_out, l, xs, mods, l, 2, n_lat, ms_small, 1024)
        if l % 2 == 0:
            tn_up = 512
            h, hid0 = _norm_first_call(xs, g_ffn, mods, l, 3, n_lat, [w_ffn_gate, w_ffn_up],
                                       lambda ck: (None, ck, tn_up), lambda n, c: (i_layer, c, 0), tn_up,
                                       "swiglu", "norm_ffn_up")
            hid1 = _ffn_up_call(h, w_ffn_gate, w_ffn_up, i_layer, ms_big, tn_up, 1)
            xs = _mm_res_call([hid0, hid1], w_ffn_down, i_layer, xs, mods, l, 5, n_lat, ms_big, 1024)
        else:
            pair = 2
            h, hid0, gates = _norm_first_call(xs, g_ffn, mods, l, 3, n_lat, [w_exp_gate, w_exp_up],
                                              lambda ck: (None, pair, ck, fe), lambda n, c: (i_layer, 0, c, 0),
                                              pair * fe, "moe", "norm_moe_up", w_router=w_router_t[i_layer], fe=fe)
            hid1 = _moe_up_call(h, w_exp_gate, w_exp_up, gates, i_layer, ms_small, pair, 1)
            xs = _mm_res_call([hid0, hid1], w_exp_down2, i_layer, xs, mods, l, 5, n_lat, ms_big, 1024)
    return _final_norm_call(xs, g_final, n_lat)[None]
```

```python
import functools

import numpy as np
import jax
import jax.numpy as jnp
from jax import lax
from jax.experimental import pallas as pl
from jax.experimental.pallas import tpu as pltpu

F32 = jnp.float32
BF16 = jnp.bfloat16
FP8 = jnp.float8_e4m3fn
FP8_MAX = 448.0

GRID_W = 64
HEAD_DIM = 128
BRANCH_W = 1024
N_BRANCHES = 3
CHUNK = 128
GMLP_GROUPS = 8
NA_HEADS = BRANCH_W // HEAD_DIM
NA_WIN_R = 8
NA_WIN_C = 16
SWA_HEADS = BRANCH_W // HEAD_DIM
SWA_KV_HEADS = 2
SWA_GROUP = SWA_HEADS // SWA_KV_HEADS
SWA_KV_W = SWA_KV_HEADS * HEAD_DIM
SWA_WINDOW = 128
SWA_BLOCK = 128
ROPE_BASE = 10000.0
N_EXPERTS = 8
NEG_INF = -1e30
EPS = 1e-6

NA_QROWS = 4
NA_KROWS = 12
NA_QB = NA_QROWS * GRID_W
NA_KB = NA_KROWS * GRID_W
NA_PIECES = NA_KB // NA_QB

SWA_QB = 2 * SWA_BLOCK
SWA_PIECES = 4

LOG2E = 1.4426950408889634

LANE_TILE = 128
ROUTER_LANES = LANE_TILE

MIB = 1024 * 1024
V7X_VMEM_BYTES = 64 * MIB
COMPILER_SCRATCH_BYTES = 6 * MIB


def _params(sem, vmem_bytes):
    limit = vmem_bytes + COMPILER_SCRATCH_BYTES
    assert limit <= V7X_VMEM_BYTES, limit
    return pltpu.CompilerParams(dimension_semantics=sem, vmem_limit_bytes=limit)


def _ada_kernel(c_ref, w_ref, b_ref, o_ref):
    cs = c_ref[...]
    cs = (cs * jax.nn.sigmoid(cs)).astype(BF16)
    w = w_ref[...].astype(BF16)
    o_ref[...] = jnp.dot(cs, w, preferred_element_type=F32) + b_ref[...]


def _ada_mods(c8, w_ada, b_ada, tn=512):
    depth, d, n6 = w_ada.shape
    return pl.pallas_call(
        _ada_kernel,
        grid=(depth, n6 // tn),
        in_specs=[
            pl.BlockSpec((8, d), lambda l, j: (0, 0)),
            pl.BlockSpec((None, d, tn), lambda l, j: (l, 0, j)),
            pl.BlockSpec((None, 1, tn), lambda l, j: (l, 0, j)),
        ],
        out_specs=pl.BlockSpec((None, 8, tn), lambda l, j: (l, 0, j)),
        out_shape=jax.ShapeDtypeStruct((depth, 8, n6), F32),
        compiler_params=_params(("arbitrary", "arbitrary"), 2 * d * tn * 4 + d * tn * 2 + MIB),
        name="ada_mods",
    )(c8, w_ada, b_ada.reshape(depth, 1, n6))


def _norm_mod(x, g_ref, sh_ref, sc_ref):
    ms = jnp.mean(x * x, axis=-1, keepdims=True)
    y = x * lax.rsqrt(ms + EPS) * g_ref[...]
    return y * (1.0 + sc_ref[...]) + sh_ref[...]


def _route_top2(h, wr_ref):
    h_hi = h.astype(BF16)
    h_lo = (h - h_hi.astype(F32)).astype(BF16)
    logits = (jnp.dot(h_hi, wr_ref[0], preferred_element_type=F32)
              + jnp.dot(h_lo, wr_ref[0], preferred_element_type=F32)
              + jnp.dot(h_hi, wr_ref[1], preferred_element_type=F32))
    lane = lax.broadcasted_iota(jnp.int32, logits.shape, 1)
    logits = jnp.where(lane < N_EXPERTS, logits, -jnp.inf)
    m1 = jnp.max(logits, axis=-1, keepdims=True)
    i1 = jnp.min(jnp.where(logits == m1, lane, ROUTER_LANES), axis=-1, keepdims=True)
    first = lane == i1
    rest = jnp.where(first, -jnp.inf, logits)
    m2 = jnp.max(rest, axis=-1, keepdims=True)
    i2 = jnp.min(jnp.where(rest == m2, lane, ROUTER_LANES), axis=-1, keepdims=True)
    second = lane == i2
    e2 = jnp.exp(m2 - m1)
    w1 = 1.0 / (1.0 + e2)
    return jnp.where(first, w1, 0.0) + jnp.where(second, e2 * w1, 0.0)


def _swiglu(hg, hu):
    return hg * jax.nn.sigmoid(hg) * hu


def _expert_weights(gates, first_expert, per_tile, fe):
    lane = lax.broadcasted_iota(jnp.int32, gates.shape, 1)
    out_lane = lax.broadcasted_iota(jnp.int32, (1, per_tile * fe), 1)
    ge = None
    for e in range(per_tile):
        w_e = jnp.sum(jnp.where(lane == first_expert + e, gates, 0.0), axis=-1, keepdims=True)
        ge = w_e if ge is None else jnp.where(out_lane >= e * fe, w_e, ge)
    return ge


def _cast_chunks(w_refs, wb_refs, slot, rows):
    for w_ref, wb_ref in zip(w_refs, wb_refs):
        if len(w_ref.shape) == 3:
            fe = w_ref.shape[2]
            for e in range(w_ref.shape[0]):
                wb_ref[slot, rows, e * fe:(e + 1) * fe] = w_ref[e].astype(BF16)
        else:
            wb_ref[slot, rows, :] = w_ref[...].astype(BF16)


def _norm_first_kernel(*refs, p_steps, ck, n_w, kind, fe, lat_tiles):
    x_ref, g_ref, sh_ref, sc_ref = refs[:4]
    w_refs = refs[4:4 + n_w]
    pos = 4 + n_w
    moe = kind == "moe"
    wr_ref = refs[pos] if moe else None
    pos += int(moe)
    stack = lat_tiles is not None
    ctx_ref = refs[pos] if stack else None
    pos += int(stack)
    h_ref, o_ref = refs[pos:pos + 2]
    pos += 2
    gate_ref = refs[pos] if moe else None
    pos += int(moe)
    quant = kind == "plain8"
    h8_ref, hs_ref = refs[pos:pos + 2] if quant else (None, None)
    pos += 2 * int(quant)
    xs_ref = refs[pos] if stack else None
    wb_refs = refs[pos + int(stack):]
    kind = "plain" if quant else kind
    s = pl.program_id(0)

    @pl.when(s < p_steps)
    def _():
        _cast_chunks(w_refs, wb_refs, 0, pl.ds(pl.multiple_of(s * ck, ck), ck))

    @pl.when(s >= p_steps)
    def _():
        x = x_ref[...]
        if stack:
            x = jnp.where(s - p_steps >= lat_tiles, ctx_ref[...], x)
            xs_ref[...] = x
        hf = _norm_mod(x, g_ref, sh_ref, sc_ref)
        h = hf.astype(BF16)
        h_ref[...] = h
        if quant:
            amax = jnp.max(jnp.abs(hf), axis=-1, keepdims=True)
            nonzero = amax > 0.0
            h8_ref[...] = (hf * jnp.where(nonzero, FP8_MAX / amax, 1.0)).astype(FP8)
            hs_ref[...] = jnp.broadcast_to(jnp.where(nonzero, amax * (1.0 / FP8_MAX), 1.0), hs_ref.shape)
        prods = [jnp.dot(h, wb_ref[0], preferred_element_type=F32) for wb_ref in wb_refs]
        if kind == "plain":
            out = prods[0] if len(prods) == 1 else jnp.concatenate(prods, axis=1)
        else:
            out = _swiglu(prods[0], prods[1])
        if moe:
            gates = _route_top2(hf, wr_ref)
            gate_ref[...] = gates
            out = out * _expert_weights(gates, 0, o_ref.shape[1] // fe, fe)
        o_ref[...] = out.astype(o_ref.dtype)


def _final_norm_kernel(x_ref, g_ref, o_ref):
    x = x_ref[...]
    ms = jnp.mean(x * x, axis=-1, keepdims=True)
    o_ref[...] = x * lax.rsqrt(ms + EPS) * g_ref[...]


NORM_ROWS = 256
NORM_STAGE_STEPS = 8


def _norm_first_call(x, g, mods, l, which, n_lat, weights, w_block, w_index, tn, kind, name, w_router=None, fe=0,
                     ctx=None):
    out_cols = tn * len(weights) if kind in ("plain", "plain8") else tn
    d = x.shape[1]
    m = x.shape[0] + (0 if ctx is None else ctx.shape[0])
    tr, p_steps = NORM_ROWS, NORM_STAGE_STEPS
    ck = d // p_steps
    assert m % tr == 0 and n_lat % tr == 0 and d % p_steps == 0
    lat_tiles = n_lat // tr
    tile = lambda s: jnp.maximum(s - p_steps, 0)
    x_tile = tile if ctx is None else (lambda s: jnp.minimum(tile(s), lat_tiles - 1))
    chunk = lambda s: jnp.minimum(s, p_steps - 1)
    row_kind = lambda s: jnp.where(tile(s) >= lat_tiles, 1, 0)
    mod_spec = lambda k: pl.BlockSpec((None, None, None, 1, d), lambda s: (l, row_kind(s), k, 0, 0))
    in_specs = [pl.BlockSpec((tr, d), lambda s: (x_tile(s), 0)),
                pl.BlockSpec((None, 1, d), lambda s: (l, 0, 0)),
                mod_spec(which), mod_spec(which + 1)]
    in_specs += [pl.BlockSpec(w_block(ck), functools.partial(lambda n, s: w_index(n, chunk(s)), n))
                 for n in range(len(weights))]
    args = [x, g.reshape(g.shape[0], 1, d), mods, mods, *weights]
    out_specs = [pl.BlockSpec((tr, d), lambda s: (tile(s), 0)),
                 pl.BlockSpec((tr, out_cols), lambda s: (tile(s), 0))]
    out_shape = [jax.ShapeDtypeStruct((m, d), BF16), jax.ShapeDtypeStruct((m, out_cols), BF16)]
    vmem = (2 * (tr * d * 4 + tr * d * 2 + tr * out_cols * 2 + len(weights) * ck * tn * 4)
            + len(weights) * d * tn * 2 + 3 * tr * d * 4 + 3 * tr * out_cols * 4)
    if kind == "moe":
        in_specs.append(pl.BlockSpec((2, d, ROUTER_LANES), lambda s: (0, 0, 0)))
        args.append(w_router)
        out_specs.append(pl.BlockSpec((tr, ROUTER_LANES), lambda s: (tile(s), 0)))
        out_shape.append(jax.ShapeDtypeStruct((m, ROUTER_LANES), F32))
        vmem += 2 * 2 * d * ROUTER_LANES * 2 + 2 * tr * d * 4
    if ctx is not None:
        ctx_tiles = ctx.shape[0] // tr
        in_specs.append(pl.BlockSpec((tr, d), lambda s: (jnp.clip(tile(s) - lat_tiles, 0, ctx_tiles - 1), 0)))
        args.append(ctx)
    if kind == "plain8":
        out_specs += [pl.BlockSpec((tr, d), lambda s: (tile(s), 0)),
                      pl.BlockSpec((tr, LANE_TILE), lambda s: (tile(s), 0))]
        out_shape += [jax.ShapeDtypeStruct((m, d), FP8), jax.ShapeDtypeStruct((m, LANE_TILE), F32)]
        vmem += 2 * (tr * d + tr * LANE_TILE * 4)
    if ctx is not None:
        out_specs.append(pl.BlockSpec((tr, d), lambda s: (tile(s), 0)))
        out_shape.append(jax.ShapeDtypeStruct((m, d), F32))
        vmem += 2 * 2 * tr * d * 4 - 2 * (tr * d * 4 + tr * out_cols * 4)
    return pl.pallas_call(
        functools.partial(_norm_first_kernel, p_steps=p_steps, ck=ck, n_w=len(weights), kind=kind, fe=fe,
                          lat_tiles=None if ctx is None else lat_tiles),
        grid=(p_steps + m // tr,),
        in_specs=in_specs,
        out_specs=out_specs,
        out_shape=out_shape,
        scratch_shapes=[pltpu.VMEM((1, d, tn), BF16) for _ in weights],
        compiler_params=_params(("arbitrary",), vmem),
        name=name,
    )(*args)


def _final_norm_call(x, g, n_lat, tr=256):
    d = x.shape[1]
    return pl.pallas_call(
        _final_norm_kernel,
        grid=(n_lat // tr,),
        in_specs=[pl.BlockSpec((tr, d), lambda i: (i, 0)),
                  pl.BlockSpec((1, d), lambda i: (0, 0))],
        out_specs=pl.BlockSpec((tr, d), lambda i: (i, 0)),
        out_shape=jax.ShapeDtypeStruct((n_lat, d), F32),
        compiler_params=_params(("arbitrary",), 6 * tr * d * 4),
        name="final_norm",
    )(x, g.reshape(1, d))


class _Tiling:
    def __init__(self, m, k, n, ms, tn):
        assert m % ms == 0 and k % ms == 0 and n % tn == 0
        self.ms, self.tn, self.nt = ms, tn, n // tn
        self.tm, self.ck = m // ms, k // ms
        assert self.tm % 16 == 0 and self.ck % 16 == 0
        self.grid = (self.nt + 1, ms)

    def row(self, j, i):
        return jnp.where(j == 0, 0, i)

    def col(self, j):
        return jnp.maximum(j - 1, 0)

    def wrow(self, j, i):
        return jnp.where(j == self.nt, self.ms - 1, i)

    def wcol(self, j):
        return jnp.minimum(j, self.nt - 1)


def _stage_weight(t, w_refs, wb_refs):
    j, i = pl.program_id(0), pl.program_id(1)

    @pl.when(j < t.nt)
    def _():
        _cast_chunks(w_refs, wb_refs, j % 2, pl.ds(pl.multiple_of(i * t.ck, t.ck), t.ck))


def _mm_kernel(a_ref, w_ref, o_ref, wb_ref, *, t):
    j = pl.program_id(0)
    _stage_weight(t, (w_ref,), (wb_ref,))

    @pl.when(j > 0)
    def _():
        o_ref[...] = jnp.dot(a_ref[...], wb_ref[(j + 1) % 2], preferred_element_type=F32).astype(o_ref.dtype)


def _mm_call(a, w, l, w_tile, n_tiles, ms, tn, name):
    m, k = a.shape
    t = _Tiling(m, k, n_tiles * tn, ms, tn)
    vmem = 2 * (t.tm * k * 2 + t.ck * tn * 4 + t.tm * tn * 2) + 2 * k * tn * 2 + t.tm * tn * 4
    return pl.pallas_call(
        functools.partial(_mm_kernel, t=t),
        grid=t.grid,
        in_specs=[pl.BlockSpec((t.tm, k), lambda j, i: (t.row(j, i), 0)),
                  pl.BlockSpec((None, t.ck, tn), lambda j, i: (l, t.wrow(j, i), w_tile(t.wcol(j))))],
        out_specs=pl.BlockSpec((t.tm, tn), lambda j, i: (t.row(j, i), t.col(j))),
        out_shape=jax.ShapeDtypeStruct((m, n_tiles * tn), BF16),
        scratch_shapes=[pltpu.VMEM((2, k, tn), BF16)],
        compiler_params=_params(("arbitrary", "arbitrary"), vmem),
        name=name,
    )(a, w)


def _mm8_kernel(a_ref, hs_ref, wa_ref, wc_ref, o_ref, wb_ref, amax_ref, *, nt, ck):
    j, i = pl.program_id(0), pl.program_id(1)

    @pl.when(j < nt)
    def _():
        slot = j % 3

        @pl.when(i == 0)
        def _():
            amax_ref[slot] = jnp.zeros(amax_ref.shape[1:], F32)

        amax_ref[slot] = jnp.maximum(amax_ref[slot], jnp.max(jnp.abs(wa_ref[...]), axis=0, keepdims=True))

    @pl.when((j >= 1) & (j <= nt))
    def _():
        amax = amax_ref[(j + 2) % 3]
        inv = jnp.where(amax > 0.0, FP8_MAX / amax, 1.0)
        rows = pl.ds(pl.multiple_of(i * ck, ck), ck)
        wb_ref[(j + 1) % 2, rows, :] = (wc_ref[...] * inv).astype(FP8)

    @pl.when(j >= 2)
    def _():
        amax = amax_ref[(j + 1) % 3]
        scale = jnp.where(amax > 0.0, amax * (1.0 / FP8_MAX), 1.0)
        y = jnp.dot(a_ref[...], wb_ref[j % 2], preferred_element_type=F32)
        o_ref[...] = (y * hs_ref[:, :1] * scale).astype(o_ref.dtype)


def _mm8_call(a8, a_scale, w, l, col0, ncols, ms, tn, name):
    m, k = a8.shape
    assert m % ms == 0 and k % ms == 0 and ncols % tn == 0 and col0 % tn == 0
    nt, tm, ck = ncols // tn, m // ms, k // ms
    assert tm % 32 == 0 and ck % 32 == 0
    lanes = a_scale.shape[1]
    row = lambda j, i: jnp.where(j < 2, 0, i)
    col = lambda j: jnp.maximum(j - 2, 0)
    scan = lambda j, i: (l, jnp.where(j >= nt, ms - 1, i), col0 // tn + jnp.minimum(j, nt - 1))
    cast = lambda j, i: (l, jnp.where(j == 0, 0, jnp.where(j > nt, ms - 1, i)), col0 // tn + jnp.clip(j - 1, 0, nt - 1))
    vmem = 2 * (tm * k + 2 * ck * tn * 4 + tm * tn * 2 + tm * lanes * 4) + 2 * k * tn + 2 * tm * tn * 4
    return pl.pallas_call(
        functools.partial(_mm8_kernel, nt=nt, ck=ck),
        grid=(nt + 2, ms),
        in_specs=[pl.BlockSpec((tm, k), lambda j, i: (row(j, i), 0)),
                  pl.BlockSpec((tm, lanes), lambda j, i: (row(j, i), 0)),
                  pl.BlockSpec((None, ck, tn), scan),
                  pl.BlockSpec((None, ck, tn), cast)],
        out_specs=pl.BlockSpec((tm, tn), lambda j, i: (row(j, i), col(j))),
        out_shape=jax.ShapeDtypeStruct((m, ncols), BF16),
        scratch_shapes=[pltpu.VMEM((2, k, tn), FP8), pltpu.VMEM((3, 1, tn), F32)],
        compiler_params=_params(("arbitrary", "arbitrary"), vmem),
        name=name,
    )(a8, a_scale, w, w)


def _mm_res_kernel(*refs, t, n_lat, ks):
    a_refs = refs[:len(ks)]
    w_ref, x_ref, gx_ref, gc_ref, o_ref, wb_ref = refs[len(ks):]
    j, i = pl.program_id(0), pl.program_id(1)
    _stage_weight(t, (w_ref,), (wb_ref,))

    @pl.when(j > 0)
    def _():
        slot = (j + 1) % 2
        y, k0 = None, 0
        for a_ref, k in zip(a_refs, ks):
            part = jnp.dot(a_ref[...], wb_ref[slot, k0:k0 + k, :], preferred_element_type=F32)
            y = part if y is None else y + part
            k0 += k
        row = i * t.tm + lax.broadcasted_iota(jnp.int32, (t.tm, 1), 0)
        gate = jnp.where(row < n_lat, gx_ref[...], gc_ref[...])
        o_ref[...] = x_ref[...] + gate * y


def _mm_res_call(a_parts, w, l, x, mods, lm, which, n_lat, ms, tn):
    m = a_parts[0].shape[0]
    ks = tuple(a.shape[1] for a in a_parts)
    k = sum(ks)
    d = w.shape[2]
    assert w.shape[1] == k
    t = _Tiling(m, k, d, ms, tn)
    gate_spec = lambda kind: pl.BlockSpec((None, None, None, 1, tn),
                                          lambda j, i: (lm, kind, which, 0, t.col(j)))
    xo_spec = pl.BlockSpec((t.tm, tn), lambda j, i: (t.row(j, i), t.col(j)))
    vmem = 2 * (t.tm * k * 2 + t.ck * tn * 4 + 2 * t.tm * tn * 4) + 2 * k * tn * 2 + 2 * t.tm * tn * 4
    return pl.pallas_call(
        functools.partial(_mm_res_kernel, t=t, n_lat=n_lat, ks=ks),
        grid=t.grid,
        in_specs=[pl.BlockSpec((t.tm, kp), lambda j, i: (t.row(j, i), 0)) for kp in ks]
                 + [pl.BlockSpec((None, t.ck, tn), lambda j, i: (l, t.wrow(j, i), t.wcol(j))),
                    xo_spec, gate_spec(0), gate_spec(1)],
        out_specs=xo_spec,
        out_shape=jax.ShapeDtypeStruct((m, d), F32),
        scratch_shapes=[pltpu.VMEM((2, k, tn), BF16)],
        compiler_params=_params(("arbitrary", "arbitrary"), vmem),
        name="proj_residual",
    )(*a_parts, w, x, mods, mods)


def _merge_kernel(ya_ref, yb_ref, yc_ref, ga_ref, gb_ref, gc_ref, w_ref, o_ref, wb_ref, *, t, bw):
    j = pl.program_id(0)
    _stage_weight(t, (w_ref,), (wb_ref,))

    @pl.when(j > 0)
    def _():
        slot = (j + 1) % 2
        acc = None
        for br, (y_ref, g_ref) in enumerate(((ya_ref, ga_ref), (yb_ref, gb_ref), (yc_ref, gc_ref))):
            part = jnp.dot(y_ref[...], wb_ref[slot, br * bw:(br + 1) * bw, :], preferred_element_type=F32)
            part = jax.nn.sigmoid(g_ref[...].astype(F32)) * part
            acc = part if acc is None else acc + part
        o_ref[...] = acc.astype(o_ref.dtype)


def _merge_call(ya, yb, yc, px, w_branch, l, off_gate, ms, tn):
    m, bw = ya.shape
    d = w_branch.shape[3]
    k = N_BRANCHES * bw
    t = _Tiling(m, k, d, ms, tn)
    y_spec = pl.BlockSpec((t.tm, bw), lambda j, i: (t.row(j, i), 0))
    g_spec = lambda br: pl.BlockSpec((t.tm, tn), lambda j, i: (t.row(j, i), (off_gate + br * d) // tn + t.col(j)))
    vmem = (2 * (3 * t.tm * bw * 2 + 3 * t.tm * tn * 2 + t.ck * tn * 4 + t.tm * tn * 2) + 2 * k * tn * 2
            + 2 * t.tm * tn * 4)
    return pl.pallas_call(
        functools.partial(_merge_kernel, t=t, bw=bw),
        grid=t.grid,
        in_specs=[y_spec, y_spec, y_spec, g_spec(0), g_spec(1), g_spec(2),
                  pl.BlockSpec((None, t.ck, tn), lambda j, i: (l, t.wrow(j, i), t.wcol(j)))],
        out_specs=pl.BlockSpec((t.tm, tn), lambda j, i: (t.row(j, i), t.col(j))),
        out_shape=jax.ShapeDtypeStruct((m, d), BF16),
        scratch_shapes=[pltpu.VMEM((2, k, tn), BF16)],
        compiler_params=_params(("arbitrary", "arbitrary"), vmem),
        name="branch_merge",
    )(ya, yb, yc, px, px, px, w_branch.reshape(w_branch.shape[0], k, d))


def _ffn_up_kernel(a_ref, wg_ref, wu_ref, o_ref, wgb_ref, wub_ref, *, t):
    j = pl.program_id(0)
    _stage_weight(t, (wg_ref, wu_ref), (wgb_ref, wub_ref))

    @pl.when(j > 0)
    def _():
        slot = (j + 1) % 2
        a = a_ref[...]
        hg = jnp.dot(a, wgb_ref[slot], preferred_element_type=F32)
        hu = jnp.dot(a, wub_ref[slot], preferred_element_type=F32)
        o_ref[...] = _swiglu(hg, hu).astype(o_ref.dtype)


def _moe_up_kernel(a_ref, wg_ref, wu_ref, gate_ref, o_ref, wgb_ref, wub_ref, *, t, fe, tile0):
    j = pl.program_id(0)
    _stage_weight(t, (wg_ref, wu_ref), (wgb_ref, wub_ref))

    @pl.when(j > 0)
    def _():
        slot = (j + 1) % 2
        a = a_ref[...]
        hg = jnp.dot(a, wgb_ref[slot], preferred_element_type=F32)
        hu = jnp.dot(a, wub_ref[slot], preferred_element_type=F32)
        per_tile = t.tn // fe
        ge = _expert_weights(gate_ref[...], (tile0 + j - 1) * per_tile, per_tile, fe)
        o_ref[...] = (_swiglu(hg, hu) * ge).astype(o_ref.dtype)


def _ffn_up_call(a, wg, wu, i_layer, ms, tn, tile0):
    m, k = a.shape
    f = wg.shape[2] - tile0 * tn
    t = _Tiling(m, k, f, ms, tn)
    w_spec = pl.BlockSpec((None, t.ck, tn), lambda j, i: (i_layer, t.wrow(j, i), tile0 + t.wcol(j)))
    vmem = 2 * (t.tm * k * 2 + 2 * t.ck * tn * 4 + t.tm * tn * 2) + 4 * k * tn * 2 + 3 * t.tm * tn * 4
    return pl.pallas_call(
        functools.partial(_ffn_up_kernel, t=t),
        grid=t.grid,
        in_specs=[pl.BlockSpec((t.tm, k), lambda j, i: (t.row(j, i), 0)), w_spec, w_spec],
        out_specs=pl.BlockSpec((t.tm, tn), lambda j, i: (t.row(j, i), t.col(j))),
        out_shape=jax.ShapeDtypeStruct((m, f), BF16),
        scratch_shapes=[pltpu.VMEM((2, k, tn), BF16), pltpu.VMEM((2, k, tn), BF16)],
        compiler_params=_params(("arbitrary", "arbitrary"), vmem),
        name="ffn_up",
    )(a, wg, wu)


def _moe_up_call(a, wg, wu, gates, i_layer, ms, per_tile, tile0):
    m, k = a.shape
    n_exp, fe = wg.shape[1], wg.shape[3]
    tn = per_tile * fe
    t = _Tiling(m, k, n_exp * fe - tile0 * tn, ms, tn)
    w_spec = pl.BlockSpec((None, per_tile, t.ck, fe),
                          lambda j, i: (i_layer, tile0 + t.wcol(j), t.wrow(j, i), 0))
    vmem = (2 * (t.tm * k * 2 + 2 * t.ck * tn * 4 + t.tm * tn * 2 + t.tm * ROUTER_LANES * 4) + 4 * k * tn * 2
            + 3 * t.tm * tn * 4)
    return pl.pallas_call(
        functools.partial(_moe_up_kernel, t=t, fe=fe, tile0=tile0),
        grid=t.grid,
        in_specs=[pl.BlockSpec((t.tm, k), lambda j, i: (t.row(j, i), 0)), w_spec, w_spec,
                  pl.BlockSpec((t.tm, ROUTER_LANES), lambda j, i: (t.row(j, i), 0))],
        out_specs=pl.BlockSpec((t.tm, tn), lambda j, i: (t.row(j, i), t.col(j))),
        out_shape=jax.ShapeDtypeStruct((m, t.nt * tn), BF16),
        scratch_shapes=[pltpu.VMEM((2, k, tn), BF16), pltpu.VMEM((2, k, tn), BF16)],
        compiler_params=_params(("arbitrary", "arbitrary"), vmem),
        name="moe_up",
    )(a, wg, wu, gates)


def _gmlp_kernel(u_ref, v_ref, lng_ref, lnb_ref, ws_ref, bs_ref, o_ref, *, chunks):
    u = jax.nn.gelu(u_ref[...].astype(F32))
    v = jax.nn.gelu(v_ref[...].astype(F32))
    mu = jnp.mean(v, axis=-1, keepdims=True)
    var = jnp.mean(jnp.square(v - mu), axis=-1, keepdims=True)
    vn = ((v - mu) * lax.rsqrt(var + EPS) * lng_ref[...] + lnb_ref[...]).astype(BF16)
    for g in range(GMLP_GROUPS):
        cols = slice(g * CHUNK, (g + 1) * CHUNK)
        wsg = ws_ref[g].astype(BF16)
        for c in range(chunks):
            rows = slice(c * CHUNK, (c + 1) * CHUNK)
            s = jnp.dot(wsg, vn[rows, cols], preferred_element_type=F32) + bs_ref[:, cols]
            o_ref[rows, cols] = (u[rows, cols] * s).astype(o_ref.dtype)


def _gmlp_call(px, lng, lnb, ws, bs_full, l, chunks):
    m = px.shape[0]
    t = chunks * CHUNK
    return pl.pallas_call(
        functools.partial(_gmlp_kernel, chunks=chunks),
        grid=(m // t,),
        in_specs=[pl.BlockSpec((t, BRANCH_W), lambda i: (i, 0)),
                  pl.BlockSpec((t, BRANCH_W), lambda i: (i, 1)),
                  pl.BlockSpec((None, 1, BRANCH_W), lambda i: (l, 0, 0)),
                  pl.BlockSpec((None, 1, BRANCH_W), lambda i: (l, 0, 0)),
                  pl.BlockSpec((None, GMLP_GROUPS, CHUNK, CHUNK), lambda i: (l, 0, 0, 0)),
                  pl.BlockSpec((None, CHUNK, BRANCH_W), lambda i: (l, 0, 0))],
        out_specs=pl.BlockSpec((t, BRANCH_W), lambda i: (i, 0)),
        out_shape=jax.ShapeDtypeStruct((m, BRANCH_W), BF16),
        compiler_params=_params(("arbitrary",), 6 * t * BRANCH_W * 2 + 6 * t * BRANCH_W * 4 + 2 * MIB),
        name="gmlp",
    )(px, px, lng.reshape(-1, 1, BRANCH_W), lnb.reshape(-1, 1, BRANCH_W), ws, bs_full)


def _dot_nt(a, b):
    return lax.dot_general(a, b, (((1,), (1,)), ((), ())), preferred_element_type=F32)


def _tree(op, xs):
    while len(xs) > 1:
        xs = [op(xs[i], xs[i + 1]) if i + 1 < len(xs) else xs[i] for i in range(0, len(xs), 2)]
    return xs[0]


def _with_ones(v):
    return jnp.concatenate([v, jnp.ones_like(v)], axis=1)


def _softmax_pv(s, v1_tiles, extra_logit=None):
    width = min(t.shape[1] for t in s)
    parts = [t[:, i:i + width] for t in s for i in range(0, t.shape[1], width)]
    m = jnp.max(_tree(jnp.maximum, parts), axis=-1, keepdims=True)
    if extra_logit is not None:
        m = jnp.maximum(m, extra_logit)
    o = _tree(jnp.add, [jnp.dot(jnp.exp2(t - m).astype(BF16), v1, preferred_element_type=F32)
                        for t, v1 in zip(s, v1_tiles)])
    hd = o.shape[1] // 2
    denom = o[:, hd:]
    if extra_logit is not None:
        denom = denom + jnp.exp2(extra_logit - m)
    return o[:, :hd] / denom


def _na_kernel(q_ref, k0_ref, k1_ref, k2_ref, v0_ref, v1_ref, v2_ref, kc_ref, vc_ref, pair_ref, mask_ref,
               o_ref, bias_ref, *, nb, roff):
    b = pl.program_id(0)
    left = lax.broadcasted_iota(jnp.int32, (GRID_W, 2 * GRID_W), 1) < GRID_W

    def build(kind):
        for h in range(NA_HEADS):
            for qr in range(NA_QROWS):
                rows = slice(qr * GRID_W, (qr + 1) * GRID_W)
                for p in range(NA_KROWS // 2):
                    cols = slice(p * 2 * GRID_W, (p + 1) * 2 * GRID_W)
                    pair = jnp.where(left, pair_ref[h, roff[kind][qr][2 * p]], pair_ref[h, roff[kind][qr][2 * p + 1]])
                    bias_ref[h, rows, cols] = pair * LOG2E + mask_ref[rows, cols]

    for kind, first_block in enumerate((0, 1, nb - 1)):
        pl.when(b == first_block)(functools.partial(build, kind))

    @pl.when(b == nb)
    def _():
        for h in range(NA_HEADS):
            bias_ref[h] = mask_ref[...]

    k_refs = (k0_ref, k1_ref, k2_ref)
    v_refs = (v0_ref, v1_ref, v2_ref)
    for h in range(NA_HEADS):
        cols = slice(h * HEAD_DIM, (h + 1) * HEAD_DIM)
        q = (q_ref[:, cols].astype(F32) * (HEAD_DIM ** -0.5 * LOG2E)).astype(BF16)
        s = [_dot_nt(q, k_refs[j][:, cols]) + bias_ref[h, :, j * NA_QB:(j + 1) * NA_QB] for j in range(NA_PIECES)]
        s.append(_dot_nt(q, kc_ref[:, cols]))
        v1 = [_with_ones(v_refs[j][:, cols]) for j in range(NA_PIECES)] + [_with_ones(vc_ref[:, cols])]
        o_ref[:, cols] = _softmax_pv(s, v1).astype(o_ref.dtype)


def _na_call(px, pv, pairs, l, n_lat, off_q, off_k, off_v):
    m = px.shape[0]
    nb = n_lat // NA_QB
    assert nb >= 3
    w = BRANCH_W
    roff, mask = _na_geometry(n_lat // GRID_W)
    kstart = lambda b: jnp.clip(b - 1, 0, nb - NA_PIECES)
    k_spec = lambda off, j: pl.BlockSpec((NA_QB, w), lambda b: (kstart(b) + j, off // w))
    kind = lambda b: jnp.where(b == 0, 0, jnp.where(b == nb - 1, 2, jnp.where(b == nb, 3, 1)))
    n_rel = 2 * NA_WIN_R - 1
    vmem = (2 * (9 * NA_QB * w * 2 + NA_HEADS * n_rel * GRID_W * 2 * GRID_W * 4 + NA_QB * NA_KB * 4 + NA_QB * w * 2)
            + NA_HEADS * NA_QB * NA_KB * 4 + 16 * NA_QB * NA_QB * 4)
    return pl.pallas_call(
        functools.partial(_na_kernel, nb=nb, roff=roff),
        grid=(m // NA_QB,),
        in_specs=[pl.BlockSpec((NA_QB, w), lambda b: (b, off_q // w)),
                  k_spec(off_k, 0), k_spec(off_k, 1), k_spec(off_k, 2),
                  k_spec(off_v, 0), k_spec(off_v, 1), k_spec(off_v, 2),
                  pl.BlockSpec((NA_QB, w), lambda b: (nb, off_k // w)),
                  pl.BlockSpec((NA_QB, w), lambda b: (nb, off_v // w)),
                  pl.BlockSpec((None, NA_HEADS, n_rel, GRID_W, 2 * GRID_W), lambda b: (l, 0, 0, 0, 0)),
                  pl.BlockSpec((None, NA_QB, NA_KB), lambda b: (kind(b), 0, 0))],
        out_specs=pl.BlockSpec((NA_QB, w), lambda b: (b, 0)),
        out_shape=jax.ShapeDtypeStruct((m, w), BF16),
        scratch_shapes=[pltpu.VMEM((NA_HEADS, NA_QB, NA_KB), F32)],
        compiler_params=_params(("arbitrary",), vmem),
        name="neighbourhood_attention",
    )(px, px, px, px, pv, pv, pv, px, pv, pairs, mask)


def _na_geometry(rows):
    col = np.arange(GRID_W)
    c0 = np.clip(col - NA_WIN_C // 2, 0, GRID_W - NA_WIN_C)
    in_win = (col[None, :] >= c0[:, None]) & (col[None, :] < c0[:, None] + NA_WIN_C)
    roffs, masks = [], []
    for r0, s0 in ((0, 0), (NA_QROWS, 0), (rows - NA_QROWS, rows - NA_KROWS)):
        r = r0 + np.arange(NA_QROWS)
        start = np.clip(r - NA_WIN_R // 2, 0, rows - NA_WIN_R)
        key_row = s0 + np.arange(NA_KROWS)
        valid_r = (key_row[None, :] >= start[:, None]) & (key_row[None, :] < start[:, None] + NA_WIN_R)
        roff = np.clip(key_row[None, :] - r[:, None] + (NA_WIN_R - 1), 0, 2 * NA_WIN_R - 2)
        valid = valid_r[:, None, :, None] & in_win[None, :, None, :]
        roffs.append(tuple(tuple(int(v) for v in row) for row in roff))
        masks.append(np.where(valid, 0.0, NEG_INF).reshape(NA_QB, NA_KB))
    masks.append(np.full((NA_QB, NA_KB), NEG_INF))
    return tuple(roffs), jnp.asarray(np.stack(masks), F32)


def _na_bias_pairs(rpb):
    qcol = np.arange(GRID_W)
    kcol = np.arange(2 * GRID_W) % GRID_W
    coff = np.clip(kcol[None, :] - qcol[:, None] + (NA_WIN_C - 1), 0, 2 * NA_WIN_C - 2)
    pick = (coff[:, :, None] == np.arange(2 * NA_WIN_C - 1)).astype(np.float32)
    return jnp.einsum('lhij,cdj->lhicd', rpb.astype(F32), pick, precision=lax.Precision.HIGHEST)


def _swa_kernel(q_ref, csq_ref, kv0_ref, kv1_ref, kv2_ref, kv3_ref, cs0_ref, cs1_ref, cs2_ref, cs3_ref,
                kvc_ref, mask_ref, sink_ref, rot_ref, o_ref):
    rot = rot_ref[...]
    hd = HEAD_DIM

    def rope(x, cs):
        swapped = jnp.dot(x, rot, preferred_element_type=F32)
        return x.astype(F32) * cs[:, :hd] + swapped * cs[:, hd:]

    csq = csq_ref[...]
    kv_refs = (kv0_ref, kv1_ref, kv2_ref, kv3_ref)
    cs_refs = (cs0_ref, cs1_ref, cs2_ref, cs3_ref)
    for kv in range(SWA_KV_HEADS):
        kcols = slice(kv * hd, (kv + 1) * hd)
        vcols = slice(SWA_KV_W + kv * hd, SWA_KV_W + (kv + 1) * hd)
        k_tiles = [rope(kv_refs[j][:, kcols], cs_refs[j][...]).astype(BF16) for j in range(SWA_PIECES)]
        k_tiles.append(kvc_ref[:, kcols])
        v1 = [_with_ones(kv_refs[j][:, vcols]) for j in range(SWA_PIECES)] + [_with_ones(kvc_ref[:, vcols])]
        for h in range(kv * SWA_GROUP, (kv + 1) * SWA_GROUP):
            cols = slice(h * hd, (h + 1) * hd)
            q = (rope(q_ref[:, cols], csq) * (hd ** -0.5 * LOG2E)).astype(BF16)
            s = [_dot_nt(q, k_tiles[j]) + mask_ref[:, j * SWA_BLOCK:(j + 1) * SWA_BLOCK] for j in range(SWA_PIECES)]
            s.append(_dot_nt(q, k_tiles[SWA_PIECES]))
            o_ref[:, cols] = _softmax_pv(s, v1, extra_logit=sink_ref[h] * LOG2E).astype(o_ref.dtype)


def _swa_call(px, pkv, cs_t, mask, sink_col, rot, n_lat, n_ctx, off_q, off_kv):
    kvb = off_kv // (2 * SWA_KV_W)
    assert off_kv % (2 * SWA_KV_W) == 0
    m = px.shape[0]
    blk, w = SWA_BLOCK, BRANCH_W
    nbl = n_lat // blk
    nbq = n_lat // SWA_QB
    assert n_ctx == SWA_QB and nbq >= 2
    piece = lambda b, j: jnp.clip(2 * b - 1 + j, 0, nbl - 1)
    kind = lambda b: jnp.where(b == 0, 0, jnp.where(b == nbq - 1, 2, jnp.where(b == nbq, 3, 1)))
    kv_spec = lambda j: pl.BlockSpec((blk, 2 * SWA_KV_W), lambda b: (piece(b, j), kvb))
    cs_spec = lambda j: pl.BlockSpec((blk, 2 * HEAD_DIM), lambda b: (piece(b, j), 0))
    vmem = (2 * (2 * SWA_QB * w * 2 + SWA_QB * SWA_PIECES * blk * 4 + SWA_HEADS * SWA_QB * HEAD_DIM * 4)
            + 16 * SWA_QB * (SWA_PIECES * blk + n_ctx) * 4)
    return pl.pallas_call(
        _swa_kernel,
        grid=(m // SWA_QB,),
        in_specs=[pl.BlockSpec((SWA_QB, w), lambda b: (b, off_q // w)),
                  pl.BlockSpec((SWA_QB, 2 * HEAD_DIM), lambda b: (b, 0)),
                  kv_spec(0), kv_spec(1), kv_spec(2), kv_spec(3),
                  cs_spec(0), cs_spec(1), cs_spec(2), cs_spec(3),
                  pl.BlockSpec((n_ctx, 2 * SWA_KV_W), lambda b: (n_lat // n_ctx, kvb)),
                  pl.BlockSpec((None, SWA_QB, SWA_PIECES * blk), lambda b: (kind(b), 0, 0)),
                  pl.BlockSpec((SWA_HEADS, SWA_QB, 1), lambda b: (0, 0, 0)),
                  pl.BlockSpec((HEAD_DIM, HEAD_DIM), lambda b: (0, 0))],
        out_specs=pl.BlockSpec((SWA_QB, w), lambda b: (b, 0)),
        out_shape=jax.ShapeDtypeStruct((m, w), BF16),
        compiler_params=_params(("arbitrary",), vmem),
        name="window_gqa",
    )(px, cs_t, pkv, pkv, pkv, pkv, cs_t, cs_t, cs_t, cs_t, pkv, mask, sink_col, rot)


def _rope_tables(n_lat, n_ctx):
    t = jnp.arange(n_lat, dtype=jnp.int32)
    half = HEAD_DIM // 2
    inv = ROPE_BASE ** (-jnp.arange(0, half, 2, dtype=F32) / half)
    parts_c, parts_s = [], []
    for pos in (t // GRID_W, t % GRID_W):
        ang = pos.astype(F32)[:, None] * inv[None, :]
        parts_c += [jnp.cos(ang), jnp.cos(ang)]
        parts_s += [jnp.sin(ang), jnp.sin(ang)]
    cos_t = jnp.concatenate(parts_c, axis=-1)
    sin_t = jnp.concatenate(parts_s, axis=-1)
    cos_t = jnp.concatenate([cos_t, jnp.ones((n_ctx, HEAD_DIM), F32)], axis=0)
    sin_t = jnp.concatenate([sin_t, jnp.zeros((n_ctx, HEAD_DIM), F32)], axis=0)
    quarter = half // 2
    rot = np.zeros((HEAD_DIM, HEAD_DIM), np.float32)
    for lane in range(HEAD_DIM):
        if lane % half < quarter:
            rot[lane + quarter, lane] = -1.0
        else:
            rot[lane - quarter, lane] = 1.0
    return jnp.concatenate([cos_t, sin_t], axis=1), jnp.asarray(rot, BF16)


def _swa_mask():
    a = np.arange(SWA_QB)
    j = np.arange(SWA_PIECES * SWA_BLOCK)
    in_band = np.abs((j[None, :] - SWA_BLOCK) - a[:, None]) <= SWA_WINDOW
    last = (SWA_PIECES - 1) * SWA_BLOCK
    piece_ok = {0: j >= SWA_BLOCK, 1: j >= 0, 2: j < last, 3: j < 0}
    tabs = [np.where(in_band & piece_ok[kind][None, :], 0.0, NEG_INF) for kind in range(4)]
    return jnp.asarray(np.stack(tabs), F32)


def kernel(x, c, ctx, c_ctx, w_ada, b_ada, g_mix, w_in, gmlp_ln_g, gmlp_ln_b, gmlp_ws, gmlp_bs, na_rpb,
           swa_sink, w_branch, w_out, g_ffn, w_ffn_gate, w_ffn_up, w_ffn_down, w_router, w_exp_gate,
           w_exp_up, w_exp_down, g_final):
    batch, n_lat, d = x.shape
    n_ctx = ctx.shape[1]
    depth = w_ada.shape[0]
    assert batch == 1 and c.shape[0] == 1
    assert n_lat % NA_QB == 0 and n_ctx == NA_QB and n_lat // GRID_W >= NA_KROWS
    m = n_lat + n_ctx

    off_au, off_av = 0, BRANCH_W
    off_bq, off_cq = 2 * BRANCH_W, 3 * BRANCH_W
    off_gate = 4 * BRANCH_W
    off_bk = off_gate + N_BRANCHES * d
    off_bv = off_bk + BRANCH_W
    off_ck = off_bv + BRANCH_W
    off_cv = off_ck + SWA_KV_W
    assert w_in.shape[2] == off_cv + SWA_KV_W and off_au == 0 and off_av == BRANCH_W

    ms_big, ms_small = 8, 16
    gmlp_chunks = 6 if m % (6 * CHUNK) == 0 else 2

    c8 = jnp.concatenate([c, c_ctx[None, :], jnp.zeros((6, d), F32)], axis=0)
    mods = _ada_mods(c8, w_ada, b_ada)[:, :2].reshape(depth, 2, 6, 1, d)

    cs_t, rot = _rope_tables(n_lat, n_ctx)
    swa_mask = _swa_mask()
    na_pairs = _na_bias_pairs(na_rpb)
    bs_full = jnp.repeat(jnp.swapaxes(gmlp_bs, 1, 2), CHUNK, axis=2)
    sink_col = jnp.broadcast_to(swa_sink[:, :, None, None], (depth, SWA_HEADS, SWA_QB, 1))
    w_router_p = jnp.pad(w_router, ((0, 0), (0, 0), (0, ROUTER_LANES - N_EXPERTS)))
    w_router_hi = w_router_p.astype(BF16)
    w_router_lo = (w_router_p - w_router_hi.astype(F32)).astype(BF16)
    w_router_t = jnp.stack([w_router_hi, w_router_lo], axis=1)
    n_moe, n_exp, _, fe = w_exp_gate.shape
    w_exp_down2 = w_exp_down.reshape(n_moe, n_exp * fe, d)

    for l in range(depth):
        i_layer = l // 2
        kvw = 2 * SWA_KV_W
        tail = [w_in] * ((w_in.shape[2] - off_bv) // kvw)
        tail_args = (g_mix, mods, l, 0, n_lat, tail, lambda ck: (None, ck, kvw),
                     lambda n, c: (l, c, off_bv // kvw + n), kvw, "plain8", "norm_in_proj_tail")
        if l == 0:
            h, pkv, h8, h8_scale, xs = _norm_first_call(x[0], *tail_args, ctx=ctx[0])
        else:
            h, pkv, h8, h8_scale = _norm_first_call(xs, *tail_args)
        tn_in = 1024
        puv = _mm_call(h, w_in, l, lambda c: c, off_bq // tn_in, ms_big, tn_in, "in_proj")
        p8 = _mm8_call(h8, h8_scale, w_in, l, off_bq, off_bv - off_bq, ms_big, tn_in, "in_proj_fp8")
        y_a = _gmlp_call(puv, gmlp_ln_g, gmlp_ln_b, gmlp_ws, bs_full, l, gmlp_chunks)
        y_b = _na_call(p8, pkv, na_pairs, l, n_lat, 0, off_bk - off_bq, 0)
        y_c = _swa_call(p8, pkv, cs_t, swa_mask, sink_col[l], rot, n_lat, n_ctx, off_cq - off_bq, off_ck - off_bv)
        acc = _merge_call(y_a, y_b, y_c, p8, w_branch, l, off_gate - off_bq, ms_small, 1024)
        xs = _mm_res_call([acc], w_out, l, xs, mods, l, 2, n_lat, ms_small, 1024)
        if l % 2 == 0:
            tn_up = 512
            h, hid0 = _norm_first_call(xs, g_ffn, mods, l, 3, n_lat, [w_ffn_gate, w_ffn_up],
                                       lambda ck: (None, ck, tn_up), lambda n, c: (i_layer, c, 0), tn_up,
                                       "swiglu", "norm_ffn_up")
            hid1 = _ffn_up_call(h, w_ffn_gate, w_ffn_up, i_layer, ms_big, tn_up, 1)
            xs = _mm_res_call([hid0, hid1], w_ffn_down, i_layer, xs, mods, l, 5, n_lat, ms_big, 1024)
        else:
            pair = 2
            h, hid0, gates = _norm_first_call(xs, g_ffn, mods, l, 3, n_lat, [w_exp_gate, w_exp_up],
                                              lambda ck: (None, pair, ck, fe), lambda n, c: (i_layer, 0, c, 0),
                                              pair * fe, "moe", "norm_moe_up", w_router=w_router_t[i_layer], fe=fe)
            hid1 = _moe_up_call(h, w_exp_gate, w_exp_up, gates, i_layer, ms_small, pair, 1)
            xs = _mm_res_call([hid0, hid1], w_exp_down2, i_layer, xs, mods, l, 5, n_lat, ms_big, 1024)
    return _final_norm_call(xs, g_final, n_lat)[None]
```

```python
import functools

import numpy as np
import jax
import jax.numpy as jnp
from jax import lax
from jax.experimental import pallas as pl
from jax.experimental.pallas import tpu as pltpu

F32 = jnp.float32
BF16 = jnp.bfloat16
FP8 = jnp.float8_e4m3fn
FP8_MAX = 448.0

GRID_W = 64
HEAD_DIM = 128
BRANCH_W = 1024
N_BRANCHES = 3
CHUNK = 128
GMLP_GROUPS = 8
NA_HEADS = BRANCH_W // HEAD_DIM
NA_WIN_R = 8
NA_WIN_C = 16
SWA_HEADS = BRANCH_W // HEAD_DIM
SWA_KV_HEADS = 2
SWA_GROUP = SWA_HEADS // SWA_KV_HEADS
SWA_KV_W = SWA_KV_HEADS * HEAD_DIM
SWA_WINDOW = 128
SWA_BLOCK = 128
ROPE_BASE = 10000.0
N_EXPERTS = 8
NEG_INF = -1e30
EPS = 1e-6

NA_QROWS = 4
NA_KROWS = 12
NA_QB = NA_QROWS * GRID_W
NA_KB = NA_KROWS * GRID_W
NA_PIECES = NA_KB // NA_QB

SWA_QB = 2 * SWA_BLOCK
SWA_PIECES = 4

LOG2E = 1.4426950408889634

LANE_TILE = 128
ROUTER_LANES = LANE_TILE

MIB = 1024 * 1024
V7X_VMEM_BYTES = 64 * MIB
COMPILER_SCRATCH_BYTES = 6 * MIB


def _params(sem, vmem_bytes):
    limit = vmem_bytes + COMPILER_SCRATCH_BYTES
    assert limit <= V7X_VMEM_BYTES, limit
    return pltpu.CompilerParams(dimension_semantics=sem, vmem_limit_bytes=limit)


def _ada_kernel(c_ref, w_ref, b_ref, o_ref):
    cs = c_ref[...]
    cs = (cs * jax.nn.sigmoid(cs)).astype(BF16)
    w = w_ref[...].astype(BF16)
    o_ref[...] = jnp.dot(cs, w, preferred_element_type=F32) + b_ref[...]


def _ada_mods(c8, w_ada, b_ada, tn=512):
    depth, d, n6 = w_ada.shape
    return pl.pallas_call(
        _ada_kernel,
        grid=(depth, n6 // tn),
        in_specs=[
            pl.BlockSpec((8, d), lambda l, j: (0, 0)),
            pl.BlockSpec((None, d, tn), lambda l, j: (l, 0, j)),
            pl.BlockSpec((None, 1, tn), lambda l, j: (l, 0, j)),
        ],
        out_specs=pl.BlockSpec((None, 8, tn), lambda l, j: (l, 0, j)),
        out_shape=jax.ShapeDtypeStruct((depth, 8, n6), F32),
        compiler_params=_params(("arbitrary", "arbitrary"), 2 * d * tn * 4 + d * tn * 2 + MIB),
        name="ada_mods",
    )(c8, w_ada, b_ada.reshape(depth, 1, n6))


def _norm_mod(x, g_ref, sh_ref, sc_ref):
    ms = jnp.mean(x * x, axis=-1, keepdims=True)
    y = x * lax.rsqrt(ms + EPS) * g_ref[...]
    return y * (1.0 + sc_ref[...]) + sh_ref[...]


def _route_top2(h, wr_ref):
    h_hi = h.astype(BF16)
    h_lo = (h - h_hi.astype(F32)).astype(BF16)
    logits = (jnp.dot(h_hi, wr_ref[0], preferred_element_type=F32)
              + jnp.dot(h_lo, wr_ref[0], preferred_element_type=F32)
              + jnp.dot(h_hi, wr_ref[1], preferred_element_type=F32))
    lane = lax.broadcasted_iota(jnp.int32, logits.shape, 1)
    logits = jnp.where(lane < N_EXPERTS, logits, -jnp.inf)
    m1 = jnp.max(logits, axis=-1, keepdims=True)
    i1 = jnp.min(jnp.where(logits == m1, lane, ROUTER_LANES), axis=-1, keepdims=True)
    first = lane == i1
    rest = jnp.where(first, -jnp.inf, logits)
    m2 = jnp.max(rest, axis=-1, keepdims=True)
    i2 = jnp.min(jnp.where(rest == m2, lane, ROUTER_LANES), axis=-1, keepdims=True)
    second = lane == i2
    e2 = jnp.exp(m2 - m1)
    w1 = 1.0 / (1.0 + e2)
    return jnp.where(first, w1, 0.0) + jnp.where(second, e2 * w1, 0.0)


def _sigmoid(x):
    return 0.5 * jnp.tanh(0.5 * x) + 0.5


def _swiglu(hg, hu):
    return hg * _sigmoid(hg) * hu


def _expert_weights(gates, first_expert, per_tile, fe):
    lane = lax.broadcasted_iota(jnp.int32, gates.shape, 1)
    out_lane = lax.broadcasted_iota(jnp.int32, (1, per_tile * fe), 1)
    ge = None
    for e in range(per_tile):
        w_e = jnp.sum(jnp.where(lane == first_expert + e, gates, 0.0), axis=-1, keepdims=True)
        ge = w_e if ge is None else jnp.where(out_lane >= e * fe, w_e, ge)
    return ge


def _cast_chunks(w_refs, wb_refs, slot, rows):
    for w_ref, wb_ref in zip(w_refs, wb_refs):
        if len(w_ref.shape) == 3:
            fe = w_ref.shape[2]
            for e in range(w_ref.shape[0]):
                wb_ref[slot, rows, e * fe:(e + 1) * fe] = w_ref[e].astype(BF16)
        else:
            wb_ref[slot, rows, :] = w_ref[...].astype(BF16)


def _norm_first_kernel(*refs, p_steps, ck, n_w, kind, fe, lat_tiles):
    x_ref, g_ref, sh_ref, sc_ref = refs[:4]
    w_refs = refs[4:4 + n_w]
    pos = 4 + n_w
    moe = kind == "moe"
    wr_ref = refs[pos] if moe else None
    pos += int(moe)
    stack = lat_tiles is not None
    ctx_ref = refs[pos] if stack else None
    pos += int(stack)
    h_ref, o_ref = refs[pos:pos + 2]
    pos += 2
    gate_ref = refs[pos] if moe else None
    pos += int(moe)
    quant = kind == "plain8"
    h8_ref, hs_ref = refs[pos:pos + 2] if quant else (None, None)
    pos += 2 * int(quant)
    xs_ref = refs[pos] if stack else None
    wb_refs = refs[pos + int(stack):]
    kind = "plain" if quant else kind
    s = pl.program_id(0)

    @pl.when(s < p_steps)
    def _():
        _cast_chunks(w_refs, wb_refs, 0, pl.ds(pl.multiple_of(s * ck, ck), ck))

    @pl.when(s >= p_steps)
    def _():
        x = x_ref[...]
        if stack:
            x = jnp.where(s - p_steps >= lat_tiles, ctx_ref[...], x)
            xs_ref[...] = x
        hf = _norm_mod(x, g_ref, sh_ref, sc_ref)
        h = hf.astype(BF16)
        h_ref[...] = h
        if quant:
            amax = jnp.max(jnp.abs(hf), axis=-1, keepdims=True)
            nonzero = amax > 0.0
            h8_ref[...] = (hf * jnp.where(nonzero, FP8_MAX / amax, 1.0)).astype(FP8)
            hs_ref[...] = jnp.broadcast_to(jnp.where(nonzero, amax * (1.0 / FP8_MAX), 1.0), hs_ref.shape)
        prods = [jnp.dot(h, wb_ref[0], preferred_element_type=F32) for wb_ref in wb_refs]
        if kind == "plain":
            out = prods[0] if len(prods) == 1 else jnp.concatenate(prods, axis=1)
        else:
            out = _swiglu(prods[0], prods[1])
        if moe:
            gates = _route_top2(hf, wr_ref)
            gate_ref[...] = gates
            out = out * _expert_weights(gates, 0, o_ref.shape[1] // fe, fe)
        o_ref[...] = out.astype(o_ref.dtype)


def _final_norm_kernel(x_ref, g_ref, o_ref):
    x = x_ref[...]
    ms = jnp.mean(x * x, axis=-1, keepdims=True)
    o_ref[...] = x * lax.rsqrt(ms + EPS) * g_ref[...]


NORM_ROWS = 256
NORM_STAGE_STEPS = 8


def _norm_first_call(x, g, mods, l, which, n_lat, weights, w_block, w_index, tn, kind, name, w_router=None, fe=0,
                     ctx=None):
    out_cols = tn * len(weights) if kind in ("plain", "plain8") else tn
    d = x.shape[1]
    m = x.shape[0] + (0 if ctx is None else ctx.shape[0])
    tr, p_steps = NORM_ROWS, NORM_STAGE_STEPS
    ck = d // p_steps
    assert m % tr == 0 and n_lat % tr == 0 and d % p_steps == 0
    lat_tiles = n_lat // tr
    tile = lambda s: jnp.maximum(s - p_steps, 0)
    x_tile = tile if ctx is None else (lambda s: jnp.minimum(tile(s), lat_tiles - 1))
    chunk = lambda s: jnp.minimum(s, p_steps - 1)
    row_kind = lambda s: jnp.where(tile(s) >= lat_tiles, 1, 0)
    mod_spec = lambda k: pl.BlockSpec((None, None, None, 1, d), lambda s: (l, row_kind(s), k, 0, 0))
    in_specs = [pl.BlockSpec((tr, d), lambda s: (x_tile(s), 0)),
                pl.BlockSpec((None, 1, d), lambda s: (l, 0, 0)),
                mod_spec(which), mod_spec(which + 1)]
    in_specs += [pl.BlockSpec(w_block(ck), functools.partial(lambda n, s: w_index(n, chunk(s)), n))
                 for n in range(len(weights))]
    args = [x, g.reshape(g.shape[0], 1, d), mods, mods, *weights]
    out_specs = [pl.BlockSpec((tr, d), lambda s: (tile(s), 0)),
                 pl.BlockSpec((tr, out_cols), lambda s: (tile(s), 0))]
    out_shape = [jax.ShapeDtypeStruct((m, d), BF16), jax.ShapeDtypeStruct((m, out_cols), BF16)]
    vmem = (2 * (tr * d * 4 + tr * d * 2 + tr * out_cols * 2 + len(weights) * ck * tn * 4)
            + len(weights) * d * tn * 2 + 3 * tr * d * 4 + 3 * tr * out_cols * 4)
    if kind == "moe":
        in_specs.append(pl.BlockSpec((2, d, ROUTER_LANES), lambda s: (0, 0, 0)))
        args.append(w_router)
        out_specs.append(pl.BlockSpec((tr, ROUTER_LANES), lambda s: (tile(s), 0)))
        out_shape.append(jax.ShapeDtypeStruct((m, ROUTER_LANES), F32))
        vmem += 2 * 2 * d * ROUTER_LANES * 2 + 2 * tr * d * 4
    if ctx is not None:
        ctx_tiles = ctx.shape[0] // tr
        in_specs.append(pl.BlockSpec((tr, d), lambda s: (jnp.clip(tile(s) - lat_tiles, 0, ctx_tiles - 1), 0)))
        args.append(ctx)
    if kind == "plain8":
        out_specs += [pl.BlockSpec((tr, d), lambda s: (tile(s), 0)),
                      pl.BlockSpec((tr, LANE_TILE), lambda s: (tile(s), 0))]
        out_shape += [jax.ShapeDtypeStruct((m, d), FP8), jax.ShapeDtypeStruct((m, LANE_TILE), F32)]
        vmem += 2 * (tr * d + tr * LANE_TILE * 4)
    if ctx is not None:
        out_specs.append(pl.BlockSpec((tr, d), lambda s: (tile(s), 0)))
        out_shape.append(jax.ShapeDtypeStruct((m, d), F32))
        vmem += 2 * 2 * tr * d * 4 - 2 * (tr * d * 4 + tr * out_cols * 4)
    return pl.pallas_call(
        functools.partial(_norm_first_kernel, p_steps=p_steps, ck=ck, n_w=len(weights), kind=kind, fe=fe,
                          lat_tiles=None if ctx is None else lat_tiles),
        grid=(p_steps + m // tr,),
        in_specs=in_specs,
        out_specs=out_specs,
        out_shape=out_shape,
        scratch_shapes=[pltpu.VMEM((1, d, tn), BF16) for _ in weights],
        compiler_params=_params(("arbitrary",), vmem),
        name=name,
    )(*args)


def _final_norm_call(x, g, n_lat, tr=256):
    d = x.shape[1]
    return pl.pallas_call(
        _final_norm_kernel,
        grid=(n_lat // tr,),
        in_specs=[pl.BlockSpec((tr, d), lambda i: (i, 0)),
                  pl.BlockSpec((1, d), lambda i: (0, 0))],
        out_specs=pl.BlockSpec((tr, d), lambda i: (i, 0)),
        out_shape=jax.ShapeDtypeStruct((n_lat, d), F32),
        compiler_params=_params(("arbitrary",), 6 * tr * d * 4),
        name="final_norm",
    )(x, g.reshape(1, d))


class _Tiling:
    def __init__(self, m, k, n, ms, tn):
        assert m % ms == 0 and k % ms == 0 and n % tn == 0
        self.ms, self.tn, self.nt = ms, tn, n // tn
        self.tm, self.ck = m // ms, k // ms
        assert self.tm % 16 == 0 and self.ck % 16 == 0
        self.grid = (self.nt + 1, ms)

    def row(self, j, i):
        return jnp.where(j == 0, 0, i)

    def col(self, j):
        return jnp.maximum(j - 1, 0)

    def wrow(self, j, i):
        return jnp.where(j == self.nt, self.ms - 1, i)

    def wcol(self, j):
        return jnp.minimum(j, self.nt - 1)


def _stage_weight(t, w_refs, wb_refs):
    j, i = pl.program_id(0), pl.program_id(1)

    @pl.when(j < t.nt)
    def _():
        _cast_chunks(w_refs, wb_refs, j % 2, pl.ds(pl.multiple_of(i * t.ck, t.ck), t.ck))


def _mm_kernel(a_ref, w_ref, o_ref, wb_ref, *, t):
    j = pl.program_id(0)
    _stage_weight(t, (w_ref,), (wb_ref,))

    @pl.when(j > 0)
    def _():
        o_ref[...] = jnp.dot(a_ref[...], wb_ref[(j + 1) % 2], preferred_element_type=F32).astype(o_ref.dtype)


def _mm_call(a, w, l, w_tile, n_tiles, ms, tn, name):
    m, k = a.shape
    t = _Tiling(m, k, n_tiles * tn, ms, tn)
    vmem = 2 * (t.tm * k * 2 + t.ck * tn * 4 + t.tm * tn * 2) + 2 * k * tn * 2 + t.tm * tn * 4
    return pl.pallas_call(
        functools.partial(_mm_kernel, t=t),
        grid=t.grid,
        in_specs=[pl.BlockSpec((t.tm, k), lambda j, i: (t.row(j, i), 0)),
                  pl.BlockSpec((None, t.ck, tn), lambda j, i: (l, t.wrow(j, i), w_tile(t.wcol(j))))],
        out_specs=pl.BlockSpec((t.tm, tn), lambda j, i: (t.row(j, i), t.col(j))),
        out_shape=jax.ShapeDtypeStruct((m, n_tiles * tn), BF16),
        scratch_shapes=[pltpu.VMEM((2, k, tn), BF16)],
        compiler_params=_params(("arbitrary", "arbitrary"), vmem),
        name=name,
    )(a, w)


def _mm8_kernel(a_ref, hs_ref, wa_ref, wc_ref, o_ref, wb_ref, amax_ref, *, nt, ck):
    j, i = pl.program_id(0), pl.program_id(1)

    @pl.when(j < nt)
    def _():
        slot = j % 3

        @pl.when(i == 0)
        def _():
            amax_ref[slot] = jnp.zeros(amax_ref.shape[1:], F32)

        amax_ref[slot] = jnp.maximum(amax_ref[slot], jnp.max(jnp.abs(wa_ref[...]), axis=0, keepdims=True))

    @pl.when((j >= 1) & (j <= nt))
    def _():
        amax = amax_ref[(j + 2) % 3]
        inv = jnp.where(amax > 0.0, FP8_MAX / amax, 1.0)
        rows = pl.ds(pl.multiple_of(i * ck, ck), ck)
        wb_ref[(j + 1) % 2, rows, :] = (wc_ref[...] * inv).astype(FP8)

    @pl.when(j >= 2)
    def _():
        amax = amax_ref[(j + 1) % 3]
        scale = jnp.where(amax > 0.0, amax * (1.0 / FP8_MAX), 1.0)
        y = jnp.dot(a_ref[...], wb_ref[j % 2], preferred_element_type=F32)
        o_ref[...] = (y * hs_ref[:, :1] * scale).astype(o_ref.dtype)


def _mm8_call(a8, a_scale, w, l, col0, ncols, ms, tn, name):
    m, k = a8.shape
    assert m % ms == 0 and k % ms == 0 and ncols % tn == 0 and col0 % tn == 0
    nt, tm, ck = ncols // tn, m // ms, k // ms
    assert tm % 32 == 0 and ck % 32 == 0
    lanes = a_scale.shape[1]
    row = lambda j, i: jnp.where(j < 2, 0, i)
    col = lambda j: jnp.maximum(j - 2, 0)
    scan = lambda j, i: (l, jnp.where(j >= nt, ms - 1, i), col0 // tn + jnp.minimum(j, nt - 1))
    cast = lambda j, i: (l, jnp.where(j == 0, 0, jnp.where(j > nt, ms - 1, i)), col0 // tn + jnp.clip(j - 1, 0, nt - 1))
    vmem = 2 * (tm * k + 2 * ck * tn * 4 + tm * tn * 2 + tm * lanes * 4) + 2 * k * tn + 2 * tm * tn * 4
    return pl.pallas_call(
        functools.partial(_mm8_kernel, nt=nt, ck=ck),
        grid=(nt + 2, ms),
        in_specs=[pl.BlockSpec((tm, k), lambda j, i: (row(j, i), 0)),
                  pl.BlockSpec((tm, lanes), lambda j, i: (row(j, i), 0)),
                  pl.BlockSpec((None, ck, tn), scan),
                  pl.BlockSpec((None, ck, tn), cast)],
        out_specs=pl.BlockSpec((tm, tn), lambda j, i: (row(j, i), col(j))),
        out_shape=jax.ShapeDtypeStruct((m, ncols), BF16),
        scratch_shapes=[pltpu.VMEM((2, k, tn), FP8), pltpu.VMEM((3, 1, tn), F32)],
        compiler_params=_params(("arbitrary", "arbitrary"), vmem),
        name=name,
    )(a8, a_scale, w, w)


def _mm_res_kernel(*refs, t, n_lat, ks):
    a_refs = refs[:len(ks)]
    w_ref, x_ref, gx_ref, gc_ref, o_ref, wb_ref = refs[len(ks):]
    j, i = pl.program_id(0), pl.program_id(1)
    _stage_weight(t, (w_ref,), (wb_ref,))

    @pl.when(j > 0)
    def _():
        slot = (j + 1) % 2
        y, k0 = None, 0
        for a_ref, k in zip(a_refs, ks):
            part = jnp.dot(a_ref[...], wb_ref[slot, k0:k0 + k, :], preferred_element_type=F32)
            y = part if y is None else y + part
            k0 += k
        row = i * t.tm + lax.broadcasted_iota(jnp.int32, (t.tm, 1), 0)
        gate = jnp.where(row < n_lat, gx_ref[...], gc_ref[...])
        o_ref[...] = x_ref[...] + gate * y


def _mm_res_call(a_parts, w, l, x, mods, lm, which, n_lat, ms, tn):
    m = a_parts[0].shape[0]
    ks = tuple(a.shape[1] for a in a_parts)
    k = sum(ks)
    d = w.shape[2]
    assert w.shape[1] == k
    t = _Tiling(m, k, d, ms, tn)
    gate_spec = lambda kind: pl.BlockSpec((None, None, None, 1, tn),
                                          lambda j, i: (lm, kind, which, 0, t.col(j)))
    xo_spec = pl.BlockSpec((t.tm, tn), lambda j, i: (t.row(j, i), t.col(j)))
    vmem = 2 * (t.tm * k * 2 + t.ck * tn * 4 + 2 * t.tm * tn * 4) + 2 * k * tn * 2 + 2 * t.tm * tn * 4
    return pl.pallas_call(
        functools.partial(_mm_res_kernel, t=t, n_lat=n_lat, ks=ks),
        grid=t.grid,
        in_specs=[pl.BlockSpec((t.tm, kp), lambda j, i: (t.row(j, i), 0)) for kp in ks]
                 + [pl.BlockSpec((None, t.ck, tn), lambda j, i: (l, t.wrow(j, i), t.wcol(j))),
                    xo_spec, gate_spec(0), gate_spec(1)],
        out_specs=xo_spec,
        out_shape=jax.ShapeDtypeStruct((m, d), F32),
        scratch_shapes=[pltpu.VMEM((2, k, tn), BF16)],
        compiler_params=_params(("arbitrary", "arbitrary"), vmem),
        name="proj_residual",
    )(*a_parts, w, x, mods, mods)


def _merge_kernel(ya_ref, yb_ref, yc_ref, ga_ref, gb_ref, gc_ref, w_ref, o_ref, wb_ref, *, t, bw):
    j = pl.program_id(0)
    _stage_weight(t, (w_ref,), (wb_ref,))

    @pl.when(j > 0)
    def _():
        slot = (j + 1) % 2
        acc = None
        for br, (y_ref, g_ref) in enumerate(((ya_ref, ga_ref), (yb_ref, gb_ref), (yc_ref, gc_ref))):
            part = jnp.dot(y_ref[...], wb_ref[slot, br * bw:(br + 1) * bw, :], preferred_element_type=F32)
            part = _sigmoid(g_ref[...].astype(F32)) * part
            acc = part if acc is None else acc + part
        o_ref[...] = acc.astype(o_ref.dtype)


def _merge_call(ya, yb, yc, px, w_branch, l, off_gate, ms, tn):
    m, bw = ya.shape
    d = w_branch.shape[3]
    k = N_BRANCHES * bw
    t = _Tiling(m, k, d, ms, tn)
    y_spec = pl.BlockSpec((t.tm, bw), lambda j, i: (t.row(j, i), 0))
    g_spec = lambda br: pl.BlockSpec((t.tm, tn), lambda j, i: (t.row(j, i), (off_gate + br * d) // tn + t.col(j)))
    vmem = (2 * (3 * t.tm * bw * 2 + 3 * t.tm * tn * 2 + t.ck * tn * 4 + t.tm * tn * 2) + 2 * k * tn * 2
            + 2 * t.tm * tn * 4)
    return pl.pallas_call(
        functools.partial(_merge_kernel, t=t, bw=bw),
        grid=t.grid,
        in_specs=[y_spec, y_spec, y_spec, g_spec(0), g_spec(1), g_spec(2),
                  pl.BlockSpec((None, t.ck, tn), lambda j, i: (l, t.wrow(j, i), t.wcol(j)))],
        out_specs=pl.BlockSpec((t.tm, tn), lambda j, i: (t.row(j, i), t.col(j))),
        out_shape=jax.ShapeDtypeStruct((m, d), BF16),
        scratch_shapes=[pltpu.VMEM((2, k, tn), BF16)],
        compiler_params=_params(("arbitrary", "arbitrary"), vmem),
        name="branch_merge",
    )(ya, yb, yc, px, px, px, w_branch.reshape(w_branch.shape[0], k, d))


def _ffn_up_kernel(a_ref, wg_ref, wu_ref, o_ref, wgb_ref, wub_ref, *, t):
    j = pl.program_id(0)
    _stage_weight(t, (wg_ref, wu_ref), (wgb_ref, wub_ref))

    @pl.when(j > 0)
    def _():
        slot = (j + 1) % 2
        a = a_ref[...]
        hg = jnp.dot(a, wgb_ref[slot], preferred_element_type=F32)
        hu = jnp.dot(a, wub_ref[slot], preferred_element_type=F32)
        o_ref[...] = _swiglu(hg, hu).astype(o_ref.dtype)


def _moe_up_kernel(a_ref, wg_ref, wu_ref, gate_ref, o_ref, wgb_ref, wub_ref, *, t, fe, tile0):
    j = pl.program_id(0)
    _stage_weight(t, (wg_ref, wu_ref), (wgb_ref, wub_ref))

    @pl.when(j > 0)
    def _():
        slot = (j + 1) % 2
        a = a_ref[...]
        hg = jnp.dot(a, wgb_ref[slot], preferred_element_type=F32)
        hu = jnp.dot(a, wub_ref[slot], preferred_element_type=F32)
        per_tile = t.tn // fe
        ge = _expert_weights(gate_ref[...], (tile0 + j - 1) * per_tile, per_tile, fe)
        o_ref[...] = (_swiglu(hg, hu) * ge).astype(o_ref.dtype)


def _ffn_up_call(a, wg, wu, i_layer, ms, tn, tile0):
    m, k = a.shape
    f = wg.shape[2] - tile0 * tn
    t = _Tiling(m, k, f, ms, tn)
    w_spec = pl.BlockSpec((None, t.ck, tn), lambda j, i: (i_layer, t.wrow(j, i), tile0 + t.wcol(j)))
    vmem = 2 * (t.tm * k * 2 + 2 * t.ck * tn * 4 + t.tm * tn * 2) + 4 * k * tn * 2 + 3 * t.tm * tn * 4
    return pl.pallas_call(
        functools.partial(_ffn_up_kernel, t=t),
        grid=t.grid,
        in_specs=[pl.BlockSpec((t.tm, k), lambda j, i: (t.row(j, i), 0)), w_spec, w_spec],
        out_specs=pl.BlockSpec((t.tm, tn), lambda j, i: (t.row(j, i), t.col(j))),
        out_shape=jax.ShapeDtypeStruct((m, f), BF16),
        scratch_shapes=[pltpu.VMEM((2, k, tn), BF16), pltpu.VMEM((2, k, tn), BF16)],
        compiler_params=_params(("arbitrary", "arbitrary"), vmem),
        name="ffn_up",
    )(a, wg, wu)


def _moe_up_call(a, wg, wu, gates, i_layer, ms, per_tile, tile0):
    m, k = a.shape
    n_exp, fe = wg.shape[1], wg.shape[3]
    tn = per_tile * fe
    t = _Tiling(m, k, n_exp * fe - tile0 * tn, ms, tn)
    w_spec = pl.BlockSpec((None, per_tile, t.ck, fe),
                          lambda j, i: (i_layer, tile0 + t.wcol(j), t.wrow(j, i), 0))
    vmem = (2 * (t.tm * k * 2 + 2 * t.ck * tn * 4 + t.tm * tn * 2 + t.tm * ROUTER_LANES * 4) + 4 * k * tn * 2
            + 3 * t.tm * tn * 4)
    return pl.pallas_call(
        functools.partial(_moe_up_kernel, t=t, fe=fe, tile0=tile0),
        grid=t.grid,
        in_specs=[pl.BlockSpec((t.tm, k), lambda j, i: (t.row(j, i), 0)), w_spec, w_spec,
                  pl.BlockSpec((t.tm, ROUTER_LANES), lambda j, i: (t.row(j, i), 0))],
        out_specs=pl.BlockSpec((t.tm, tn), lambda j, i: (t.row(j, i), t.col(j))),
        out_shape=jax.ShapeDtypeStruct((m, t.nt * tn), BF16),
        scratch_shapes=[pltpu.VMEM((2, k, tn), BF16), pltpu.VMEM((2, k, tn), BF16)],
        compiler_params=_params(("arbitrary", "arbitrary"), vmem),
        name="moe_up",
    )(a, wg, wu, gates)


def _gmlp_kernel(u_ref, v_ref, lng_ref, lnb_ref, ws_ref, bs_ref, o_ref, *, chunks):
    u = jax.nn.gelu(u_ref[...].astype(F32))
    v = jax.nn.gelu(v_ref[...].astype(F32))
    mu = jnp.mean(v, axis=-1, keepdims=True)
    var = jnp.mean(jnp.square(v - mu), axis=-1, keepdims=True)
    vn = ((v - mu) * lax.rsqrt(var + EPS) * lng_ref[...] + lnb_ref[...]).astype(BF16)
    for g in range(GMLP_GROUPS):
        cols = slice(g * CHUNK, (g + 1) * CHUNK)
        wsg = ws_ref[g].astype(BF16)
        for c in range(chunks):
            rows = slice(c * CHUNK, (c + 1) * CHUNK)
            s = jnp.dot(wsg, vn[rows, cols], preferred_element_type=F32) + bs_ref[:, cols]
            o_ref[rows, cols] = (u[rows, cols] * s).astype(o_ref.dtype)


def _gmlp_call(px, lng, lnb, ws, bs_full, l, chunks):
    m = px.shape[0]
    t = chunks * CHUNK
    return pl.pallas_call(
        functools.partial(_gmlp_kernel, chunks=chunks),
        grid=(m // t,),
        in_specs=[pl.BlockSpec((t, BRANCH_W), lambda i: (i, 0)),
                  pl.BlockSpec((t, BRANCH_W), lambda i: (i, 1)),
                  pl.BlockSpec((None, 1, BRANCH_W), lambda i: (l, 0, 0)),
                  pl.BlockSpec((None, 1, BRANCH_W), lambda i: (l, 0, 0)),
                  pl.BlockSpec((None, GMLP_GROUPS, CHUNK, CHUNK), lambda i: (l, 0, 0, 0)),
                  pl.BlockSpec((None, CHUNK, BRANCH_W), lambda i: (l, 0, 0))],
        out_specs=pl.BlockSpec((t, BRANCH_W), lambda i: (i, 0)),
        out_shape=jax.ShapeDtypeStruct((m, BRANCH_W), BF16),
        compiler_params=_params(("arbitrary",), 6 * t * BRANCH_W * 2 + 6 * t * BRANCH_W * 4 + 2 * MIB),
        name="gmlp",
    )(px, px, lng.reshape(-1, 1, BRANCH_W), lnb.reshape(-1, 1, BRANCH_W), ws, bs_full)


def _dot_nt(a, b):
    return lax.dot_general(a, b, (((1,), (1,)), ((), ())), preferred_element_type=F32)


def _tree(op, xs):
    while len(xs) > 1:
        xs = [op(xs[i], xs[i + 1]) if i + 1 < len(xs) else xs[i] for i in range(0, len(xs), 2)]
    return xs[0]


def _with_ones(v):
    return jnp.concatenate([v, jnp.ones_like(v)], axis=1)


def _softmax_pv(s, v1_tiles, extra_logit=None):
    width = min(t.shape[1] for t in s)
    parts = [t[:, i:i + width] for t in s for i in range(0, t.shape[1], width)]
    m = jnp.max(_tree(jnp.maximum, parts), axis=-1, keepdims=True)
    if extra_logit is not None:
        m = jnp.maximum(m, extra_logit)
    o = _tree(jnp.add, [jnp.dot(jnp.exp2(t - m).astype(BF16), v1, preferred_element_type=F32)
                        for t, v1 in zip(s, v1_tiles)])
    hd = o.shape[1] // 2
    denom = o[:, hd:]
    if extra_logit is not None:
        denom = denom + jnp.exp2(extra_logit - m)
    return o[:, :hd] / denom


def _na_kernel(q_ref, k0_ref, k1_ref, k2_ref, v0_ref, v1_ref, v2_ref, kc_ref, vc_ref, pair_ref, mask_ref,
               o_ref, bias_ref, *, nb, roff):
    b = pl.program_id(0)
    left = lax.broadcasted_iota(jnp.int32, (GRID_W, 2 * GRID_W), 1) < GRID_W

    def build(kind):
        for h in range(NA_HEADS):
            for qr in range(NA_QROWS):
                rows = slice(qr * GRID_W, (qr + 1) * GRID_W)
                for p in range(NA_KROWS // 2):
                    cols = slice(p * 2 * GRID_W, (p + 1) * 2 * GRID_W)
                    pair = jnp.where(left, pair_ref[h, roff[kind][qr][2 * p]], pair_ref[h, roff[kind][qr][2 * p + 1]])
                    bias_ref[h, rows, cols] = pair * LOG2E + mask_ref[rows, cols]

    for kind, first_block in enumerate((0, 1, nb - 1)):
        pl.when(b == first_block)(functools.partial(build, kind))

    @pl.when(b == nb)
    def _():
        for h in range(NA_HEADS):
            bias_ref[h] = mask_ref[...]

    k_refs = (k0_ref, k1_ref, k2_ref)
    v_refs = (v0_ref, v1_ref, v2_ref)
    for h in range(NA_HEADS):
        cols = slice(h * HEAD_DIM, (h + 1) * HEAD_DIM)
        q = (q_ref[:, cols].astype(F32) * (HEAD_DIM ** -0.5 * LOG2E)).astype(BF16)
        s = [_dot_nt(q, k_refs[j][:, cols]) + bias_ref[h, :, j * NA_QB:(j + 1) * NA_QB] for j in range(NA_PIECES)]
        s.append(_dot_nt(q, kc_ref[:, cols]))
        v1 = [_with_ones(v_refs[j][:, cols]) for j in range(NA_PIECES)] + [_with_ones(vc_ref[:, cols])]
        o_ref[:, cols] = _softmax_pv(s, v1).astype(o_ref.dtype)


def _na_call(px, pv, pairs, l, n_lat, off_q, off_k, off_v):
    m = px.shape[0]
    nb = n_lat // NA_QB
    assert nb >= 3
    w = BRANCH_W
    roff, mask = _na_geometry(n_lat // GRID_W)
    kstart = lambda b: jnp.clip(b - 1, 0, nb - NA_PIECES)
    k_spec = lambda off, j: pl.BlockSpec((NA_QB, w), lambda b: (kstart(b) + j, off // w))
    kind = lambda b: jnp.where(b == 0, 0, jnp.where(b == nb - 1, 2, jnp.where(b == nb, 3, 1)))
    n_rel = 2 * NA_WIN_R - 1
    vmem = (2 * (9 * NA_QB * w * 2 + NA_HEADS * n_rel * GRID_W * 2 * GRID_W * 4 + NA_QB * NA_KB * 4 + NA_QB * w * 2)
            + NA_HEADS * NA_QB * NA_KB * 4 + 16 * NA_QB * NA_QB * 4)
    return pl.pallas_call(
        functools.partial(_na_kernel, nb=nb, roff=roff),
        grid=(m // NA_QB,),
        in_specs=[pl.BlockSpec((NA_QB, w), lambda b: (b, off_q // w)),
                  k_spec(off_k, 0), k_spec(off_k, 1), k_spec(off_k, 2),
                  k_spec(off_v, 0), k_spec(off_v, 1), k_spec(off_v, 2),
                  pl.BlockSpec((NA_QB, w), lambda b: (nb, off_k // w)),
                  pl.BlockSpec((NA_QB, w), lambda b: (nb, off_v // w)),
                  pl.BlockSpec((None, NA_HEADS, n_rel, GRID_W, 2 * GRID_W), lambda b: (l, 0, 0, 0, 0)),
                  pl.BlockSpec((None, NA_QB, NA_KB), lambda b: (kind(b), 0, 0))],
        out_specs=pl.BlockSpec((NA_QB, w), lambda b: (b, 0)),
        out_shape=jax.ShapeDtypeStruct((m, w), BF16),
        scratch_shapes=[pltpu.VMEM((NA_HEADS, NA_QB, NA_KB), F32)],
        compiler_params=_params(("arbitrary",), vmem),
        name="neighbourhood_attention",
    )(px, px, px, px, pv, pv, pv, px, pv, pairs, mask)


def _na_geometry(rows):
    col = np.arange(GRID_W)
    c0 = np.clip(col - NA_WIN_C // 2, 0, GRID_W - NA_WIN_C)
    in_win = (col[None, :] >= c0[:, None]) & (col[None, :] < c0[:, None] + NA_WIN_C)
    roffs, masks = [], []
    for r0, s0 in ((0, 0), (NA_QROWS, 0), (rows - NA_QROWS, rows - NA_KROWS)):
        r = r0 + np.arange(NA_QROWS)
        start = np.clip(r - NA_WIN_R // 2, 0, rows - NA_WIN_R)
        key_row = s0 + np.arange(NA_KROWS)
        valid_r = (key_row[None, :] >= start[:, None]) & (key_row[None, :] < start[:, None] + NA_WIN_R)
        roff = np.clip(key_row[None, :] - r[:, None] + (NA_WIN_R - 1), 0, 2 * NA_WIN_R - 2)
        valid = valid_r[:, None, :, None] & in_win[None, :, None, :]
        roffs.append(tuple(tuple(int(v) for v in row) for row in roff))
        masks.append(np.where(valid, 0.0, NEG_INF).reshape(NA_QB, NA_KB))
    masks.append(np.full((NA_QB, NA_KB), NEG_INF))
    return tuple(roffs), jnp.asarray(np.stack(masks), F32)


def _na_bias_pairs(rpb):
    qcol = np.arange(GRID_W)
    kcol = np.arange(2 * GRID_W) % GRID_W
    coff = np.clip(kcol[None, :] - qcol[:, None] + (NA_WIN_C - 1), 0, 2 * NA_WIN_C - 2)
    pick = (coff[:, :, None] == np.arange(2 * NA_WIN_C - 1)).astype(np.float32)
    return jnp.einsum('lhij,cdj->lhicd', rpb.astype(F32), pick, precision=lax.Precision.HIGHEST)


def _swa_kernel(q_ref, csq_ref, kv0_ref, kv1_ref, kv2_ref, kv3_ref, cs0_ref, cs1_ref, cs2_ref, cs3_ref,
                kvc_ref, mask_ref, sink_ref, rot_ref, o_ref):
    rot = rot_ref[...]
    hd = HEAD_DIM

    def rope(x, cs):
        swapped = jnp.dot(x, rot, preferred_element_type=F32)
        return x.astype(F32) * cs[:, :hd] + swapped * cs[:, hd:]

    csq = csq_ref[...]
    kv_refs = (kv0_ref, kv1_ref, kv2_ref, kv3_ref)
    cs_refs = (cs0_ref, cs1_ref, cs2_ref, cs3_ref)
    for kv in range(SWA_KV_HEADS):
        kcols = slice(kv * hd, (kv + 1) * hd)
        vcols = slice(SWA_KV_W + kv * hd, SWA_KV_W + (kv + 1) * hd)
        k_tiles = [rope(kv_refs[j][:, kcols], cs_refs[j][...]).astype(BF16) for j in range(SWA_PIECES)]
        k_tiles.append(kvc_ref[:, kcols])
        v1 = [_with_ones(kv_refs[j][:, vcols]) for j in range(SWA_PIECES)] + [_with_ones(kvc_ref[:, vcols])]
        for h in range(kv * SWA_GROUP, (kv + 1) * SWA_GROUP):
            cols = slice(h * hd, (h + 1) * hd)
            q = (rope(q_ref[:, cols], csq) * (hd ** -0.5 * LOG2E)).astype(BF16)
            s = [_dot_nt(q, k_tiles[j]) + mask_ref[:, j * SWA_BLOCK:(j + 1) * SWA_BLOCK] for j in range(SWA_PIECES)]
            s.append(_dot_nt(q, k_tiles[SWA_PIECES]))
            o_ref[:, cols] = _softmax_pv(s, v1, extra_logit=sink_ref[h] * LOG2E).astype(o_ref.dtype)


def _swa_call(px, pkv, cs_t, mask, sink_col, rot, n_lat, n_ctx, off_q, off_kv):
    kvb = off_kv // (2 * SWA_KV_W)
    assert off_kv % (2 * SWA_KV_W) == 0
    m = px.shape[0]
    blk, w = SWA_BLOCK, BRANCH_W
    nbl = n_lat // blk
    nbq = n_lat // SWA_QB
    assert n_ctx == SWA_QB and nbq >= 2
    piece = lambda b, j: jnp.clip(2 * b - 1 + j, 0, nbl - 1)
    kind = lambda b: jnp.where(b == 0, 0, jnp.where(b == nbq - 1, 2, jnp.where(b == nbq, 3, 1)))
    kv_spec = lambda j: pl.BlockSpec((blk, 2 * SWA_KV_W), lambda b: (piece(b, j), kvb))
    cs_spec = lambda j: pl.BlockSpec((blk, 2 * HEAD_DIM), lambda b: (piece(b, j), 0))
    vmem = (2 * (2 * SWA_QB * w * 2 + SWA_QB * SWA_PIECES * blk * 4 + SWA_HEADS * SWA_QB * HEAD_DIM * 4)
            + 16 * SWA_QB * (SWA_PIECES * blk + n_ctx) * 4)
    return pl.pallas_call(
        _swa_kernel,
        grid=(m // SWA_QB,),
        in_specs=[pl.BlockSpec((SWA_QB, w), lambda b: (b, off_q // w)),
                  pl.BlockSpec((SWA_QB, 2 * HEAD_DIM), lambda b: (b, 0)),
                  kv_spec(0), kv_spec(1), kv_spec(2), kv_spec(3),
                  cs_spec(0), cs_spec(1), cs_spec(2), cs_spec(3),
                  pl.BlockSpec((n_ctx, 2 * SWA_KV_W), lambda b: (n_lat // n_ctx, kvb)),
                  pl.BlockSpec((None, SWA_QB, SWA_PIECES * blk), lambda b: (kind(b), 0, 0)),
                  pl.BlockSpec((SWA_HEADS, SWA_QB, 1), lambda b: (0, 0, 0)),
                  pl.BlockSpec((HEAD_DIM, HEAD_DIM), lambda b: (0, 0))],
        out_specs=pl.BlockSpec((SWA_QB, w), lambda b: (b, 0)),
        out_shape=jax.ShapeDtypeStruct((m, w), BF16),
        compiler_params=_params(("arbitrary",), vmem),
        name="window_gqa",
    )(px, cs_t, pkv, pkv, pkv, pkv, cs_t, cs_t, cs_t, cs_t, pkv, mask, sink_col, rot)


def _rope_tables(n_lat, n_ctx):
    t = jnp.arange(n_lat, dtype=jnp.int32)
    half = HEAD_DIM // 2
    inv = ROPE_BASE ** (-jnp.arange(0, half, 2, dtype=F32) / half)
    parts_c, parts_s = [], []
    for pos in (t // GRID_W, t % GRID_W):
        ang = pos.astype(F32)[:, None] * inv[None, :]
        parts_c += [jnp.cos(ang), jnp.cos(ang)]
        parts_s += [jnp.sin(ang), jnp.sin(ang)]
    cos_t = jnp.concatenate(parts_c, axis=-1)
    sin_t = jnp.concatenate(parts_s, axis=-1)
    cos_t = jnp.concatenate([cos_t, jnp.ones((n_ctx, HEAD_DIM), F32)], axis=0)
    sin_t = jnp.concatenate([sin_t, jnp.zeros((n_ctx, HEAD_DIM), F32)], axis=0)
    quarter = half // 2
    rot = np.zeros((HEAD_DIM, HEAD_DIM), np.float32)
    for lane in range(HEAD_DIM):
        if lane % half < quarter:
            rot[lane + quarter, lane] = -1.0
        else:
            rot[lane - quarter, lane] = 1.0
    return jnp.concatenate([cos_t, sin_t], axis=1), jnp.asarray(rot, BF16)


def _swa_mask():
    a = np.arange(SWA_QB)
    j = np.arange(SWA_PIECES * SWA_BLOCK)
    in_band = np.abs((j[None, :] - SWA_BLOCK) - a[:, None]) <= SWA_WINDOW
    last = (SWA_PIECES - 1) * SWA_BLOCK
    piece_ok = {0: j >= SWA_BLOCK, 1: j >= 0, 2: j < last, 3: j < 0}
    tabs = [np.where(in_band & piece_ok[kind][None, :], 0.0, NEG_INF) for kind in range(4)]
    return jnp.asarray(np.stack(tabs), F32)


def kernel(x, c, ctx, c_ctx, w_ada, b_ada, g_mix, w_in, gmlp_ln_g, gmlp_ln_b, gmlp_ws, gmlp_bs, na_rpb,
           swa_sink, w_branch, w_out, g_ffn, w_ffn_gate, w_ffn_up, w_ffn_down, w_router, w_exp_gate,
           w_exp_up, w_exp_down, g_final):
    batch, n_lat, d = x.shape
    n_ctx = ctx.shape[1]
    depth = w_ada.shape[0]
    assert batch == 1 and c.shape[0] == 1
    assert n_lat % NA_QB == 0 and n_ctx == NA_QB and n_lat // GRID_W >= NA_KROWS
    m = n_lat + n_ctx

    off_au, off_av = 0, BRANCH_W
    off_bq, off_cq = 2 * BRANCH_W, 3 * BRANCH_W
    off_gate = 4 * BRANCH_W
    off_bk = off_gate + N_BRANCHES * d
    off_bv = off_bk + BRANCH_W
    off_ck = off_bv + BRANCH_W
    off_cv = off_ck + SWA_KV_W
    assert w_in.shape[2] == off_cv + SWA_KV_W and off_au == 0 and off_av == BRANCH_W

    ms_big, ms_small = 8, 16
    gmlp_chunks = 6 if m % (6 * CHUNK) == 0 else 2

    c8 = jnp.concatenate([c, c_ctx[None, :], jnp.zeros((6, d), F32)], axis=0)
    mods = _ada_mods(c8, w_ada, b_ada)[:, :2].reshape(depth, 2, 6, 1, d)

    cs_t, rot = _rope_tables(n_lat, n_ctx)
    swa_mask = _swa_mask()
    na_pairs = _na_bias_pairs(na_rpb)
    bs_full = jnp.repeat(jnp.swapaxes(gmlp_bs, 1, 2), CHUNK, axis=2)
    sink_col = jnp.broadcast_to(swa_sink[:, :, None, None], (depth, SWA_HEADS, SWA_QB, 1))
    w_router_p = jnp.pad(w_router, ((0, 0), (0, 0), (0, ROUTER_LANES - N_EXPERTS)))
    w_router_hi = w_router_p.astype(BF16)
    w_router_lo = (w_router_p - w_router_hi.astype(F32)).astype(BF16)
    w_router_t = jnp.stack([w_router_hi, w_router_lo], axis=1)
    n_moe, n_exp, _, fe = w_exp_gate.shape
    w_exp_down2 = w_exp_down.reshape(n_moe, n_exp * fe, d)

    for l in range(depth):
        i_layer = l // 2
        kvw = 2 * SWA_KV_W
        tail = [w_in] * ((w_in.shape[2] - off_bv) // kvw)
        tail_args = (g_mix, mods, l, 0, n_lat, tail, lambda ck: (None, ck, kvw),
                     lambda n, c: (l, c, off_bv // kvw + n), kvw, "plain8", "norm_in_proj_tail")
        if l == 0:
            h, pkv, h8, h8_scale, xs = _norm_first_call(x[0], *tail_args, ctx=ctx[0])
        else:
            h, pkv, h8, h8_scale = _norm_first_call(xs, *tail_args)
        tn_in = 1024
        puv = _mm_call(h, w_in, l, lambda c: c, off_bq // tn_in, ms_big, tn_in, "in_proj")
        p8 = _mm8_call(h8, h8_scale, w_in, l, off_bq, off_bv - off_bq, ms_big, tn_in, "in_proj_fp8")
        y_a = _gmlp_call(puv, gmlp_ln_g, gmlp_ln_b, gmlp_ws, bs_full, l, gmlp_chunks)
        y_b = _na_call(p8, pkv, na_pairs, l, n_lat, 0, off_bk - off_bq, 0)
        y_c = _swa_call(p8, pkv, cs_t, swa_mask, sink_col[l], rot, n_lat, n_ctx, off_cq - off_bq, off_ck - off_bv)
        acc = _merge_call(y_a, y_b, y_c, p8, w_branch, l, off_gate - off_bq, ms_small, 1024)
        xs = _mm_res_call([acc], w_out, l, xs, mods, l, 2, n_lat, ms_small, 1024)
        if l % 2 == 0:
            tn_up = 512
            h, hid0 = _norm_first_call(xs, g_ffn, mods, l, 3, n_lat, [w_ffn_gate, w_ffn_up],
                                       lambda ck: (None, ck, tn_up), lambda n, c: (i_layer, c, 0), tn_up,
                                       "swiglu", "norm_ffn_up")
            hid1 = _ffn_up_call(h, w_ffn_gate, w_ffn_up, i_layer, ms_big, tn_up, 1)
            xs = _mm_res_call([hid0, hid1], w_ffn_down, i_layer, xs, mods, l, 5, n_lat, ms_big, 1024)
        else:
            pair = 2
            h, hid0, gates = _norm_first_call(xs, g_ffn, mods, l, 3, n_lat, [w_exp_gate, w_exp_up],
                                              lambda ck: (None, pair, ck, fe), lambda n, c: (i_layer, 0, c, 0),
                                              pair * fe, "moe", "norm_moe_up", w_router=w_router_t[i_layer], fe=fe)
            hid1 = _moe_up_call(h, w_exp_gate, w_exp_up, gates, i_layer, ms_small, pair, 1)
            xs = _mm_res_call([hid0, hid1], w_exp_down2, i_layer, xs, mods, l, 5, n_lat, ms_big, 1024)
    return _final_norm_call(xs, g_final, n_lat)[None]
```
